```python
import jax
import jax.numpy as jnp
from jax import lax
import numpy as np

D_MODEL = 2048
BATCH = 4
SEQ = 4096
DEPTH = 2

GRID_W = 64
CTX_LEN = 256
HEAD_DIM = 128
BLOCK = 128
NA_HEADS = 4
NA_WIN_R = 8
NA_WIN_C = 16
SWA_HEADS = 4
SWA_KV_HEADS = 2
SWA_WINDOW = 128
MLA_HEADS = 4
MLA_Q_LORA = 384
MLA_KV_LORA = 128
MLA_NOPE = 128
MLA_ROPE = 64
MLA_V = 128
GQA_HEADS = 4
GQA_KV_HEADS = 2
MIX_WIDTH = (NA_HEADS + SWA_HEADS + GQA_HEADS) * HEAD_DIM + MLA_HEADS * MLA_V
D_FF = 5632
CONV_W = 3
ROPE_THETA = 10000.0
EPS = 1e-6
NEG = -1e30
DEEPNORM_ALPHA = (2 * DEPTH) ** 0.25
DEEPNORM_BETA = (8 * DEPTH) ** -0.25
IN_SIZES = (
    NA_HEADS * HEAD_DIM, NA_HEADS * HEAD_DIM, NA_HEADS * HEAD_DIM,
    SWA_HEADS * HEAD_DIM, SWA_KV_HEADS * HEAD_DIM, SWA_KV_HEADS * HEAD_DIM,
    MLA_Q_LORA, MLA_KV_LORA, MLA_ROPE,
    GQA_HEADS * HEAD_DIM, GQA_KV_HEADS * HEAD_DIM, GQA_KV_HEADS * HEAD_DIM,
)
IN_COLS = sum(IN_SIZES)

kernel_name = 'hybrid_dit_parallel_heads_deepnorm'


def layer_norm(x, g=None, b=None):
    xf = x.astype(jnp.float32)
    mu = jnp.mean(xf, axis=-1, keepdims=True)
    var = jnp.mean(jnp.square(xf - mu), axis=-1, keepdims=True)
    y = (xf - mu) * lax.rsqrt(var + EPS)
    if g is not None:
        y = y * g + b
    return y.astype(x.dtype)


def rms_norm(x, g):
    xf = x.astype(jnp.float32)
    y = xf * lax.rsqrt(jnp.mean(xf * xf, axis=-1, keepdims=True) + EPS) * g
    return y.astype(x.dtype)


def modulate(x, shift, scale):
    return layer_norm(x) * (1 + scale) + shift


def rope_1d(x, pos):
    half = x.shape[-1] // 2
    inv_freq = ROPE_THETA ** (-jnp.arange(half, dtype=jnp.float32) / half)
    ang = pos.astype(jnp.float32)[:, None] * inv_freq[None, :]
    cos, sin = jnp.cos(ang), jnp.sin(ang)
    xf = x.astype(jnp.float32)
    x1, x2 = xf[..., :half], xf[..., half:]
    return jnp.concatenate([x1 * cos - x2 * sin, x2 * cos + x1 * sin], axis=-1).astype(x.dtype)


def rope_2d(x, row, col):
    h = x.shape[-1] // 2
    return jnp.concatenate([rope_1d(x[..., :h], row), rope_1d(x[..., h:], col)], axis=-1)


def split_cols(p):
    out, o = [], 0
    for n in IN_SIZES:
        out.append(p[..., o:o + n])
        o += n
    return out


def to_heads(t, n):
    B, S, _ = t.shape
    return t.reshape(B, S, n, -1).transpose(0, 2, 1, 3)


def from_heads(o):
    B, n, S, d = o.shape
    return o.transpose(0, 2, 1, 3).reshape(B, S, n * d)


def full_attention(q, k, v, scale, sink=None):
    s = jnp.einsum('bhgqd,bhkd->bhgqk', q, k, preferred_element_type=jnp.float32) * scale
    if sink is not None:
        s_sink = jnp.broadcast_to(sink.astype(jnp.float32)[None, :, :, None, None], s.shape[:-1] + (1,))
        s = jnp.concatenate([s, s_sink], axis=-1)
    p = jax.nn.softmax(s, axis=-1)[..., :k.shape[2]]
    return jnp.einsum('bhgqk,bhkd->bhgqd', p.astype(v.dtype), v)


def mixer_neighbourhood(px, pc, rpb, need_ctx):
    q, k, v = (to_heads(t, NA_HEADS) for t in px)
    kc, vc = to_heads(pc[1], NA_HEADS), to_heads(pc[2], NA_HEADS)
    scale = HEAD_DIM ** -0.5
    B, H, S, d = q.shape
    rows = S // GRID_W
    wr = min(NA_WIN_R, rows)
    qg = q.reshape(B, H, rows, GRID_W, d)
    kg = k.reshape(B, H, rows, GRID_W, d)
    vg = v.reshape(B, H, rows, GRID_W, d)
    r = jnp.arange(rows)
    r0 = jnp.clip(r - wr // 2, 0, rows - wr)
    krow = r0[:, None] + jnp.arange(wr)[None, :]
    k_band = kg[:, :, krow]
    v_band = vg[:, :, krow]
    cq = jnp.arange(GRID_W)
    c0 = jnp.clip(cq - NA_WIN_C // 2, 0, GRID_W - NA_WIN_C)
    col_in = (cq[None, :] >= c0[:, None]) & (cq[None, :] < c0[:, None] + NA_WIN_C)
    drow_idx = krow - r[:, None] + NA_WIN_R - 1
    dcol_idx = jnp.clip(cq[None, :] - cq[:, None] + NA_WIN_C - 1, 0, 2 * NA_WIN_C - 2)
    bias = rpb[:, drow_idx[:, None, :, None], dcol_idx[None, :, None, :]]
    s = jnp.einsum('bhrqd,bhrwkd->bhrqwk', qg, k_band, preferred_element_type=jnp.float32) * scale
    s = jnp.where(col_in[:, None, :], s + bias[None], NEG)
    nwin = wr * GRID_W
    s = s.reshape(B, H, rows, GRID_W, nwin)
    s_ctx = jnp.einsum('bhrqd,bhcd->bhrqc', qg, kc, preferred_element_type=jnp.float32) * scale
    p = jax.nn.softmax(jnp.concatenate([s, s_ctx], axis=-1), axis=-1)
    o = (jnp.einsum('bhrqn,bhrnd->bhrqd', p[..., :nwin].astype(v.dtype), v_band.reshape(B, H, rows, nwin, d))
         + jnp.einsum('bhrqc,bhcd->bhrqd', p[..., nwin:].astype(v.dtype), vc))
    out_x = from_heads(o.reshape(B, H, S, d))
    out_c = None
    if need_ctx:
        qc = to_heads(pc[0], NA_HEADS)
        out_c = from_heads(full_attention(qc[:, :, None], kc, vc, scale)[:, :, 0])
    return out_x, out_c


def mixer_sliding(px, pc, row, col, sink, need_ctx):
    G = SWA_HEADS // SWA_KV_HEADS
    q = rope_2d(to_heads(px[0], SWA_HEADS), row, col)
    k = rope_2d(to_heads(px[1], SWA_KV_HEADS), row, col)
    v = to_heads(px[2], SWA_KV_HEADS)
    kc, vc = to_heads(pc[1], SWA_KV_HEADS), to_heads(pc[2], SWA_KV_HEADS)
    sink = sink.reshape(SWA_KV_HEADS, G)
    scale = HEAD_DIM ** -0.5
    B, _, S, d = q.shape
    nb = S // BLOCK
    qb = q.reshape(B, SWA_KV_HEADS, G, nb, BLOCK, d)
    pad = ((0, 0), (0, 0), (BLOCK, BLOCK), (0, 0))
    idx = jnp.arange(nb)[:, None] * BLOCK + jnp.arange(3 * BLOCK)[None, :]
    kb = jnp.pad(k, pad)[:, :, idx]
    vb = jnp.pad(v, pad)[:, :, idx]
    kpos = idx - BLOCK
    qpos = jnp.arange(S).reshape(nb, BLOCK)
    valid = ((jnp.abs(kpos[:, None, :] - qpos[:, :, None]) <= SWA_WINDOW)
             & (kpos >= 0)[:, None, :] & (kpos < S)[:, None, :])
    s = jnp.einsum('bhgnqd,bhnkd->bhgnqk', qb, kb, preferred_element_type=jnp.float32) * scale
    s = jnp.where(valid, s, NEG)
    s_ctx = jnp.einsum('bhgnqd,bhcd->bhgnqc', qb, kc, preferred_element_type=jnp.float32) * scale
    s_sink = jnp.broadcast_to(sink.astype(jnp.float32)[None, :, :, None, None, None], s.shape[:-1] + (1,))
    p = jax.nn.softmax(jnp.concatenate([s, s_ctx, s_sink], axis=-1), axis=-1)
    nk, C = 3 * BLOCK, kc.shape[2]
    o = (jnp.einsum('bhgnqk,bhnkd->bhgnqd', p[..., :nk].astype(v.dtype), vb)
         + jnp.einsum('bhgnqc,bhcd->bhgnqd', p[..., nk:nk + C].astype(v.dtype), vc))
    out_x = from_heads(o.reshape(B, SWA_HEADS, S, d))
    out_c = None
    if need_ctx:
        qc = to_heads(pc[0], SWA_HEADS)
        qc = qc.reshape(B, SWA_KV_HEADS, G, qc.shape[2], d)
        oc = full_attention(qc, kc, vc, scale, sink)
        out_c = from_heads(oc.reshape(B, SWA_HEADS, qc.shape[3], d))
    return out_x, out_c


def mixer_mla(px, pc, row, col, q_norm, kv_norm, w_uq, w_ukv, need_ctx):
    def proj_q(cq):
        qh = to_heads(rms_norm(cq, q_norm) @ w_uq, MLA_HEADS)
        return qh[..., :MLA_NOPE], qh[..., MLA_NOPE:]

    def proj_kv(ckv):
        kvh = to_heads(rms_norm(ckv, kv_norm) @ w_ukv, MLA_HEADS)
        return kvh[..., :MLA_NOPE], kvh[..., MLA_NOPE:]

    scale = (MLA_NOPE + MLA_ROPE) ** -0.5

    def attend(qn, qpe, kn, kpe, v):
        s = (jnp.einsum('bhqd,bhkd->bhqk', qn, kn, preferred_element_type=jnp.float32)
             + jnp.einsum('bhqr,bkr->bhqk', qpe, kpe, preferred_element_type=jnp.float32)) * scale
        p = jax.nn.softmax(s, axis=-1)
        return jnp.einsum('bhqk,bhkd->bhqd', p.astype(v.dtype), v)

    qn, qpe = proj_q(px[0])
    qpe = rope_2d(qpe, row, col)
    kn, v = proj_kv(px[1])
    kpe = rope_2d(px[2], row, col)
    kn_c, v_c = proj_kv(pc[1])
    kpe_c = pc[2]
    kn_all = jnp.concatenate([kn, kn_c], axis=2)
    kpe_all = jnp.concatenate([kpe, kpe_c], axis=1)
    v_all = jnp.concatenate([v, v_c], axis=2)
    B, H, S, _ = qn.shape
    nb = S // BLOCK
    blocks = lambda t: jnp.moveaxis(t.reshape(B, H, nb, BLOCK, t.shape[-1]), 2, 0)
    o = lax.map(lambda qs: attend(qs[0], qs[1], kn_all, kpe_all, v_all), (blocks(qn), blocks(qpe)))
    out_x = from_heads(jnp.moveaxis(o, 0, 2).reshape(B, H, S, MLA_V))
    out_c = None
    if need_ctx:
        qn_c, qpe_c = proj_q(pc[0])
        out_c = from_heads(attend(qn_c, qpe_c, kn_c, kpe_c, v_c))
    return out_x, out_c


def mixer_gqa(px, pc, row, col, q_norm, k_norm, need_ctx):
    G = GQA_HEADS // GQA_KV_HEADS
    q = rope_2d(rms_norm(to_heads(px[0], GQA_HEADS), q_norm), row, col)
    k = rope_2d(rms_norm(to_heads(px[1], GQA_KV_HEADS), k_norm), row, col)
    v = to_heads(px[2], GQA_KV_HEADS)
    kc = rms_norm(to_heads(pc[1], GQA_KV_HEADS), k_norm)
    vc = to_heads(pc[2], GQA_KV_HEADS)
    k_all = jnp.concatenate([k, kc], axis=2)
    v_all = jnp.concatenate([v, vc], axis=2)
    scale = HEAD_DIM ** -0.5
    B, _, S, d = q.shape
    nb = S // BLOCK
    qb = jnp.moveaxis(q.reshape(B, GQA_KV_HEADS, G, nb, BLOCK, d), 3, 0)
    o = lax.map(lambda qi: full_attention(qi, k_all, v_all, scale), qb)
    out_x = from_heads(jnp.moveaxis(o, 0, 3).reshape(B, GQA_HEADS, S, d))
    out_c = None
    if need_ctx:
        qc = rms_norm(to_heads(pc[0], GQA_HEADS), q_norm)
        C = qc.shape[2]
        oc = full_attention(qc.reshape(B, GQA_KV_HEADS, G, C, d), kc, vc, scale)
        out_c = from_heads(oc.reshape(B, GQA_HEADS, C, d))
    return out_x, out_c


def depthwise_conv(h, w, b):
    S = h.shape[1]
    hp = jnp.pad(h, ((0, 0), (CONV_W // 2, CONV_W // 2), (0, 0)))
    return hp[:, 0:S] * w[0] + hp[:, 1:S + 1] * w[1] + hp[:, 2:S + 2] * w[2] + b


def conv_ffn(h, w_gate, w_up, conv_w, conv_b, w_down):
    a = depthwise_conv(h @ w_gate, conv_w, conv_b)
    return (jax.nn.silu(a) * (h @ w_up)) @ w_down


def setup_inputs(seed: int = 0) -> dict:
    key = jax.random.key(seed)
    ks = jax.random.split(key, 25)
    L, D = DEPTH, D_MODEL

    def nrm(k, shape, s):
        return jax.random.normal(k, shape, jnp.float32) * s

    return {
        'x': nrm(ks[0], (BATCH, SEQ, D), 1.0),
        'c': nrm(ks[1], (BATCH, D), 1.0),
        'ctx': nrm(ks[2], (BATCH, CTX_LEN, D), 1.0),
        'c_ctx': nrm(ks[3], (D,), 1.0),
        'w_ada': nrm(ks[4], (L, D, 6 * D), D ** -0.5),
        'b_ada': nrm(ks[5], (L, 6 * D), 0.02),
        'w_in': nrm(ks[6], (L, D, IN_COLS), D ** -0.5),
        'na_rpb': nrm(ks[7], (L, NA_HEADS, 2 * NA_WIN_R - 1, 2 * NA_WIN_C - 1), 0.5),
        'swa_sink': nrm(ks[8], (L, SWA_HEADS), 0.5),
        'mla_q_norm': 1.0 + nrm(ks[9], (L, MLA_Q_LORA), 0.05),
        'mla_kv_norm': 1.0 + nrm(ks[10], (L, MLA_KV_LORA), 0.05),
        'mla_w_uq': nrm(ks[11], (L, MLA_Q_LORA, MLA_HEADS * (MLA_NOPE + MLA_ROPE)), MLA_Q_LORA ** -0.5),
        'mla_w_ukv': nrm(ks[12], (L, MLA_KV_LORA, MLA_HEADS * (MLA_NOPE + MLA_V)), MLA_KV_LORA ** -0.5),
        'gqa_q_norm': 1.0 + nrm(ks[13], (L, HEAD_DIM), 0.05),
        'gqa_k_norm': 1.0 + nrm(ks[14], (L, HEAD_DIM), 0.05),
        'w_out': nrm(ks[15], (L, MIX_WIDTH, D), DEEPNORM_BETA * MIX_WIDTH ** -0.5),
        'ln1_g': 1.0 + nrm(ks[16], (L, D), 0.05),
        'ln1_b': nrm(ks[17], (L, D), 0.02),
        'ffn_w_gate': nrm(ks[18], (L, D, D_FF), D ** -0.5),
        'ffn_w_up': nrm(ks[19], (L, D, D_FF), D ** -0.5),
        'ffn_conv_w': nrm(ks[20], (L, CONV_W, D_FF), CONV_W ** -0.5),
        'ffn_conv_b': nrm(ks[21], (L, D_FF), 0.02),
        'ffn_w_down': nrm(ks[22], (L, D_FF, D), DEEPNORM_BETA * D_FF ** -0.5),
        'ln2_g': 1.0 + nrm(ks[23], (L, D), 0.05),
        'ln2_b': nrm(ks[24], (L, D), 0.02),
    }


def reference(x, c, ctx, c_ctx, w_ada, b_ada, w_in, na_rpb, swa_sink, mla_q_norm, mla_kv_norm,
              mla_w_uq, mla_w_ukv, gqa_q_norm, gqa_k_norm, w_out, ln1_g, ln1_b,
              ffn_w_gate, ffn_w_up, ffn_conv_w, ffn_conv_b, ffn_w_down, ln2_g, ln2_b):
    t = jnp.arange(x.shape[1])
    row, col = t // GRID_W, t % GRID_W
    for i in range(DEPTH):
        need_ctx = i < DEPTH - 1
        mod_x = jnp.split((jax.nn.silu(c) @ w_ada[i] + b_ada[i])[:, None, :], 6, axis=-1)
        mod_c = jnp.split(jax.nn.silu(c_ctx) @ w_ada[i] + b_ada[i], 6, axis=-1)

        px = split_cols(modulate(x, mod_x[0], mod_x[1]) @ w_in[i])
        pc = split_cols(modulate(ctx, mod_c[0], mod_c[1]) @ w_in[i])
        oa_x, oa_c = mixer_neighbourhood(px[0:3], pc[0:3], na_rpb[i], need_ctx)
        ob_x, ob_c = mixer_sliding(px[3:6], pc[3:6], row, col, swa_sink[i], need_ctx)
        oc_x, oc_c = mixer_mla(px[6:9], pc[6:9], row, col, mla_q_norm[i], mla_kv_norm[i],
                               mla_w_uq[i], mla_w_ukv[i], need_ctx)
        od_x, od_c = mixer_gqa(px[9:12], pc[9:12], row, col, gqa_q_norm[i], gqa_k_norm[i], need_ctx)
        mix_x = jnp.concatenate([oa_x, ob_x, oc_x, od_x], axis=-1)
        x = layer_norm(DEEPNORM_ALPHA * x + mod_x[2] * (mix_x @ w_out[i]), ln1_g[i], ln1_b[i])
        if need_ctx:
            mix_c = jnp.concatenate([oa_c, ob_c, oc_c, od_c], axis=-1)
            ctx = layer_norm(DEEPNORM_ALPHA * ctx + mod_c[2] * (mix_c @ w_out[i]), ln1_g[i], ln1_b[i])

        ffn_args = (ffn_w_gate[i], ffn_w_up[i], ffn_conv_w[i], ffn_conv_b[i], ffn_w_down[i])
        x = layer_norm(DEEPNORM_ALPHA * x + mod_x[5] * conv_ffn(modulate(x, mod_x[3], mod_x[4]), *ffn_args),
                       ln2_g[i], ln2_b[i])
        if need_ctx:
            ctx = layer_norm(DEEPNORM_ALPHA * ctx + mod_c[5] * conv_ffn(modulate(ctx, mod_c[3], mod_c[4]), *ffn_args),
                             ln2_g[i], ln2_b[i])
    return x
```

```python
import functools

import numpy as np
import jax
import jax.numpy as jnp
from jax import lax
from jax.experimental import pallas as pl
from jax.experimental.pallas import tpu as pltpu

GRID_W = 64
HEAD_DIM = 128
NA_HEADS = 4
NA_WIN_R = 8
NA_WIN_C = 16
SWA_HEADS = 4
SWA_KV_HEADS = 2
SWA_WINDOW = 128
MLA_HEADS = 4
MLA_Q_LORA = 384
MLA_KV_LORA = 128
MLA_NOPE = 128
MLA_ROPE = 64
MLA_V = 128
GQA_HEADS = 4
GQA_KV_HEADS = 2
CONV_W = 3
ROPE_THETA = 10000.0
EPS = 1e-6
NEG = -1e30

LANE = 128
BF16_SUBLANES = 16
VMEM_LIMIT = 56 * 1024 * 1024

F32 = jnp.float32
BF16 = jnp.bfloat16

NA_KEY_ROWS = NA_WIN_R + 1
NA_Q_ROWS = 2


def _cparams(*sem):
    return pltpu.CompilerParams(dimension_semantics=sem, vmem_limit_bytes=VMEM_LIMIT)


def _dot(a, b):
    return jnp.dot(a, b, preferred_element_type=F32)


def _dot_nt(a, b):
    return lax.dot_general(a, b, (((1,), (1,)), ((), ())), preferred_element_type=F32)


def _layer_norm(x):
    mu = jnp.mean(x, axis=-1, keepdims=True)
    xc = x - mu
    var = jnp.mean(xc * xc, axis=-1, keepdims=True)
    return xc * lax.rsqrt(var + EPS)


def _rms_norm(x, g):
    return x * lax.rsqrt(jnp.mean(x * x, axis=-1, keepdims=True) + EPS) * g


def _rope(x, cos, sin_lo, sin_hi, half):
    n = x.shape[-1]
    return x * cos + pltpu.roll(x, n - half, 1) * sin_lo + pltpu.roll(x, half, 1) * sin_hi


def _ada_kernel(c_ref, w_ref, b_ref, o_ref):
    c = c_ref[...]
    a = (c * jax.nn.sigmoid(c)).astype(BF16)
    o_ref[0] = _dot(a, w_ref[0].astype(BF16)) + b_ref[0]


def _ada(cvec, w_ada, b_ada):
    L, D, N = w_ada.shape
    M = cvec.shape[0]
    tn = 1024
    return pl.pallas_call(
        _ada_kernel,
        out_shape=jax.ShapeDtypeStruct((L, M, N), F32),
        grid=(L, N // tn),
        in_specs=[
            pl.BlockSpec((M, D), lambda l, j: (0, 0)),
            pl.BlockSpec((1, D, tn), lambda l, j: (l, 0, j)),
            pl.BlockSpec((1, 1, tn), lambda l, j: (l, 0, j)),
        ],
        out_specs=pl.BlockSpec((1, M, tn), lambda l, j: (l, 0, j)),
        compiler_params=_cparams("parallel", "parallel"),
        name="ada",
    )(cvec, w_ada, b_ada.reshape(L, 1, N))


_A0 = 0
_B0 = _A0 + 3 * NA_HEADS * HEAD_DIM
_C0 = _B0 + (SWA_HEADS + 2 * SWA_KV_HEADS) * HEAD_DIM
_C_W = MLA_Q_LORA + MLA_KV_LORA + LANE
_D0 = _C0 + _C_W
_D_W = (GQA_HEADS + 2 * GQA_KV_HEADS) * HEAD_DIM
IN_COLS_PAD = _D0 + _D_W
MLA_QK = 2 * LANE


def _proj_kernel(x_ref, shift_ref, scale_ref, w_ref, rope_ref, gq_lora_ref, gkv_lora_ref, gq_ref, gk_ref,
                 wuq_ref, wukv_ref,
                 qa_ref, ka_ref, va_ref, qb_ref, kb_ref, vb_ref, qc_ref, kc_ref, vc_ref, qd_ref, kd_ref, vd_ref):
    h = (_layer_norm(x_ref[0]) * (1.0 + scale_ref[0]) + shift_ref[0]).astype(BF16)
    cos2, slo2, shi2 = rope_ref[0], rope_ref[1], rope_ref[2]
    cos1, slo1, shi1 = rope_ref[3], rope_ref[4], rope_ref[5]
    rope2 = lambda t: _rope(t, cos2, slo2, shi2, HEAD_DIM // 4)
    rope1 = lambda t: _rope(t, cos1, slo1, shi1, MLA_ROPE // 4)
    hd = HEAD_DIM
    sc = hd ** -0.5

    w = NA_HEADS * hd
    pa = _dot(h, w_ref[:, _A0:_A0 + 3 * w])
    qa_ref[0] = (pa[:, :w] * sc).astype(BF16)
    ka_ref[0] = pa[:, w:2 * w].astype(BF16)
    va_ref[0] = pa[:, 2 * w:].astype(BF16)

    wq, wk = SWA_HEADS * hd, SWA_KV_HEADS * hd
    pb = _dot(h, w_ref[:, _B0:_B0 + wq + 2 * wk])
    for i in range(SWA_HEADS):
        qb_ref[0, :, i * hd:(i + 1) * hd] = (rope2(pb[:, i * hd:(i + 1) * hd]) * sc).astype(BF16)
    for i in range(SWA_KV_HEADS):
        kb_ref[0, :, i * hd:(i + 1) * hd] = rope2(pb[:, wq + i * hd:wq + (i + 1) * hd]).astype(BF16)
    vb_ref[0] = pb[:, wq + wk:].astype(BF16)

    pc = _dot(h, w_ref[:, _C0:_C0 + _C_W])
    cq = _rms_norm(pc[:, :MLA_Q_LORA], gq_lora_ref[...]).astype(BF16)
    ckv = _rms_norm(pc[:, MLA_Q_LORA:MLA_Q_LORA + MLA_KV_LORA], gkv_lora_ref[...]).astype(BF16)
    kpe = rope1(pc[:, MLA_Q_LORA + MLA_KV_LORA:]).astype(BF16)
    qup = _dot(cq, wuq_ref[...])
    kvup = _dot(ckv, wukv_ref[...])
    sc_mla = (MLA_NOPE + MLA_ROPE) ** -0.5
    for i in range(MLA_HEADS):
        o = i * MLA_QK
        qc_ref[0, :, o:o + LANE] = (qup[:, o:o + LANE] * sc_mla).astype(BF16)
        qc_ref[0, :, o + LANE:o + 2 * LANE] = (rope1(qup[:, o + LANE:o + 2 * LANE]) * sc_mla).astype(BF16)
        kc_ref[0, :, o:o + LANE] = kvup[:, o:o + LANE].astype(BF16)
        kc_ref[0, :, o + LANE:o + 2 * LANE] = kpe
        vc_ref[0, :, i * MLA_V:(i + 1) * MLA_V] = kvup[:, o + LANE:o + 2 * LANE].astype(BF16)

    wq, wk = GQA_HEADS * hd, GQA_KV_HEADS * hd
    pd = _dot(h, w_ref[:, _D0:_D0 + wq + 2 * wk])
    for i in range(GQA_HEADS):
        t = _rms_norm(pd[:, i * hd:(i + 1) * hd], gq_ref[...])
        qd_ref[0, :, i * hd:(i + 1) * hd] = (rope2(t) * sc).astype(BF16)
    for i in range(GQA_KV_HEADS):
        t = _rms_norm(pd[:, wq + i * hd:wq + (i + 1) * hd], gk_ref[...])
        kd_ref[0, :, i * hd:(i + 1) * hd] = rope2(t).astype(BF16)
    vd_ref[0] = pd[:, wq + wk:].astype(BF16)


def _proj(x, shift, scale, w_in_p, rope_tab, lw, tm):
    B, S, D = x.shape
    widths = (NA_HEADS * HEAD_DIM,) * 3 + (SWA_HEADS * HEAD_DIM, SWA_KV_HEADS * HEAD_DIM, SWA_KV_HEADS * HEAD_DIM) \
        + (MLA_HEADS * MLA_QK, MLA_HEADS * MLA_QK, MLA_HEADS * MLA_V) \
        + (GQA_HEADS * HEAD_DIM, GQA_KV_HEADS * HEAD_DIM, GQA_KV_HEADS * HEAD_DIM)
    const = lambda shape: pl.BlockSpec(shape, lambda b, i: (0,) * len(shape), pipeline_mode=pl.Buffered(1))
    vec = pl.BlockSpec((1, 1, D), lambda b, i: (b, 0, 0))
    return pl.pallas_call(
        _proj_kernel,
        out_shape=[jax.ShapeDtypeStruct((B, S, n), BF16) for n in widths],
        grid=(B, S // tm),
        in_specs=[
            pl.BlockSpec((1, tm, D), lambda b, i: (b, i, 0)),
            vec, vec,
            const((D, IN_COLS_PAD)),
            pl.BlockSpec((6, tm, LANE), lambda b, i: (0, i, 0)),
            const((1, MLA_Q_LORA)), const((1, MLA_KV_LORA)), const((1, HEAD_DIM)), const((1, HEAD_DIM)),
            const((MLA_Q_LORA, MLA_HEADS * MLA_QK)), const((MLA_KV_LORA, MLA_HEADS * MLA_QK)),
        ],
        out_specs=[pl.BlockSpec((1, tm, n), lambda b, i: (b, i, 0)) for n in widths],
        compiler_params=_cparams("parallel", "parallel"),
        name="proj",
    )(x, shift, scale, w_in_p, rope_tab, lw["mla_q_norm"], lw["mla_kv_norm"], lw["gqa_q_norm"], lw["gqa_k_norm"],
      lw["w_uq"], lw["w_ukv"])


def _softmax_pv(parts, sink=None):
    m = parts[0][0].max(axis=-1, keepdims=True)
    for s, _ in parts[1:]:
        m = jnp.maximum(m, s.max(axis=-1, keepdims=True))
    if sink is not None:
        m = jnp.maximum(m, sink)
    l = jnp.exp(sink - m) if sink is not None else 0.0
    acc = None
    for s, v in parts:
        p = jnp.exp(s - m)
        l = l + p.sum(axis=-1, keepdims=True)
        pv = _dot(p.astype(BF16), v)
        acc = pv if acc is None else acc + pv
    return acc / l


def _na_kernel(cls_ref, ws_ref, q_ref, k_ref, v_ref, kc_ref, vc_ref, bias_ref, o_ref):
    del cls_ref
    nk = NA_KEY_ROWS * GRID_W
    start = pl.multiple_of(ws_ref[pl.program_id(2)] * GRID_W, GRID_W)
    q = q_ref[0]
    s = _dot_nt(q, k_ref[0, pl.ds(start, nk), :]) + bias_ref[0, 0]
    s_ctx = _dot_nt(q, kc_ref[0])
    o_ref[0] = _softmax_pv([(s, v_ref[0, pl.ds(start, nk), :]), (s_ctx, vc_ref[0])]).astype(o_ref.dtype)


def _na_tables(rpb, rows):
    nt = rows // NA_Q_ROWS
    R = np.arange(nt) * NA_Q_ROWS
    ws = np.clip(R - NA_WIN_R // 2, 0, rows - NA_KEY_ROWS)
    qr = (R[:, None] + np.arange(NA_Q_ROWS)[None, :])
    kr = ws[:, None] + np.arange(NA_KEY_ROWS)[None, :]
    r0 = np.clip(qr - NA_WIN_R // 2, 0, rows - NA_WIN_R)
    row_ok = (kr[:, None, :] >= r0[:, :, None]) & (kr[:, None, :] < r0[:, :, None] + NA_WIN_R)
    drow = np.clip(kr[:, None, :] - qr[:, :, None] + NA_WIN_R - 1, 0, 2 * NA_WIN_R - 2)
    geom = np.concatenate([row_ok.reshape(nt, -1).astype(np.int64), drow.reshape(nt, -1)], axis=1)
    uniq, first, cls = np.unique(geom, axis=0, return_index=True, return_inverse=True)
    cq = np.arange(GRID_W)
    c0 = np.clip(cq - NA_WIN_C // 2, 0, GRID_W - NA_WIN_C)
    col_ok = (cq[None, :] >= c0[:, None]) & (cq[None, :] < c0[:, None] + NA_WIN_C)
    dcol = np.clip(cq[None, :] - cq[:, None] + NA_WIN_C - 1, 0, 2 * NA_WIN_C - 2)
    nc = len(first)
    ok = row_ok[first][:, :, None, :, None] & col_ok[None, None, :, None, :]
    di = np.broadcast_to(drow[first][:, :, None, :, None], ok.shape)
    dj = np.broadcast_to(dcol[None, None, :, None, :], ok.shape)
    nq, nk = NA_Q_ROWS * GRID_W, NA_KEY_ROWS * GRID_W
    bias = rpb[:, di.reshape(nc, nq, nk), dj.reshape(nc, nq, nk)]
    bias = jnp.where(ok.reshape(nc, nq, nk)[None], bias, NEG).transpose(1, 0, 2, 3)
    return bias.astype(F32), jnp.asarray(cls.reshape(-1), jnp.int32), jnp.asarray(ws, jnp.int32)


def _na_attention(q, k, v, kc, vc, rpb):
    B, S, _ = q.shape
    C = kc.shape[1]
    rows = S // GRID_W
    assert rows >= NA_KEY_ROWS and rows % NA_Q_ROWS == 0
    bias, cls, ws = _na_tables(rpb, rows)
    tq, nk, hd = NA_Q_ROWS * GRID_W, NA_KEY_ROWS * GRID_W, HEAD_DIM
    grid_spec = pltpu.PrefetchScalarGridSpec(
        num_scalar_prefetch=2,
        grid=(B, NA_HEADS, S // tq),
        in_specs=[
            pl.BlockSpec((1, tq, hd), lambda b, h, i, c, w: (b, i, h)),
            pl.BlockSpec((1, S, hd), lambda b, h, i, c, w: (b, 0, h)),
            pl.BlockSpec((1, S, hd), lambda b, h, i, c, w: (b, 0, h)),
            pl.BlockSpec((1, C, hd), lambda b, h, i, c, w: (b, 0, h)),
            pl.BlockSpec((1, C, hd), lambda b, h, i, c, w: (b, 0, h)),
            pl.BlockSpec((1, 1, tq, nk), lambda b, h, i, c, w: (c[i], h, 0, 0)),
        ],
        out_specs=pl.BlockSpec((1, tq, hd), lambda b, h, i, c, w: (b, i, h)),
    )
    return pl.pallas_call(
        _na_kernel,
        out_shape=jax.ShapeDtypeStruct((B, S, NA_HEADS * hd), BF16),
        grid_spec=grid_spec,
        compiler_params=_cparams("parallel", "parallel", "arbitrary"),
        name="na_attn",
    )(cls, ws, q, k, v, kc, vc, bias)


def _swa_kernel(q_ref, k_ref, v_ref, kc_ref, vc_ref, sink_ref, o_ref, *, tq, seq):
    nk = tq + 2 * SWA_WINDOW
    i = pl.program_id(2)
    start = pl.multiple_of(jnp.clip(i * tq - SWA_WINDOW, 0, seq - nk), SWA_WINDOW)
    q = q_ref[0]
    s = _dot_nt(q, k_ref[0, pl.ds(start, nk), :])
    qpos = i * tq + lax.broadcasted_iota(jnp.int32, (tq, nk), 0)
    kpos = start + lax.broadcasted_iota(jnp.int32, (tq, nk), 1)
    s = jnp.where(jnp.abs(qpos - kpos) <= SWA_WINDOW, s, NEG)
    s_ctx = _dot_nt(q, kc_ref[0])
    o = _softmax_pv([(s, v_ref[0, pl.ds(start, nk), :]), (s_ctx, vc_ref[0])], sink=sink_ref[0])
    o_ref[0] = o.astype(o_ref.dtype)


def _swa_attention(q, k, v, kc, vc, sink, tq):
    B, S, _ = q.shape
    C = kc.shape[1]
    hd, G = HEAD_DIM, SWA_HEADS // SWA_KV_HEADS
    assert S >= tq + 2 * SWA_WINDOW and S % tq == 0 and tq % SWA_WINDOW == 0
    return pl.pallas_call(
        functools.partial(_swa_kernel, tq=tq, seq=S),
        out_shape=jax.ShapeDtypeStruct((B, S, SWA_HEADS * hd), BF16),
        grid=(B, SWA_HEADS, S // tq),
        in_specs=[
            pl.BlockSpec((1, tq, hd), lambda b, h, i: (b, i, h)),
            pl.BlockSpec((1, S, hd), lambda b, h, i: (b, 0, h // G)),
            pl.BlockSpec((1, S, hd), lambda b, h, i: (b, 0, h // G)),
            pl.BlockSpec((1, C, hd), lambda b, h, i: (b, 0, h // G)),
            pl.BlockSpec((1, C, hd), lambda b, h, i: (b, 0, h // G)),
            pl.BlockSpec((1, 1, 1), lambda b, h, i: (h, 0, 0)),
        ],
        out_specs=pl.BlockSpec((1, tq, hd), lambda b, h, i: (b, i, h)),
        compiler_params=_cparams("parallel", "parallel", "arbitrary"),
        name="swa_attn",
    )(q, k, v, kc, vc, sink.reshape(SWA_HEADS, 1, 1).astype(F32))


def _dense_kernel(q_ref, k_ref, v_ref, kc_ref, vc_ref, o_ref, m_ref, l_ref, acc_ref, *, tk, seq):
    q = q_ref[0]
    m_ref[...] = jnp.full(m_ref.shape, -jnp.inf, F32)
    l_ref[...] = jnp.zeros(l_ref.shape, F32)
    acc_ref[...] = jnp.zeros(acc_ref.shape, F32)

    def update(k, v):
        s = _dot_nt(q, k)
        m_prev = m_ref[...]
        m_new = jnp.maximum(m_prev, s.max(axis=-1, keepdims=True))
        alpha = jnp.exp(m_prev - m_new)
        p = jnp.exp(s - m_new)
        l_ref[...] = alpha * l_ref[...] + p.sum(axis=-1, keepdims=True)
        acc_ref[...] = alpha * acc_ref[...] + _dot(p.astype(BF16), v)
        m_ref[...] = m_new

    def body(j, carry):
        start = pl.multiple_of(j * tk, tk)
        update(k_ref[0, pl.ds(start, tk), :], v_ref[0, pl.ds(start, tk), :])
        return carry

    lax.fori_loop(0, seq // tk, body, 0)
    update(kc_ref[0], vc_ref[0])
    o_ref[0] = (acc_ref[...] / l_ref[...]).astype(o_ref.dtype)


def _dense_attention(q, k, v, kc, vc, n_heads, group, tq, tk):
    B, S, _ = q.shape
    C = kc.shape[1]
    dk = q.shape[2] // n_heads
    dv = v.shape[2] // (n_heads // group)
    assert S % tq == 0 and S % tk == 0
    return pl.pallas_call(
        functools.partial(_dense_kernel, tk=tk, seq=S),
        out_shape=jax.ShapeDtypeStruct((B, S, n_heads * dv), BF16),
        grid=(B, n_heads, S // tq),
        in_specs=[
            pl.BlockSpec((1, tq, dk), lambda b, h, i: (b, i, h)),
            pl.BlockSpec((1, S, dk), lambda b, h, i: (b, 0, h // group)),
            pl.BlockSpec((1, S, dv), lambda b, h, i: (b, 0, h // group)),
            pl.BlockSpec((1, C, dk), lambda b, h, i: (b, 0, h // group)),
            pl.BlockSpec((1, C, dv), lambda b, h, i: (b, 0, h // group)),
        ],
        out_specs=pl.BlockSpec((1, tq, dv), lambda b, h, i: (b, i, h)),
        scratch_shapes=[pltpu.VMEM((tq, 1), F32), pltpu.VMEM((tq, 1), F32), pltpu.VMEM((tq, dv), F32)],
        compiler_params=_cparams("parallel", "parallel", "arbitrary"),
        name="dense_attn",
    )(q, k, v, kc, vc)


def _ctx_kernel(q_ref, k_ref, v_ref, sink_ref, o_ref):
    s = _dot_nt(q_ref[0], k_ref[0])
    o_ref[0] = _softmax_pv([(s, v_ref[0])], sink=sink_ref[0]).astype(o_ref.dtype)


def _ctx_attention(q, k, v, sink, n_heads, group):
    B, C, _ = q.shape
    dk = q.shape[2] // n_heads
    dv = v.shape[2] // (n_heads // group)
    return pl.pallas_call(
        _ctx_kernel,
        out_shape=jax.ShapeDtypeStruct((B, C, n_heads * dv), BF16),
        grid=(B, n_heads),
        in_specs=[
            pl.BlockSpec((1, C, dk), lambda b, h: (b, 0, h)),
            pl.BlockSpec((1, C, dk), lambda b, h: (b, 0, h // group)),
            pl.BlockSpec((1, C, dv), lambda b, h: (b, 0, h // group)),
            pl.BlockSpec((1, 1, 1), lambda b, h: (h, 0, 0)),
        ],
        out_specs=pl.BlockSpec((1, C, dv), lambda b, h: (b, 0, h)),
        compiler_params=_cparams("parallel", "parallel"),
        name="ctx_attn",
    )(q, k, v, sink.reshape(n_heads, 1, 1).astype(F32))


def _out_kernel(oa_ref, ob_ref, oc_ref, od_ref, w_ref, x_ref, gate_ref, g_ref, b_ref, o_ref, *, alpha):
    y = None
    o = 0
    for r in (oa_ref, ob_ref, oc_ref, od_ref):
        n = r.shape[2]
        t = _dot(r[0], w_ref[o:o + n, :])
        y = t if y is None else y + t
        o += n
    z = alpha * x_ref[0] + gate_ref[0] * y
    o_ref[0] = _layer_norm(z) * g_ref[...] + b_ref[...]


def _out_proj(mix, w_out, x, gate, g, b, alpha, tm):
    B, S, D = x.shape
    const = lambda shape: pl.BlockSpec(shape, lambda bb, i: (0,) * len(shape), pipeline_mode=pl.Buffered(1))
    return pl.pallas_call(
        functools.partial(_out_kernel, alpha=alpha),
        out_shape=jax.ShapeDtypeStruct((B, S, D), F32),
        grid=(B, S // tm),
        in_specs=[pl.BlockSpec((1, tm, m.shape[2]), lambda bb, i: (bb, i, 0)) for m in mix] + [
            const(w_out.shape),
            pl.BlockSpec((1, tm, D), lambda bb, i: (bb, i, 0)),
            pl.BlockSpec((1, 1, D), lambda bb, i: (bb, 0, 0)),
            const((1, D)), const((1, D)),
        ],
        out_specs=pl.BlockSpec((1, tm, D), lambda bb, i: (bb, i, 0)),
        compiler_params=_cparams("parallel", "parallel"),
        name="out_proj",
    )(*mix, w_out, x, gate, g, b)


HALO = BF16_SUBLANES


def _ffn_kernel(x_ref, xp_ref, xn_ref, shift_ref, scale_ref, gate_ref, wg_ref, wu_ref, cw_ref, cb_ref, wd_ref,
                g_ref, b_ref, o_ref, h_ref, acc_ref, *, alpha, tm):
    i, f = pl.program_id(1), pl.program_id(2)

    @pl.when(f == 0)
    def _():
        mod = lambda t: _layer_norm(t) * (1.0 + scale_ref[0]) + shift_ref[0]
        keep_p = (i > 0).astype(F32)
        keep_n = (i < pl.num_programs(1) - 1).astype(F32)
        h_ref[0:HALO, :] = (mod(xp_ref[0]) * keep_p).astype(BF16)
        h_ref[HALO:HALO + tm, :] = mod(x_ref[0]).astype(BF16)
        h_ref[HALO + tm:, :] = (mod(xn_ref[0]) * keep_n).astype(BF16)
        acc_ref[...] = jnp.zeros(acc_ref.shape, F32)

    n = tm + 2 * HALO
    gt = _dot(h_ref[...], wg_ref[...])
    up = _dot(h_ref[HALO:HALO + tm, :], wu_ref[...])
    a = (pltpu.roll(gt, 1, 0) * cw_ref[0:1, :] + gt * cw_ref[1:2, :]
         + pltpu.roll(gt, n - 1, 0) * cw_ref[2:3, :] + cb_ref[...])[HALO:HALO + tm]
    y = (a * jax.nn.sigmoid(a) * up).astype(BF16)
    acc_ref[...] += _dot(y, wd_ref[...])

    @pl.when(f == pl.num_programs(2) - 1)
    def _():
        z = alpha * x_ref[0] + gate_ref[0] * acc_ref[...]
        o_ref[0] = _layer_norm(z) * g_ref[...] + b_ref[...]


def _ffn(x, shift, scale, gate, wg, wu, cw, cb, wd, g, b, alpha, tm, tf):
    B, S, D = x.shape
    F = wg.shape[1]
    assert S % tm == 0 and F % tf == 0 and tm % HALO == 0
    nh = tm // HALO
    vec = pl.BlockSpec((1, 1, D), lambda bb, i, f: (bb, 0, 0))
    const = pl.BlockSpec((1, D), lambda bb, i, f: (0, 0))
    return pl.pallas_call(
        functools.partial(_ffn_kernel, alpha=alpha, tm=tm),
        out_shape=jax.ShapeDtypeStruct((B, S, D), F32),
        grid=(B, S // tm, F // tf),
        in_specs=[
            pl.BlockSpec((1, tm, D), lambda bb, i, f: (bb, i, 0)),
            pl.BlockSpec((1, HALO, D), lambda bb, i, f: (bb, jnp.maximum(i * nh - 1, 0), 0)),
            pl.BlockSpec((1, HALO, D), lambda bb, i, f: (bb, jnp.minimum((i + 1) * nh, S // HALO - 1), 0)),
            vec, vec, vec,
            pl.BlockSpec((D, tf), lambda bb, i, f: (0, f)),
            pl.BlockSpec((D, tf), lambda bb, i, f: (0, f)),
            pl.BlockSpec((CONV_W, tf), lambda bb, i, f: (0, f)),
            pl.BlockSpec((1, tf), lambda bb, i, f: (0, f)),
            pl.BlockSpec((tf, D), lambda bb, i, f: (f, 0)),
            const, const,
        ],
        out_specs=pl.BlockSpec((1, tm, D), lambda bb, i, f: (bb, i, 0)),
        scratch_shapes=[pltpu.VMEM((tm + 2 * HALO, D), BF16), pltpu.VMEM((tm, D), F32)],
        compiler_params=_cparams("parallel", "parallel", "arbitrary"),
        name="conv_ffn",
    )(x, x, x, shift, scale, gate, wg, wu, cw, cb, wd, g, b)


def _rope_tables(S):
    t = jnp.arange(S)
    row, col = (t // GRID_W).astype(F32), (t % GRID_W).astype(F32)
    lane = np.arange(LANE)

    def tables(dim):
        half = dim // 4
        inv_freq = ROPE_THETA ** (-jnp.arange(half, dtype=F32) / half)
        used = lane < dim
        pos = jnp.where(jnp.asarray((lane % dim) < dim // 2)[None, :], row[:, None], col[:, None])
        ang = pos * inv_freq[jnp.asarray(lane % half)][None, :]
        cos, sin = jnp.cos(ang), jnp.sin(ang)
        lo = jnp.asarray(used & ((lane % (2 * half)) < half))[None, :]
        hi = jnp.asarray(used & ((lane % (2 * half)) >= half))[None, :]
        return [jnp.where(jnp.asarray(used)[None, :], cos, 1.0), jnp.where(lo, -sin, 0.0), jnp.where(hi, sin, 0.0)]

    return jnp.stack(tables(HEAD_DIM) + tables(MLA_ROPE)).astype(F32)


def _identity_rope(C):
    one, zero = jnp.ones((C, LANE), F32), jnp.zeros((C, LANE), F32)
    return jnp.stack([one, zero, zero, one, zero, zero])


def _pad_w_in(w):
    cut = _C0 + MLA_Q_LORA + MLA_KV_LORA + MLA_ROPE
    pad = jnp.zeros((w.shape[0], LANE - MLA_ROPE), w.dtype)
    return jnp.concatenate([w[:, :cut], pad, w[:, cut:]], axis=1).astype(BF16)


def _pad_w_uq(w):
    r = w.shape[0]
    w = w.reshape(r, MLA_HEADS, MLA_NOPE + MLA_ROPE)
    w = jnp.concatenate([w, jnp.zeros((r, MLA_HEADS, MLA_QK - MLA_NOPE - MLA_ROPE), w.dtype)], axis=2)
    return w.reshape(r, MLA_HEADS * MLA_QK).astype(BF16)


def kernel(x, c, ctx, c_ctx, w_ada, b_ada, w_in, na_rpb, swa_sink, mla_q_norm, mla_kv_norm, mla_w_uq, mla_w_ukv,
           gqa_q_norm, gqa_k_norm, w_out, ln1_g, ln1_b, ffn_w_gate, ffn_w_up, ffn_conv_w, ffn_conv_b, ffn_w_down,
           ln2_g, ln2_b):
    B, S, D = x.shape
    C = ctx.shape[1]
    depth = w_ada.shape[0]
    alpha = (2 * depth) ** 0.25

    cvec = jnp.concatenate([c, c_ctx[None, :], jnp.zeros((8 - B - 1, D), F32)], axis=0)
    ada = _ada(cvec, w_ada, b_ada)
    rope_x, rope_c = _rope_tables(S), _identity_rope(C)
    no_sink = jnp.full((NA_HEADS,), NEG, F32)

    tm_proj = min(256, S)
    tm_out = min(512, S)
    tm_ffn, tf = min(512, S), 512
    tq_dense = min(512, S)

    for i in range(depth):
        need_ctx = i < depth - 1
        mx = ada[i, :B].reshape(B, 1, 6, D)
        mod_x = [mx[:, :, j] for j in range(6)]
        mod_c = [jnp.broadcast_to(ada[i, B].reshape(1, 1, 6, D)[:, :, j], (B, 1, D)) for j in range(6)]
        lw = {
            "mla_q_norm": mla_q_norm[i][None, :], "mla_kv_norm": mla_kv_norm[i][None, :],
            "gqa_q_norm": gqa_q_norm[i][None, :], "gqa_k_norm": gqa_k_norm[i][None, :],
            "w_uq": _pad_w_uq(mla_w_uq[i]), "w_ukv": mla_w_ukv[i].astype(BF16),
        }
        w_in_p = _pad_w_in(w_in[i])
        w_out_b = w_out[i].astype(BF16)
        g1, b1, g2, b2 = ln1_g[i][None, :], ln1_b[i][None, :], ln2_g[i][None, :], ln2_b[i][None, :]
        ffn_w = (ffn_w_gate[i].astype(BF16), ffn_w_up[i].astype(BF16), ffn_conv_w[i], ffn_conv_b[i][None, :],
                 ffn_w_down[i].astype(BF16))

        px = _proj(x, mod_x[0], mod_x[1], w_in_p, rope_x, lw, tm_proj)
        pc = _proj(ctx, mod_c[0], mod_c[1], w_in_p, rope_c, lw, C)
        qa, ka, va, qb, kb, vb, qc, kc, vc, qd, kd, vd = px
        qa_c, ka_c, va_c, qb_c, kb_c, vb_c, qc_c, kc_c, vc_c, qd_c, kd_c, vd_c = pc

        mix_x = (
            _na_attention(qa, ka, va, ka_c, va_c, na_rpb[i]),
            _swa_attention(qb, kb, vb, kb_c, vb_c, swa_sink[i], min(256, S - 2 * SWA_WINDOW)),
            _dense_attention(qc, kc, vc, kc_c, vc_c, MLA_HEADS, 1, tq_dense, tq_dense),
            _dense_attention(qd, kd, vd, kd_c, vd_c, GQA_HEADS, GQA_HEADS // GQA_KV_HEADS, tq_dense, tq_dense),
        )
        x = _out_proj(mix_x, w_out_b, x, mod_x[2], g1, b1, alpha, tm_out)
        if need_ctx:
            mix_c = (
                _ctx_attention(qa_c, ka_c, va_c, no_sink, NA_HEADS, 1),
                _ctx_attention(qb_c, kb_c, vb_c, swa_sink[i], SWA_HEADS, SWA_HEADS // SWA_KV_HEADS),
                _ctx_attention(qc_c, kc_c, vc_c, no_sink, MLA_HEADS, 1),
                _ctx_attention(qd_c, kd_c, vd_c, no_sink, GQA_HEADS, GQA_HEADS // GQA_KV_HEADS),
            )
            ctx = _out_proj(mix_c, w_out_b, ctx, mod_c[2], g1, b1, alpha, C)

        x = _ffn(x, mod_x[3], mod_x[4], mod_x[5], *ffn_w, g2, b2, alpha, tm_ffn, tf)
        if need_ctx:
            ctx = _ffn(ctx, mod_c[3], mod_c[4], mod_c[5], *ffn_w, g2, b2, alpha, C, tf)
    return x
```

```python
import functools

import numpy as np
import jax
import jax.numpy as jnp
from jax import lax
from jax.experimental import pallas as pl
from jax.experimental.pallas import tpu as pltpu

GRID_W = 64
HEAD_DIM = 128
NA_HEADS = 4
NA_WIN_R = 8
NA_WIN_C = 16
SWA_HEADS = 4
SWA_KV_HEADS = 2
SWA_WINDOW = 128
MLA_HEADS = 4
MLA_Q_LORA = 384
MLA_KV_LORA = 128
MLA_NOPE = 128
MLA_ROPE = 64
MLA_V = 128
GQA_HEADS = 4
GQA_KV_HEADS = 2
CONV_W = 3
ROPE_THETA = 10000.0
EPS = 1e-6
NEG = -1e30

LANE = 128
BF16_SUBLANES = 16
VMEM_LIMIT = 56 * 1024 * 1024

F32 = jnp.float32
BF16 = jnp.bfloat16

NA_KEY_ROWS = NA_WIN_R + 1
NA_Q_ROWS = 2


def _cparams(*sem):
    return pltpu.CompilerParams(dimension_semantics=sem, vmem_limit_bytes=VMEM_LIMIT)


def _dot(a, b):
    return jnp.dot(a, b, preferred_element_type=F32)


def _dot_nt(a, b):
    return lax.dot_general(a, b, (((1,), (1,)), ((), ())), preferred_element_type=F32)


def _layer_norm(x):
    mu = jnp.mean(x, axis=-1, keepdims=True)
    xc = x - mu
    var = jnp.mean(xc * xc, axis=-1, keepdims=True)
    return xc * lax.rsqrt(var + EPS)


def _rms_norm(x, g):
    return x * lax.rsqrt(jnp.mean(x * x, axis=-1, keepdims=True) + EPS) * g


def _rope(x, cos, sin_lo, sin_hi, half):
    n = x.shape[-1]
    return x * cos + pltpu.roll(x, n - half, 1) * sin_lo + pltpu.roll(x, half, 1) * sin_hi


def _ada_kernel(c_ref, w_ref, b_ref, o_ref):
    c = c_ref[...]
    a = (c * jax.nn.sigmoid(c)).astype(BF16)
    o_ref[0] = _dot(a, w_ref[0].astype(BF16)) + b_ref[0]


def _ada(cvec, w_ada, b_ada):
    L, D, N = w_ada.shape
    M = cvec.shape[0]
    tn = 1024
    return pl.pallas_call(
        _ada_kernel,
        out_shape=jax.ShapeDtypeStruct((L, M, N), F32),
        grid=(L, N // tn),
        in_specs=[
            pl.BlockSpec((M, D), lambda l, j: (0, 0)),
            pl.BlockSpec((1, D, tn), lambda l, j: (l, 0, j)),
            pl.BlockSpec((1, 1, tn), lambda l, j: (l, 0, j)),
        ],
        out_specs=pl.BlockSpec((1, M, tn), lambda l, j: (l, 0, j)),
        compiler_params=_cparams("parallel", "parallel"),
        name="ada",
    )(cvec, w_ada, b_ada.reshape(L, 1, N))


_A0 = 0
_B0 = _A0 + 3 * NA_HEADS * HEAD_DIM
_C0 = _B0 + (SWA_HEADS + 2 * SWA_KV_HEADS) * HEAD_DIM
_C_W = MLA_Q_LORA + MLA_KV_LORA + LANE
_D0 = _C0 + _C_W
_D_W = (GQA_HEADS + 2 * GQA_KV_HEADS) * HEAD_DIM
IN_COLS_PAD = _D0 + _D_W
MLA_QK = 2 * LANE


def _proj_kernel(x_ref, shift_ref, scale_ref, w_ref, rope_ref, gq_lora_ref, gkv_lora_ref, gq_ref, gk_ref,
                 wuq_ref, wukv_ref,
                 qa_ref, ka_ref, va_ref, qb_ref, kb_ref, vb_ref, qc_ref, kc_ref, vc_ref, qd_ref, kd_ref, vd_ref):
    h = (_layer_norm(x_ref[0]) * (1.0 + scale_ref[0]) + shift_ref[0]).astype(BF16)
    cos2, slo2, shi2 = rope_ref[0], rope_ref[1], rope_ref[2]
    cos1, slo1, shi1 = rope_ref[3], rope_ref[4], rope_ref[5]
    rope2 = lambda t: _rope(t, cos2, slo2, shi2, HEAD_DIM // 4)
    rope1 = lambda t: _rope(t, cos1, slo1, shi1, MLA_ROPE // 4)
    hd = HEAD_DIM
    sc = hd ** -0.5

    w = NA_HEADS * hd
    pa = _dot(h, w_ref[:, _A0:_A0 + 3 * w])
    qa_ref[0] = (pa[:, :w] * sc).astype(BF16)
    ka_ref[0] = pa[:, w:2 * w].astype(BF16)
    va_ref[0] = pa[:, 2 * w:].astype(BF16)

    wq, wk = SWA_HEADS * hd, SWA_KV_HEADS * hd
    pb = _dot(h, w_ref[:, _B0:_B0 + wq + 2 * wk])
    for i in range(SWA_HEADS):
        qb_ref[0, :, i * hd:(i + 1) * hd] = (rope2(pb[:, i * hd:(i + 1) * hd]) * sc).astype(BF16)
    for i in range(SWA_KV_HEADS):
        kb_ref[0, :, i * hd:(i + 1) * hd] = rope2(pb[:, wq + i * hd:wq + (i + 1) * hd]).astype(BF16)
    vb_ref[0] = pb[:, wq + wk:].astype(BF16)

    pc = _dot(h, w_ref[:, _C0:_C0 + _C_W])
    cq = _rms_norm(pc[:, :MLA_Q_LORA], gq_lora_ref[...]).astype(BF16)
    ckv = _rms_norm(pc[:, MLA_Q_LORA:MLA_Q_LORA + MLA_KV_LORA], gkv_lora_ref[...]).astype(BF16)
    kpe = rope1(pc[:, MLA_Q_LORA + MLA_KV_LORA:]).astype(BF16)
    qup = _dot(cq, wuq_ref[...])
    kvup = _dot(ckv, wukv_ref[...])
    sc_mla = (MLA_NOPE + MLA_ROPE) ** -0.5
    for i in range(MLA_HEADS):
        o = i * MLA_QK
        qc_ref[0, o:o + LANE, :] = (qup[:, o:o + LANE] * sc_mla).T.astype(BF16)
        qc_ref[0, o + LANE:o + 2 * LANE, :] = (rope1(qup[:, o + LANE:o + 2 * LANE]) * sc_mla).T.astype(BF16)
        kc_ref[0, :, o:o + LANE] = kvup[:, o:o + LANE].astype(BF16)
        kc_ref[0, :, o + LANE:o + 2 * LANE] = kpe
        vc_ref[0, i * MLA_V:(i + 1) * MLA_V, :] = kvup[:, o + LANE:o + 2 * LANE].T.astype(BF16)

    wq, wk = GQA_HEADS * hd, GQA_KV_HEADS * hd
    pd = _dot(h, w_ref[:, _D0:_D0 + wq + 2 * wk])
    for i in range(GQA_HEADS):
        t = _rms_norm(pd[:, i * hd:(i + 1) * hd], gq_ref[...])
        qd_ref[0, i * hd:(i + 1) * hd, :] = (rope2(t) * sc).T.astype(BF16)
    for i in range(GQA_KV_HEADS):
        t = _rms_norm(pd[:, wq + i * hd:wq + (i + 1) * hd], gk_ref[...])
        kd_ref[0, :, i * hd:(i + 1) * hd] = rope2(t).astype(BF16)
        vd_ref[0, i * hd:(i + 1) * hd, :] = pd[:, wq + wk + i * hd:wq + wk + (i + 1) * hd].T.astype(BF16)


def _proj(x, shift, scale, w_in_p, rope_tab, lw, tm):
    B, S, D = x.shape
    widths = (NA_HEADS * HEAD_DIM,) * 3 + (SWA_HEADS * HEAD_DIM, SWA_KV_HEADS * HEAD_DIM, SWA_KV_HEADS * HEAD_DIM) \
        + (MLA_HEADS * MLA_QK, MLA_HEADS * MLA_QK, MLA_HEADS * MLA_V) \
        + (GQA_HEADS * HEAD_DIM, GQA_KV_HEADS * HEAD_DIM, GQA_KV_HEADS * HEAD_DIM)
    transposed = (False,) * 6 + (True, False, True, True, False, True)
    const = lambda shape: pl.BlockSpec(shape, lambda b, i: (0,) * len(shape), pipeline_mode=pl.Buffered(1))
    vec = pl.BlockSpec((1, 1, D), lambda b, i: (b, 0, 0))
    out_spec = lambda n, t: (pl.BlockSpec((1, n, tm), lambda b, i: (b, 0, i)) if t
                             else pl.BlockSpec((1, tm, n), lambda b, i: (b, i, 0)))
    return pl.pallas_call(
        _proj_kernel,
        out_shape=[jax.ShapeDtypeStruct((B, n, S) if t else (B, S, n), BF16) for n, t in zip(widths, transposed)],
        grid=(B, S // tm),
        in_specs=[
            pl.BlockSpec((1, tm, D), lambda b, i: (b, i, 0)),
            vec, vec,
            const((D, IN_COLS_PAD)),
            pl.BlockSpec((6, tm, LANE), lambda b, i: (0, i, 0)),
            const((1, MLA_Q_LORA)), const((1, MLA_KV_LORA)), const((1, HEAD_DIM)), const((1, HEAD_DIM)),
            const((MLA_Q_LORA, MLA_HEADS * MLA_QK)), const((MLA_KV_LORA, MLA_HEADS * MLA_QK)),
        ],
        out_specs=[out_spec(n, t) for n, t in zip(widths, transposed)],
        compiler_params=_cparams("parallel", "parallel"),
        name="proj",
    )(x, shift, scale, w_in_p, rope_tab, lw["mla_q_norm"], lw["mla_kv_norm"], lw["gqa_q_norm"], lw["gqa_k_norm"],
      lw["w_uq"], lw["w_ukv"])


def _softmax_pv(parts, sink=None):
    m = parts[0][0].max(axis=-1, keepdims=True)
    for s, _ in parts[1:]:
        m = jnp.maximum(m, s.max(axis=-1, keepdims=True))
    if sink is not None:
        m = jnp.maximum(m, sink)
    l = jnp.exp(sink - m) if sink is not None else 0.0
    acc = None
    for s, v in parts:
        p = jnp.exp(s - m)
        l = l + p.sum(axis=-1, keepdims=True)
        pv = _dot(p.astype(BF16), v)
        acc = pv if acc is None else acc + pv
    return acc / l


def _na_kernel(cls_ref, ws_ref, q_ref, k_ref, v_ref, kc_ref, vc_ref, bias_ref, o_ref):
    del cls_ref
    nk = NA_KEY_ROWS * GRID_W
    start = pl.multiple_of(ws_ref[pl.program_id(2)] * GRID_W, GRID_W)
    q = q_ref[0]
    s = _dot_nt(q, k_ref[0, pl.ds(start, nk), :]) + bias_ref[0, 0]
    s_ctx = _dot_nt(q, kc_ref[0])
    o_ref[0] = _softmax_pv([(s, v_ref[0, pl.ds(start, nk), :]), (s_ctx, vc_ref[0])]).astype(o_ref.dtype)


def _na_tables(rpb, rows):
    nt = rows // NA_Q_ROWS
    R = np.arange(nt) * NA_Q_ROWS
    ws = np.clip(R - NA_WIN_R // 2, 0, rows - NA_KEY_ROWS)
    qr = (R[:, None] + np.arange(NA_Q_ROWS)[None, :])
    kr = ws[:, None] + np.arange(NA_KEY_ROWS)[None, :]
    r0 = np.clip(qr - NA_WIN_R // 2, 0, rows - NA_WIN_R)
    row_ok = (kr[:, None, :] >= r0[:, :, None]) & (kr[:, None, :] < r0[:, :, None] + NA_WIN_R)
    drow = np.clip(kr[:, None, :] - qr[:, :, None] + NA_WIN_R - 1, 0, 2 * NA_WIN_R - 2)
    geom = np.concatenate([row_ok.reshape(nt, -1).astype(np.int64), drow.reshape(nt, -1)], axis=1)
    uniq, first, cls = np.unique(geom, axis=0, return_index=True, return_inverse=True)
    cq = np.arange(GRID_W)
    c0 = np.clip(cq - NA_WIN_C // 2, 0, GRID_W - NA_WIN_C)
    col_ok = (cq[None, :] >= c0[:, None]) & (cq[None, :] < c0[:, None] + NA_WIN_C)
    dcol = np.clip(cq[None, :] - cq[:, None] + NA_WIN_C - 1, 0, 2 * NA_WIN_C - 2)
    nc = len(first)
    cols = jnp.zeros(rpb.shape[:2] + dcol.shape, F32)
    for j in range(rpb.shape[2]):
        cols = jnp.where(jnp.asarray(dcol == j)[None, None], rpb[:, :, j][:, :, None, None], cols)
    cols = jnp.where(jnp.asarray(col_ok)[None, None], cols, NEG)
    blocks = jnp.stack([cols[:, d] for d in drow[first].reshape(-1)], axis=1)
    blocks = blocks.reshape(rpb.shape[0], nc, NA_Q_ROWS, NA_KEY_ROWS, GRID_W, GRID_W)
    blocks = jnp.where(jnp.asarray(row_ok[first])[None, :, :, :, None, None], blocks, NEG)
    bias = blocks.transpose(1, 0, 2, 4, 3, 5).reshape(nc, rpb.shape[0], NA_Q_ROWS * GRID_W, NA_KEY_ROWS * GRID_W)
    return bias.astype(F32), jnp.asarray(cls.reshape(-1), jnp.int32), jnp.asarray(ws, jnp.int32)


def _na_attention(q, k, v, kc, vc, rpb):
    B, S, _ = q.shape
    C = kc.shape[1]
    rows = S // GRID_W
    assert rows >= NA_KEY_ROWS and rows % NA_Q_ROWS == 0
    bias, cls, ws = _na_tables(rpb, rows)
    tq, nk, hd = NA_Q_ROWS * GRID_W, NA_KEY_ROWS * GRID_W, HEAD_DIM
    grid_spec = pltpu.PrefetchScalarGridSpec(
        num_scalar_prefetch=2,
        grid=(B, NA_HEADS, S // tq),
        in_specs=[
            pl.BlockSpec((1, tq, hd), lambda b, h, i, c, w: (b, i, h)),
            pl.BlockSpec((1, S, hd), lambda b, h, i, c, w: (b, 0, h)),
            pl.BlockSpec((1, S, hd), lambda b, h, i, c, w: (b, 0, h)),
            pl.BlockSpec((1, C, hd), lambda b, h, i, c, w: (b, 0, h)),
            pl.BlockSpec((1, C, hd), lambda b, h, i, c, w: (b, 0, h)),
            pl.BlockSpec((1, 1, tq, nk), lambda b, h, i, c, w: (c[i], h, 0, 0)),
        ],
        out_specs=pl.BlockSpec((1, tq, hd), lambda b, h, i, c, w: (b, i, h)),
    )
    return pl.pallas_call(
        _na_kernel,
        out_shape=jax.ShapeDtypeStruct((B, S, NA_HEADS * hd), BF16),
        grid_spec=grid_spec,
        compiler_params=_cparams("parallel", "parallel", "arbitrary"),
        name="na_attn",
    )(cls, ws, q, k, v, kc, vc, bias)


def _swa_kernel(q_ref, k_ref, v_ref, kc_ref, vc_ref, sink_ref, o_ref, *, tq, seq):
    nk = tq + 2 * SWA_WINDOW
    i = pl.program_id(2)
    start = pl.multiple_of(jnp.clip(i * tq - SWA_WINDOW, 0, seq - nk), SWA_WINDOW)
    q = q_ref[0]
    s = _dot_nt(q, k_ref[0, pl.ds(start, nk), :])
    qpos = i * tq + lax.broadcasted_iota(jnp.int32, (tq, nk), 0)
    kpos = start + lax.broadcasted_iota(jnp.int32, (tq, nk), 1)
    s = jnp.where(jnp.abs(qpos - kpos) <= SWA_WINDOW, s, NEG)
    s_ctx = _dot_nt(q, kc_ref[0])
    o = _softmax_pv([(s, v_ref[0, pl.ds(start, nk), :]), (s_ctx, vc_ref[0])], sink=sink_ref[0])
    o_ref[0] = o.astype(o_ref.dtype)


def _swa_attention(q, k, v, kc, vc, sink, tq):
    B, S, _ = q.shape
    C = kc.shape[1]
    hd, G = HEAD_DIM, SWA_HEADS // SWA_KV_HEADS
    assert S >= tq + 2 * SWA_WINDOW and S % tq == 0 and tq % SWA_WINDOW == 0
    return pl.pallas_call(
        functools.partial(_swa_kernel, tq=tq, seq=S),
        out_shape=jax.ShapeDtypeStruct((B, S, SWA_HEADS * hd), BF16),
        grid=(B, SWA_HEADS, S // tq),
        in_specs=[
            pl.BlockSpec((1, tq, hd), lambda b, h, i: (b, i, h)),
            pl.BlockSpec((1, S, hd), lambda b, h, i: (b, 0, h // G)),
            pl.BlockSpec((1, S, hd), lambda b, h, i: (b, 0, h // G)),
            pl.BlockSpec((1, C, hd), lambda b, h, i: (b, 0, h // G)),
            pl.BlockSpec((1, C, hd), lambda b, h, i: (b, 0, h // G)),
            pl.BlockSpec((1, 1, 1), lambda b, h, i: (h, 0, 0)),
        ],
        out_specs=pl.BlockSpec((1, tq, hd), lambda b, h, i: (b, i, h)),
        compiler_params=_cparams("parallel", "parallel", "arbitrary"),
        name="swa_attn",
    )(q, k, v, kc, vc, sink.reshape(SWA_HEADS, 1, 1).astype(F32))


def _dense_kernel(*refs, tk, seq, n_ctx):
    if n_ctx:
        qt_ref, k_ref, vt_ref, kc_ref, vct_ref, o_ref = refs
    else:
        qt_ref, k_ref, vt_ref, o_ref = refs
    qt = qt_ref[0]
    tq = qt.shape[1]
    m = jnp.full((1, tq), -jnp.inf, F32)
    l = jnp.zeros((1, tq), F32)
    acc = jnp.zeros((vt_ref.shape[1], tq), F32)

    chunks = [(k_ref, vt_ref, j * tk, tk) for j in range(seq // tk)]
    if n_ctx:
        chunks.append((kc_ref, vct_ref, 0, n_ctx))
    scores = lambda c: _dot(c[0][0, c[2]:c[2] + c[3], :], qt)

    s_next = scores(chunks[0])
    for j, c in enumerate(chunks):
        s = s_next
        if j + 1 < len(chunks):
            s_next = scores(chunks[j + 1])
        m_new = jnp.maximum(m, s.max(axis=0, keepdims=True))
        alpha = jnp.exp(m - m_new)
        p = jnp.exp(s - m_new)
        l = alpha * l + p.sum(axis=0, keepdims=True)
        acc = alpha * acc + _dot(c[1][0, :, c[2]:c[2] + c[3]], p.astype(BF16))
        m = m_new
    o_ref[0] = (acc / l).T.astype(o_ref.dtype)


def _dense_attention(qt, k, vt, kc, vct, n_heads, group, tq, tk):
    B, _, S = qt.shape
    Sk = k.shape[1]
    dk = qt.shape[1] // n_heads
    dv = vt.shape[1] // (n_heads // group)
    assert S % tq == 0 and Sk % tk == 0
    n_ctx = 0 if kc is None else kc.shape[1]
    in_specs = [
        pl.BlockSpec((1, dk, tq), lambda b, h, i: (b, h, i)),
        pl.BlockSpec((1, Sk, dk), lambda b, h, i: (b, 0, h // group)),
        pl.BlockSpec((1, dv, Sk), lambda b, h, i: (b, h // group, 0)),
    ]
    args = [qt, k, vt]
    if n_ctx:
        in_specs += [
            pl.BlockSpec((1, n_ctx, dk), lambda b, h, i: (b, 0, h // group)),
            pl.BlockSpec((1, dv, n_ctx), lambda b, h, i: (b, h // group, 0)),
        ]
        args += [kc, vct]
    return pl.pallas_call(
        functools.partial(_dense_kernel, tk=tk, seq=Sk, n_ctx=n_ctx),
        out_shape=jax.ShapeDtypeStruct((B, S, n_heads * dv), BF16),
        grid=(B, n_heads, S // tq),
        in_specs=in_specs,
        out_specs=pl.BlockSpec((1, tq, dv), lambda b, h, i: (b, i, h)),
        compiler_params=_cparams("parallel", "parallel", "arbitrary"),
        name="dense_attn",
    )(*args)


def _ctx_kernel(q_ref, k_ref, v_ref, sink_ref, o_ref):
    s = _dot_nt(q_ref[0], k_ref[0])
    o_ref[0] = _softmax_pv([(s, v_ref[0])], sink=sink_ref[0]).astype(o_ref.dtype)


def _ctx_attention(q, k, v, sink, n_heads, group):
    B, C, _ = q.shape
    dk = q.shape[2] // n_heads
    dv = v.shape[2] // (n_heads // group)
    return pl.pallas_call(
        _ctx_kernel,
        out_shape=jax.ShapeDtypeStruct((B, C, n_heads * dv), BF16),
        grid=(B, n_heads),
        in_specs=[
            pl.BlockSpec((1, C, dk), lambda b, h: (b, 0, h)),
            pl.BlockSpec((1, C, dk), lambda b, h: (b, 0, h // group)),
            pl.BlockSpec((1, C, dv), lambda b, h: (b, 0, h // group)),
            pl.BlockSpec((1, 1, 1), lambda b, h: (h, 0, 0)),
        ],
        out_specs=pl.BlockSpec((1, C, dv), lambda b, h: (b, 0, h)),
        compiler_params=_cparams("parallel", "parallel"),
        name="ctx_attn",
    )(q, k, v, sink.reshape(n_heads, 1, 1).astype(F32))


def _out_kernel(oa_ref, ob_ref, oc_ref, od_ref, w_ref, x_ref, gate_ref, g_ref, b_ref, o_ref, *, alpha):
    y = None
    o = 0
    for r in (oa_ref, ob_ref, oc_ref, od_ref):
        n = r.shape[2]
        t = _dot(r[0], w_ref[o:o + n, :])
        y = t if y is None else y + t
        o += n
    z = alpha * x_ref[0] + gate_ref[0] * y
    o_ref[0] = _layer_norm(z) * g_ref[...] + b_ref[...]


def _out_proj(mix, w_out, x, gate, g, b, alpha, tm):
    B, S, D = x.shape
    const = lambda shape: pl.BlockSpec(shape, lambda bb, i: (0,) * len(shape), pipeline_mode=pl.Buffered(1))
    return pl.pallas_call(
        functools.partial(_out_kernel, alpha=alpha),
        out_shape=jax.ShapeDtypeStruct((B, S, D), F32),
        grid=(B, S // tm),
        in_specs=[pl.BlockSpec((1, tm, m.shape[2]), lambda bb, i: (bb, i, 0)) for m in mix] + [
            const(w_out.shape),
            pl.BlockSpec((1, tm, D), lambda bb, i: (bb, i, 0)),
            pl.BlockSpec((1, 1, D), lambda bb, i: (bb, 0, 0)),
            const((1, D)), const((1, D)),
        ],
        out_specs=pl.BlockSpec((1, tm, D), lambda bb, i: (bb, i, 0)),
        compiler_params=_cparams("parallel", "parallel"),
        name="out_proj",
    )(*mix, w_out, x, gate, g, b)


HALO = BF16_SUBLANES


def _ffn_kernel(x_ref, xp_ref, xn_ref, shift_ref, scale_ref, gate_ref, wg_ref, wu_ref, cw_ref, cb_ref, wd_ref,
                g_ref, b_ref, o_ref, h_ref, acc_ref, *, alpha, tm):
    i, f = pl.program_id(1), pl.program_id(2)

    @pl.when(f == 0)
    def _():
        mod = lambda t: _layer_norm(t) * (1.0 + scale_ref[0]) + shift_ref[0]
        keep_p = (i > 0).astype(F32)
        keep_n = (i < pl.num_programs(1) - 1).astype(F32)
        h_ref[0:HALO, :] = (mod(xp_ref[0]) * keep_p).astype(BF16)
        h_ref[HALO:HALO + tm, :] = mod(x_ref[0]).astype(BF16)
        h_ref[HALO + tm:, :] = (mod(xn_ref[0]) * keep_n).astype(BF16)
        acc_ref[...] = jnp.zeros(acc_ref.shape, F32)

    n = tm + 2 * HALO
    gt = _dot(h_ref[...], wg_ref[...])
    up = _dot(h_ref[HALO:HALO + tm, :], wu_ref[...])
    a = (pltpu.roll(gt, 1, 0) * cw_ref[0:1, :] + gt * cw_ref[1:2, :]
         + pltpu.roll(gt, n - 1, 0) * cw_ref[2:3, :] + cb_ref[...])[HALO:HALO + tm]
    y = (a * jax.nn.sigmoid(a) * up).astype(BF16)
    acc_ref[...] += _dot(y, wd_ref[...])

    @pl.when(f == pl.num_programs(2) - 1)
    def _():
        z = alpha * x_ref[0] + gate_ref[0] * acc_ref[...]
        o_ref[0] = _layer_norm(z) * g_ref[...] + b_ref[...]


def _ffn(x, shift, scale, gate, wg, wu, cw, cb, wd, g, b, alpha, tm, tf):
    B, S, D = x.shape
    F = wg.shape[1]
    assert S % tm == 0 and F % tf == 0 and tm % HALO == 0
    nh = tm // HALO
    vec = pl.BlockSpec((1, 1, D), lambda bb, i, f: (bb, 0, 0))
    const = pl.BlockSpec((1, D), lambda bb, i, f: (0, 0))
    return pl.pallas_call(
        functools.partial(_ffn_kernel, alpha=alpha, tm=tm),
        out_shape=jax.ShapeDtypeStruct((B, S, D), F32),
        grid=(B, S // tm, F // tf),
        in_specs=[
            pl.BlockSpec((1, tm, D), lambda bb, i, f: (bb, i, 0)),
            pl.BlockSpec((1, HALO, D), lambda bb, i, f: (bb, jnp.maximum(i * nh - 1, 0), 0)),
            pl.BlockSpec((1, HALO, D), lambda bb, i, f: (bb, jnp.minimum((i + 1) * nh, S // HALO - 1), 0)),
            vec, vec, vec,
            pl.BlockSpec((D, tf), lambda bb, i, f: (0, f)),
            pl.BlockSpec((D, tf), lambda bb, i, f: (0, f)),
            pl.BlockSpec((CONV_W, tf), lambda bb, i, f: (0, f)),
            pl.BlockSpec((1, tf), lambda bb, i, f: (0, f)),
            pl.BlockSpec((tf, D), lambda bb, i, f: (f, 0)),
            const, const,
        ],
        out_specs=pl.BlockSpec((1, tm, D), lambda bb, i, f: (bb, i, 0)),
        scratch_shapes=[pltpu.VMEM((tm + 2 * HALO, D), BF16), pltpu.VMEM((tm, D), F32)],
        compiler_params=_cparams("parallel", "parallel", "arbitrary"),
        name="conv_ffn",
    )(x, x, x, shift, scale, gate, wg, wu, cw, cb, wd, g, b)


def _rope_tables(S):
    t = jnp.arange(S)
    row, col = (t // GRID_W).astype(F32), (t % GRID_W).astype(F32)
    lane = np.arange(LANE)

    def tables(dim):
        half = dim // 4
        inv_freq = ROPE_THETA ** (-jnp.arange(half, dtype=F32) / half)
        used = lane < dim
        pos = jnp.where(jnp.asarray((lane % dim) < dim // 2)[None, :], row[:, None], col[:, None])
        ang = pos * inv_freq[jnp.asarray(lane % half)][None, :]
        cos, sin = jnp.cos(ang), jnp.sin(ang)
        lo = jnp.asarray(used & ((lane % (2 * half)) < half))[None, :]
        hi = jnp.asarray(used & ((lane % (2 * half)) >= half))[None, :]
        return [jnp.where(jnp.asarray(used)[None, :], cos, 1.0), jnp.where(lo, -sin, 0.0), jnp.where(hi, sin, 0.0)]

    return jnp.stack(tables(HEAD_DIM) + tables(MLA_ROPE)).astype(F32)


def _identity_rope(C):
    one, zero = jnp.ones((C, LANE), F32), jnp.zeros((C, LANE), F32)
    return jnp.stack([one, zero, zero, one, zero, zero])


def _pad_w_in(w):
    cut = _C0 + MLA_Q_LORA + MLA_KV_LORA + MLA_ROPE
    pad = jnp.zeros((w.shape[0], LANE - MLA_ROPE), w.dtype)
    return jnp.concatenate([w[:, :cut], pad, w[:, cut:]], axis=1).astype(BF16)


def _pad_w_uq(w):
    r = w.shape[0]
    w = w.reshape(r, MLA_HEADS, MLA_NOPE + MLA_ROPE)
    w = jnp.concatenate([w, jnp.zeros((r, MLA_HEADS, MLA_QK - MLA_NOPE - MLA_ROPE), w.dtype)], axis=2)
    return w.reshape(r, MLA_HEADS * MLA_QK).astype(BF16)


def kernel(x, c, ctx, c_ctx, w_ada, b_ada, w_in, na_rpb, swa_sink, mla_q_norm, mla_kv_norm, mla_w_uq, mla_w_ukv,
           gqa_q_norm, gqa_k_norm, w_out, ln1_g, ln1_b, ffn_w_gate, ffn_w_up, ffn_conv_w, ffn_conv_b, ffn_w_down,
           ln2_g, ln2_b):
    B, S, D = x.shape
    C = ctx.shape[1]
    depth = w_ada.shape[0]
    alpha = (2 * depth) ** 0.25

    cvec = jnp.concatenate([c, c_ctx[None, :], jnp.zeros((8 - B - 1, D), F32)], axis=0)
    ada = _ada(cvec, w_ada, b_ada)
    rope_x, rope_c = _rope_tables(S), _identity_rope(C)
    no_sink = jnp.full((NA_HEADS,), NEG, F32)

    tm_proj = min(256, S)
    tm_out = min(512, S)
    tm_ffn, tf = min(512, S), 512
    tq_dense, tk_dense = min(256, S), min(256, S)

    for i in range(depth):
        need_ctx = i < depth - 1
        mx = ada[i, :B].reshape(B, 1, 6, D)
        mod_x = [mx[:, :, j] for j in range(6)]
        mod_c = [jnp.broadcast_to(ada[i, B].reshape(1, 1, 6, D)[:, :, j], (B, 1, D)) for j in range(6)]
        lw = {
            "mla_q_norm": mla_q_norm[i][None, :], "mla_kv_norm": mla_kv_norm[i][None, :],
            "gqa_q_norm": gqa_q_norm[i][None, :], "gqa_k_norm": gqa_k_norm[i][None, :],
            "w_uq": _pad_w_uq(mla_w_uq[i]), "w_ukv": mla_w_ukv[i].astype(BF16),
        }
        w_in_p = _pad_w_in(w_in[i])
        w_out_b = w_out[i].astype(BF16)
        g1, b1, g2, b2 = ln1_g[i][None, :], ln1_b[i][None, :], ln2_g[i][None, :], ln2_b[i][None, :]
        ffn_w = (ffn_w_gate[i].astype(BF16), ffn_w_up[i].astype(BF16), ffn_conv_w[i], ffn_conv_b[i][None, :],
                 ffn_w_down[i].astype(BF16))

        px = _proj(x, mod_x[0], mod_x[1], w_in_p, rope_x, lw, tm_proj)
        pc = _proj(ctx, mod_c[0], mod_c[1], w_in_p, rope_c, lw, C)
        qa, ka, va, qb, kb, vb, qc, kc, vc, qd, kd, vd = px
        qa_c, ka_c, va_c, qb_c, kb_c, vb_c, qc_c, kc_c, vc_c, qd_c, kd_c, vd_c = pc

        mix_x = (
            _na_attention(qa, ka, va, ka_c, va_c, na_rpb[i]),
            _swa_attention(qb, kb, vb, kb_c, vb_c, swa_sink[i], min(256, S - 2 * SWA_WINDOW)),
            _dense_attention(qc, kc, vc, kc_c, vc_c, MLA_HEADS, 1, tq_dense, tk_dense),
            _dense_attention(qd, kd, vd, kd_c, vd_c, GQA_HEADS, GQA_HEADS // GQA_KV_HEADS, tq_dense, tk_dense),
        )
        x = _out_proj(mix_x, w_out_b, x, mod_x[2], g1, b1, alpha, tm_out)
        if need_ctx:
            mix_c = (
                _ctx_attention(qa_c, ka_c, va_c, no_sink, NA_HEADS, 1),
                _ctx_attention(qb_c, kb_c, vb_c, swa_sink[i], SWA_HEADS, SWA_HEADS // SWA_KV_HEADS),
                _dense_attention(qc_c, kc_c, vc_c, None, None, MLA_HEADS, 1, C, C),
                _dense_attention(qd_c, kd_c, vd_c, None, None, GQA_HEADS, GQA_HEADS // GQA_KV_HEADS, C, C),
            )
            ctx = _out_proj(mix_c, w_out_b, ctx, mod_c[2], g1, b1, alpha, C)

        x = _ffn(x, mod_x[3], mod_x[4], mod_x[5], *ffn_w, g2, b2, alpha, tm_ffn, tf)
        if need_ctx:
            ctx = _ffn(ctx, mod_c[3], mod_c[4], mod_c[5], *ffn_w, g2, b2, alpha, C, tf)
    return x
```

```python
import functools
import math

import numpy as np
import jax
import jax.numpy as jnp
from jax import lax
from jax.experimental import pallas as pl
from jax.experimental.pallas import tpu as pltpu

GRID_W = 64
HEAD_DIM = 128
NA_HEADS = 4
NA_WIN_R = 8
NA_WIN_C = 16
SWA_HEADS = 4
SWA_KV_HEADS = 2
SWA_WINDOW = 128
MLA_HEADS = 4
MLA_Q_LORA = 384
MLA_KV_LORA = 128
MLA_NOPE = 128
MLA_ROPE = 64
MLA_V = 128
GQA_HEADS = 4
GQA_KV_HEADS = 2
CONV_W = 3
ROPE_THETA = 10000.0
EPS = 1e-6
NEG = -1e30
LOG2E = math.log2(math.e)

LANE = 128
BF16_SUBLANES = 16
VMEM_LIMIT = 56 * 1024 * 1024

F32 = jnp.float32
BF16 = jnp.bfloat16

NA_Q_ROWS = 4
NA_KEY_ROWS = NA_WIN_R + NA_Q_ROWS


def _cparams(*sem):
    return pltpu.CompilerParams(dimension_semantics=sem, vmem_limit_bytes=VMEM_LIMIT)


def _dot(a, b):
    return jnp.dot(a, b, preferred_element_type=F32)


def _layer_norm(x):
    mu = jnp.mean(x, axis=-1, keepdims=True)
    xc = x - mu
    var = jnp.mean(xc * xc, axis=-1, keepdims=True)
    return xc * lax.rsqrt(var + EPS)


def _rms_norm(x, g):
    return x * lax.rsqrt(jnp.mean(x * x, axis=-1, keepdims=True) + EPS) * g


def _rope(x, cos, sin_lo, sin_hi, half):
    n = x.shape[-1]
    return x * cos + pltpu.roll(x, n - half, 1) * sin_lo + pltpu.roll(x, half, 1) * sin_hi


def _layer_spec(arr, layer, index_map_rest, block_rest, **kw):
    return pl.BlockSpec((None,) + tuple(block_rest), lambda *g: (layer,) + tuple(index_map_rest(*g)), **kw)


def _ada_kernel(c_ref, w_ref, b_ref, o_ref):
    c = c_ref[...]
    a = (c * jax.nn.sigmoid(c)).astype(BF16)
    o_ref[0] = _dot(a, w_ref[0].astype(BF16)) + b_ref[0]


def _ada(cvec, w_ada, b_ada):
    L, D, N = w_ada.shape
    M = cvec.shape[0]
    tn = 1024
    return pl.pallas_call(
        _ada_kernel,
        out_shape=jax.ShapeDtypeStruct((L, M, N), F32),
        grid=(L, N // tn),
        in_specs=[
            pl.BlockSpec((M, D), lambda l, j: (0, 0)),
            pl.BlockSpec((1, D, tn), lambda l, j: (l, 0, j)),
            pl.BlockSpec((1, 1, tn), lambda l, j: (l, 0, j)),
        ],
        out_specs=pl.BlockSpec((1, M, tn), lambda l, j: (l, 0, j)),
        compiler_params=_cparams("parallel", "parallel"),
        name="ada",
    )(cvec, w_ada, b_ada.reshape(L, 1, N))


_A0 = 0
_B0 = _A0 + 3 * NA_HEADS * HEAD_DIM
_C0 = _B0 + (SWA_HEADS + 2 * SWA_KV_HEADS) * HEAD_DIM
_C_W = MLA_Q_LORA + MLA_KV_LORA + LANE
_D0 = _C0 + _C_W
_D_W = (GQA_HEADS + 2 * GQA_KV_HEADS) * HEAD_DIM
IN_COLS_PAD = _D0 + _D_W
MLA_QK = 2 * LANE


def _proj_kernel(x_ref, shift_ref, scale_ref, w_ref, rope_ref, gq_lora_ref, gkv_lora_ref, gq_ref, gk_ref,
                 wuq_ref, wukv_ref,
                 qa_ref, ka_ref, va_ref, qb_ref, kb_ref, vb_ref, qc_ref, kc_ref, vc_ref, qd_ref, kd_ref, vd_ref):
    h = (_layer_norm(x_ref[0]) * (1.0 + scale_ref[0]) + shift_ref[0]).astype(BF16)
    cos2, slo2, shi2 = rope_ref[0], rope_ref[1], rope_ref[2]
    cos1, slo1, shi1 = rope_ref[3], rope_ref[4], rope_ref[5]
    rope2 = lambda t: _rope(t, cos2, slo2, shi2, HEAD_DIM // 4)
    rope1 = lambda t: _rope(t, cos1, slo1, shi1, MLA_ROPE // 4)
    hd = HEAD_DIM
    sc = hd ** -0.5 * LOG2E

    def store_t(ref, lo, t):
        ref[0, lo:lo + t.shape[1], :] = t.T.astype(BF16)

    w = NA_HEADS * hd
    pa = _dot(h, w_ref[:, _A0:_A0 + 3 * w])
    store_t(qa_ref, 0, pa[:, :w] * sc)
    ka_ref[0] = pa[:, w:2 * w].astype(BF16)
    store_t(va_ref, 0, pa[:, 2 * w:])

    wq, wk = SWA_HEADS * hd, SWA_KV_HEADS * hd
    pb = _dot(h, w_ref[:, _B0:_B0 + wq + 2 * wk])
    for i in range(SWA_HEADS):
        store_t(qb_ref, i * hd, rope2(pb[:, i * hd:(i + 1) * hd]) * sc)
    for i in range(SWA_KV_HEADS):
        kb_ref[0, :, i * hd:(i + 1) * hd] = rope2(pb[:, wq + i * hd:wq + (i + 1) * hd]).astype(BF16)
    store_t(vb_ref, 0, pb[:, wq + wk:])

    pc = _dot(h, w_ref[:, _C0:_C0 + _C_W])
    cq = _rms_norm(pc[:, :MLA_Q_LORA], gq_lora_ref[...]).astype(BF16)
    ckv = _rms_norm(pc[:, MLA_Q_LORA:MLA_Q_LORA + MLA_KV_LORA], gkv_lora_ref[...]).astype(BF16)
    kpe = rope1(pc[:, MLA_Q_LORA + MLA_KV_LORA:]).astype(BF16)
    qup = _dot(cq, wuq_ref[...])
    kvup = _dot(ckv, wukv_ref[...])
    sc_mla = (MLA_NOPE + MLA_ROPE) ** -0.5 * LOG2E
    for i in range(MLA_HEADS):
        o = i * MLA_QK
        store_t(qc_ref, o, qup[:, o:o + LANE] * sc_mla)
        store_t(qc_ref, o + LANE, rope1(qup[:, o + LANE:o + 2 * LANE]) * sc_mla)
        kc_ref[0, :, o:o + LANE] = kvup[:, o:o + LANE].astype(BF16)
        kc_ref[0, :, o + LANE:o + 2 * LANE] = kpe
        store_t(vc_ref, i * MLA_V, kvup[:, o + LANE:o + 2 * LANE])

    wq, wk = GQA_HEADS * hd, GQA_KV_HEADS * hd
    pd = _dot(h, w_ref[:, _D0:_D0 + wq + 2 * wk])
    for i in range(GQA_HEADS):
        store_t(qd_ref, i * hd, rope2(_rms_norm(pd[:, i * hd:(i + 1) * hd], gq_ref[...])) * sc)
    for i in range(GQA_KV_HEADS):
        t = _rms_norm(pd[:, wq + i * hd:wq + (i + 1) * hd], gk_ref[...])
        kd_ref[0, :, i * hd:(i + 1) * hd] = rope2(t).astype(BF16)
    store_t(vd_ref, 0, pd[:, wq + wk:])


def _proj(x, shift, scale, rope_tab, sw, layer, tm):
    B, S, D = x.shape
    widths = (NA_HEADS * HEAD_DIM,) * 3 + (SWA_HEADS * HEAD_DIM, SWA_KV_HEADS * HEAD_DIM, SWA_KV_HEADS * HEAD_DIM) \
        + (MLA_HEADS * MLA_QK, MLA_HEADS * MLA_QK, MLA_HEADS * MLA_V) \
        + (GQA_HEADS * HEAD_DIM, GQA_KV_HEADS * HEAD_DIM, GQA_KV_HEADS * HEAD_DIM)
    transposed = (True, False, True) * 4
    const = lambda arr: _layer_spec(arr, layer, lambda b, i: (0, 0), arr.shape[1:], pipeline_mode=pl.Buffered(1))
    vec = pl.BlockSpec((1, 1, D), lambda b, i: (b, 0, 0))
    out_spec = lambda n, t: (pl.BlockSpec((1, n, tm), lambda b, i: (b, 0, i)) if t
                             else pl.BlockSpec((1, tm, n), lambda b, i: (b, i, 0)))
    params = (sw["w_in"], None, sw["mla_q_norm"], sw["mla_kv_norm"], sw["gqa_q_norm"], sw["gqa_k_norm"],
              sw["w_uq"], sw["w_ukv"])
    in_specs = [pl.BlockSpec((1, tm, D), lambda b, i: (b, i, 0)), vec, vec]
    in_specs += [pl.BlockSpec((6, tm, LANE), lambda b, i: (0, i, 0)) if p is None else const(p) for p in params]
    args = [rope_tab if p is None else p for p in params]
    return pl.pallas_call(
        _proj_kernel,
        out_shape=[jax.ShapeDtypeStruct((B, n, S) if t else (B, S, n), BF16) for n, t in zip(widths, transposed)],
        grid=(B, S // tm),
        in_specs=in_specs,
        out_specs=[out_spec(n, t) for n, t in zip(widths, transposed)],
        compiler_params=_cparams("parallel", "parallel"),
        name="proj",
    )(x, shift, scale, *args)


def _softmax_pv(parts, sink=None):
    m = parts[0][0].max(axis=0, keepdims=True)
    for s, _ in parts[1:]:
        m = jnp.maximum(m, s.max(axis=0, keepdims=True))
    l = 0.0
    if sink is not None:
        m = jnp.maximum(m, sink)
        l = jnp.exp2(sink - m)
    acc = None
    for s, vt in parts:
        p = jnp.exp2(s - m)
        l = l + p.sum(axis=0, keepdims=True)
        pv = _dot(vt, p.astype(BF16))
        acc = pv if acc is None else acc + pv
    return acc / l


def _na_kernel(cls_ref, ws_ref, qt_ref, k_ref, vt_ref, kc_ref, vct_ref, bias_ref, o_ref):
    del cls_ref
    nk = NA_KEY_ROWS * GRID_W
    start = pl.multiple_of(ws_ref[pl.program_id(2)] * GRID_W, NA_Q_ROWS * GRID_W)
    qt = qt_ref[0]
    s = _dot(k_ref[0, pl.ds(start, nk), :], qt) + bias_ref[0, 0]
    s_ctx = _dot(kc_ref[0], qt)
    o = _softmax_pv([(s, vt_ref[0, :, pl.ds(start, nk)]), (s_ctx, vct_ref[0])])
    o_ref[0] = o.T.astype(o_ref.dtype)


def _na_tables(rpb, rows):
    nt = rows // NA_Q_ROWS
    R = np.arange(nt) * NA_Q_ROWS
    ws = np.clip(R - NA_WIN_R // 2, 0, rows - NA_KEY_ROWS)
    qr = (R[:, None] + np.arange(NA_Q_ROWS)[None, :])
    kr = ws[:, None] + np.arange(NA_KEY_ROWS)[None, :]
    r0 = np.clip(qr - NA_WIN_R // 2, 0, rows - NA_WIN_R)
    row_ok = (kr[:, None, :] >= r0[:, :, None]) & (kr[:, None, :] < r0[:, :, None] + NA_WIN_R)
    drow = np.clip(kr[:, None, :] - qr[:, :, None] + NA_WIN_R - 1, 0, 2 * NA_WIN_R - 2)
    geom = np.concatenate([row_ok.reshape(nt, -1).astype(np.int64), drow.reshape(nt, -1)], axis=1)
    _, first, cls = np.unique(geom, axis=0, return_index=True, return_inverse=True)
    cq = np.arange(GRID_W)
    c0 = np.clip(cq - NA_WIN_C // 2, 0, GRID_W - NA_WIN_C)
    col_ok = (cq[None, :] >= c0[:, None]) & (cq[None, :] < c0[:, None] + NA_WIN_C)
    dcol = np.clip(cq[None, :] - cq[:, None] + NA_WIN_C - 1, 0, 2 * NA_WIN_C - 2)
    nc, H = len(first), rpb.shape[0]
    cols = jnp.zeros(rpb.shape[:2] + dcol.shape, F32)
    for j in range(rpb.shape[2]):
        cols = jnp.where(jnp.asarray(dcol == j)[None, None], rpb[:, :, j][:, :, None, None] * LOG2E, cols)
    cols = jnp.where(jnp.asarray(col_ok)[None, None], cols, NEG)
    blocks = jnp.stack([cols[:, d] for d in drow[first].reshape(-1)], axis=1)
    blocks = blocks.reshape(H, nc, NA_Q_ROWS, NA_KEY_ROWS, GRID_W, GRID_W)
    blocks = jnp.where(jnp.asarray(row_ok[first])[None, :, :, :, None, None], blocks, NEG)
    bias = blocks.transpose(1, 0, 3, 5, 2, 4).reshape(nc, H, NA_KEY_ROWS * GRID_W, NA_Q_ROWS * GRID_W)
    return bias.astype(F32), jnp.asarray(cls.reshape(-1), jnp.int32), jnp.asarray(ws, jnp.int32)


def _na_attention(qt, k, vt, kc, vct, rpb):
    B, _, S = qt.shape
    C = kc.shape[1]
    rows = S // GRID_W
    assert rows >= NA_KEY_ROWS and rows % NA_Q_ROWS == 0
    bias, cls, ws = _na_tables(rpb, rows)
    tq, nk, hd = NA_Q_ROWS * GRID_W, NA_KEY_ROWS * GRID_W, HEAD_DIM
    grid_spec = pltpu.PrefetchScalarGridSpec(
        num_scalar_prefetch=2,
        grid=(B, NA_HEADS, S // tq),
        in_specs=[
            pl.BlockSpec((1, hd, tq), lambda b, h, i, c, w: (b, h, i)),
            pl.BlockSpec((1, S, hd), lambda b, h, i, c, w: (b, 0, h)),
            pl.BlockSpec((1, hd, S), lambda b, h, i, c, w: (b, h, 0)),
            pl.BlockSpec((1, C, hd), lambda b, h, i, c, w: (b, 0, h)),
            pl.BlockSpec((1, hd, C), lambda b, h, i, c, w: (b, h, 0)),
            pl.BlockSpec((1, 1, nk, tq), lambda b, h, i, c, w: (c[i], h, 0, 0)),
        ],
        out_specs=pl.BlockSpec((1, tq, hd), lambda b, h, i, c, w: (b, i, h)),
    )
    return pl.pallas_call(
        _na_kernel,
        out_shape=jax.ShapeDtypeStruct((B, S, NA_HEADS * hd), BF16),
        grid_spec=grid_spec,
        compiler_params=_cparams("parallel", "parallel", "arbitrary"),
        name="na_attn",
    )(cls, ws, qt, k, vt, kc, vct, bias)


def _swa_kernel(qt_ref, k_ref, vt_ref, kc_ref, vct_ref, sink_ref, o_ref, *, tq, seq):
    nk = tq + 2 * SWA_WINDOW
    i = pl.program_id(2)
    start = pl.multiple_of(jnp.clip(i * tq - SWA_WINDOW, 0, seq - nk), SWA_WINDOW)
    qt = qt_ref[0]
    s = _dot(k_ref[0, pl.ds(start, nk), :], qt)
    kpos = start + lax.broadcasted_iota(jnp.int32, (nk, tq), 0)
    qpos = i * tq + lax.broadcasted_iota(jnp.int32, (nk, tq), 1)
    s = jnp.where(jnp.abs(qpos - kpos) <= SWA_WINDOW, s, NEG)
    s_ctx = _dot(kc_ref[0], qt)
    o = _softmax_pv([(s, vt_ref[0, :, pl.ds(start, nk)]), (s_ctx, vct_ref[0])], sink=sink_ref[0])
    o_ref[0] = o.T.astype(o_ref.dtype)


def _swa_attention(qt, k, vt, kc, vct, sink, tq):
    B, _, S = qt.shape
    C = kc.shape[1]
    hd, G = HEAD_DIM, SWA_HEADS // SWA_KV_HEADS
    assert S >= tq + 2 * SWA_WINDOW and S % tq == 0 and tq % SWA_WINDOW == 0
    return pl.pallas_call(
        functools.partial(_swa_kernel, tq=tq, seq=S),
        out_shape=jax.ShapeDtypeStruct((B, S, SWA_HEADS * hd), BF16),
        grid=(B, SWA_HEADS, S // tq),
        in_specs=[
            pl.BlockSpec((1, hd, tq), lambda b, h, i: (b, h, i)),
            pl.BlockSpec((1, S, hd), lambda b, h, i: (b, 0, h // G)),
            pl.BlockSpec((1, hd, S), lambda b, h, i: (b, h // G, 0)),
            pl.BlockSpec((1, C, hd), lambda b, h, i: (b, 0, h // G)),
            pl.BlockSpec((1, hd, C), lambda b, h, i: (b, h // G, 0)),
            pl.BlockSpec((1, 1, 1), lambda b, h, i: (h, 0, 0)),
        ],
        out_specs=pl.BlockSpec((1, tq, hd), lambda b, h, i: (b, i, h)),
        compiler_params=_cparams("parallel", "parallel", "arbitrary"),
        name="swa_attn",
    )(qt, k, vt, kc, vct, (sink * LOG2E).reshape(SWA_HEADS, 1, 1).astype(F32))


def _dense_kernel(*refs, tk, seq, n_ctx, has_sink):
    refs = list(refs)
    qt_ref, k_ref, vt_ref = refs[:3]
    o_ref = refs[-1]
    kc_ref, vct_ref = refs[3:5] if n_ctx else (None, None)
    qt = qt_ref[0]
    tq = qt.shape[1]
    if has_sink:
        m = jnp.broadcast_to(refs[-2][0], (1, tq))
        l = jnp.ones((1, tq), F32)
    else:
        m = jnp.full((1, tq), -jnp.inf, F32)
        l = jnp.zeros((1, tq), F32)
    acc = jnp.zeros((vt_ref.shape[1], tq), F32)

    chunks = [(k_ref, vt_ref, j * tk, tk) for j in range(seq // tk)]
    if n_ctx:
        chunks.append((kc_ref, vct_ref, 0, n_ctx))
    scores = lambda c: _dot(c[0][0, c[2]:c[2] + c[3], :], qt)

    s_next = scores(chunks[0])
    for j, c in enumerate(chunks):
        s = s_next
        if j + 1 < len(chunks):
            s_next = scores(chunks[j + 1])
        m_new = jnp.maximum(m, s.max(axis=0, keepdims=True))
        alpha = jnp.exp2(m - m_new)
        p = jnp.exp2(s - m_new)
        l = alpha * l + p.sum(axis=0, keepdims=True)
        acc = alpha * acc + _dot(c[1][0, :, c[2]:c[2] + c[3]], p.astype(BF16))
        m = m_new
    o_ref[0] = (acc / l).T.astype(o_ref.dtype)


def _dense_attention(qt, k, vt, kc, vct, n_heads, group, tq, tk, sink=None):
    B, _, S = qt.shape
    Sk = k.shape[1]
    dk = qt.shape[1] // n_heads
    dv = vt.shape[1] // (n_heads // group)
    assert S % tq == 0 and Sk % tk == 0
    n_ctx = 0 if kc is None else kc.shape[1]
    in_specs = [
        pl.BlockSpec((1, dk, tq), lambda b, h, i: (b, h, i)),
        pl.BlockSpec((1, Sk, dk), lambda b, h, i: (b, 0, h // group)),
        pl.BlockSpec((1, dv, Sk), lambda b, h, i: (b, h // group, 0)),
    ]
    args = [qt, k, vt]
    if n_ctx:
        in_specs += [
            pl.BlockSpec((1, n_ctx, dk), lambda b, h, i: (b, 0, h // group)),
            pl.BlockSpec((1, dv, n_ctx), lambda b, h, i: (b, h // group, 0)),
        ]
        args += [kc, vct]
    if sink is not None:
        in_specs.append(pl.BlockSpec((1, 1, 1), lambda b, h, i: (h, 0, 0)))
        args.append((sink * LOG2E).reshape(n_heads, 1, 1).astype(F32))
    return pl.pallas_call(
        functools.partial(_dense_kernel, tk=tk, seq=Sk, n_ctx=n_ctx, has_sink=sink is not None),
        out_shape=jax.ShapeDtypeStruct((B, S, n_heads * dv), BF16),
        grid=(B, n_heads, S // tq),
        in_specs=in_specs,
        out_specs=pl.BlockSpec((1, tq, dv), lambda b, h, i: (b, i, h)),
        compiler_params=_cparams("parallel", "parallel", "arbitrary"),
        name="dense_attn",
    )(*args)


def _out_kernel(oa_ref, ob_ref, oc_ref, od_ref, w_ref, x_ref, gate_ref, g_ref, b_ref, o_ref, *, alpha):
    y = None
    o = 0
    for r in (oa_ref, ob_ref, oc_ref, od_ref):
        n = r.shape[2]
        t = _dot(r[0], w_ref[o:o + n, :])
        y = t if y is None else y + t
        o += n
    z = alpha * x_ref[0] + gate_ref[0] * y
    o_ref[0] = _layer_norm(z) * g_ref[...] + b_ref[...]


def _out_proj(mix, x, gate, sw, layer, alpha, tm):
    B, S, D = x.shape
    const = lambda arr: _layer_spec(arr, layer, lambda bb, i: (0, 0), arr.shape[1:], pipeline_mode=pl.Buffered(1))
    return pl.pallas_call(
        functools.partial(_out_kernel, alpha=alpha),
        out_shape=jax.ShapeDtypeStruct((B, S, D), F32),
        grid=(B, S // tm),
        in_specs=[pl.BlockSpec((1, tm, m.shape[2]), lambda bb, i: (bb, i, 0)) for m in mix] + [
            const(sw["w_out"]),
            pl.BlockSpec((1, tm, D), lambda bb, i: (bb, i, 0)),
            pl.BlockSpec((1, 1, D), lambda bb, i: (bb, 0, 0)),
            const(sw["ln1_g"]), const(sw["ln1_b"]),
        ],
        out_specs=pl.BlockSpec((1, tm, D), lambda bb, i: (bb, i, 0)),
        compiler_params=_cparams("parallel", "parallel"),
        name="out_proj",
    )(*mix, sw["w_out"], x, gate, sw["ln1_g"], sw["ln1_b"])


HALO = BF16_SUBLANES


def _ffn_kernel(x_ref, xp_ref, xn_ref, shift_ref, scale_ref, gate_ref, wg_ref, wu_ref, cw_ref, cb_ref, wd_ref,
                g_ref, b_ref, o_ref, h_ref, acc_ref, *, alpha, tm):
    i, f = pl.program_id(1), pl.program_id(2)

    @pl.when(f == 0)
    def _():
        mod = lambda t: _layer_norm(t) * (1.0 + scale_ref[0]) + shift_ref[0]
        keep_p = (i > 0).astype(F32)
        keep_n = (i < pl.num_programs(1) - 1).astype(F32)
        h_ref[0:HALO, :] = (mod(xp_ref[0]) * keep_p).astype(BF16)
        h_ref[HALO:HALO + tm, :] = mod(x_ref[0]).astype(BF16)
        h_ref[HALO + tm:, :] = (mod(xn_ref[0]) * keep_n).astype(BF16)
        acc_ref[...] = jnp.zeros(acc_ref.shape, F32)

    n = tm + 2 * HALO
    gt = _dot(h_ref[...], wg_ref[...])
    up = _dot(h_ref[HALO:HALO + tm, :], wu_ref[...])
    a = (pltpu.roll(gt, 1, 0) * cw_ref[0:1, :] + gt * cw_ref[1:2, :]
         + pltpu.roll(gt, n - 1, 0) * cw_ref[2:3, :] + cb_ref[...])[HALO:HALO + tm]
    y = (a * jax.nn.sigmoid(a) * up).astype(BF16)
    acc_ref[...] += _dot(y, wd_ref[...])

    @pl.when(f == pl.num_programs(2) - 1)
    def _():
        z = alpha * x_ref[0] + gate_ref[0] * acc_ref[...]
        o_ref[0] = _layer_norm(z) * g_ref[...] + b_ref[...]


def _ffn(x, shift, scale, gate, sw, layer, alpha, tm, tf):
    B, S, D = x.shape
    F = sw["ffn_w_gate"].shape[2]
    assert S % tm == 0 and F % tf == 0 and tm % HALO == 0
    nh = tm // HALO
    vec = pl.BlockSpec((1, 1, D), lambda bb, i, f: (bb, 0, 0))
    cols = lambda arr, rows: _layer_spec(arr, layer, lambda bb, i, f: (0, f), (rows, tf))
    const = lambda arr: _layer_spec(arr, layer, lambda bb, i, f: (0, 0), arr.shape[1:])
    return pl.pallas_call(
        functools.partial(_ffn_kernel, alpha=alpha, tm=tm),
        out_shape=jax.ShapeDtypeStruct((B, S, D), F32),
        grid=(B, S // tm, F // tf),
        in_specs=[
            pl.BlockSpec((1, tm, D), lambda bb, i, f: (bb, i, 0)),
            pl.BlockSpec((1, HALO, D), lambda bb, i, f: (bb, jnp.maximum(i * nh - 1, 0), 0)),
            pl.BlockSpec((1, HALO, D), lambda bb, i, f: (bb, jnp.minimum((i + 1) * nh, S // HALO - 1), 0)),
            vec, vec, vec,
            cols(sw["ffn_w_gate"], D), cols(sw["ffn_w_up"], D), cols(sw["ffn_conv_w"], CONV_W),
            cols(sw["ffn_conv_b"], 1),
            _layer_spec(sw["ffn_w_down"], layer, lambda bb, i, f: (f, 0), (tf, D)),
            const(sw["ln2_g"]), const(sw["ln2_b"]),
        ],
        out_specs=pl.BlockSpec((1, tm, D), lambda bb, i, f: (bb, i, 0)),
        scratch_shapes=[pltpu.VMEM((tm + 2 * HALO, D), BF16), pltpu.VMEM((tm, D), F32)],
        compiler_params=_cparams("parallel", "parallel", "arbitrary"),
        name="conv_ffn",
    )(x, x, x, shift, scale, gate, sw["ffn_w_gate"], sw["ffn_w_up"], sw["ffn_conv_w"], sw["ffn_conv_b"],
      sw["ffn_w_down"], sw["ln2_g"], sw["ln2_b"])


def _rope_tables(S):
    t = jnp.arange(S)
    row, col = (t // GRID_W).astype(F32), (t % GRID_W).astype(F32)
    lane = np.arange(LANE)

    def tables(dim):
        half = dim // 4
        inv_freq = ROPE_THETA ** (-jnp.arange(half, dtype=F32) / half)
        used = lane < dim
        pos = jnp.where(jnp.asarray((lane % dim) < dim // 2)[None, :], row[:, None], col[:, None])
        ang = pos * inv_freq[jnp.asarray(lane % half)][None, :]
        cos, sin = jnp.cos(ang), jnp.sin(ang)
        lo = jnp.asarray(used & ((lane % (2 * half)) < half))[None, :]
        hi = jnp.asarray(used & ((lane % (2 * half)) >= half))[None, :]
        return [jnp.where(jnp.asarray(used)[None, :], cos, 1.0), jnp.where(lo, -sin, 0.0), jnp.where(hi, sin, 0.0)]

    return jnp.stack(tables(HEAD_DIM) + tables(MLA_ROPE)).astype(F32)


def _identity_rope(C):
    one, zero = jnp.ones((C, LANE), F32), jnp.zeros((C, LANE), F32)
    return jnp.stack([one, zero, zero, one, zero, zero])


def _pad_w_in(w):
    cut = _C0 + MLA_Q_LORA + MLA_KV_LORA + MLA_ROPE
    pad = jnp.zeros(w.shape[:2] + (LANE - MLA_ROPE,), w.dtype)
    return jnp.concatenate([w[..., :cut], pad, w[..., cut:]], axis=-1).astype(BF16)


def _pad_w_uq(w):
    L, r, _ = w.shape
    w = w.reshape(L, r, MLA_HEADS, MLA_NOPE + MLA_ROPE)
    w = jnp.concatenate([w, jnp.zeros((L, r, MLA_HEADS, MLA_QK - MLA_NOPE - MLA_ROPE), w.dtype)], axis=3)
    return w.reshape(L, r, MLA_HEADS * MLA_QK).astype(BF16)


def kernel(x, c, ctx, c_ctx, w_ada, b_ada, w_in, na_rpb, swa_sink, mla_q_norm, mla_kv_norm, mla_w_uq, mla_w_ukv,
           gqa_q_norm, gqa_k_norm, w_out, ln1_g, ln1_b, ffn_w_gate, ffn_w_up, ffn_conv_w, ffn_conv_b, ffn_w_down,
           ln2_g, ln2_b):
    B, S, D = x.shape
    C = ctx.shape[1]
    depth = w_ada.shape[0]
    alpha = (2 * depth) ** 0.25

    cvec = jnp.concatenate([c, c_ctx[None, :], jnp.zeros((8 - B - 1, D), F32)], axis=0)
    ada = _ada(cvec, w_ada, b_ada)
    rope_x, rope_c = _rope_tables(S), _identity_rope(C)

    row = lambda a: a[:, None, :]
    sw = {
        "w_in": _pad_w_in(w_in), "w_uq": _pad_w_uq(mla_w_uq), "w_ukv": mla_w_ukv.astype(BF16),
        "mla_q_norm": row(mla_q_norm), "mla_kv_norm": row(mla_kv_norm),
        "gqa_q_norm": row(gqa_q_norm), "gqa_k_norm": row(gqa_k_norm),
        "w_out": w_out.astype(BF16), "ln1_g": row(ln1_g), "ln1_b": row(ln1_b),
        "ffn_w_gate": ffn_w_gate.astype(BF16), "ffn_w_up": ffn_w_up.astype(BF16), "ffn_conv_w": ffn_conv_w,
        "ffn_conv_b": row(ffn_conv_b), "ffn_w_down": ffn_w_down.astype(BF16), "ln2_g": row(ln2_g), "ln2_b": row(ln2_b),
    }

    tm_proj = min(256, S)
    tm_out = min(512, S)
    tm_ffn, tf = min(512, S), 512
    tq_dense, tk_dense = min(512, S), min(512, S)
    tq_swa = min(512, S - 2 * SWA_WINDOW)
    g_swa, g_gqa = SWA_HEADS // SWA_KV_HEADS, GQA_HEADS // GQA_KV_HEADS

    for i in range(depth):
        need_ctx = i < depth - 1
        mx = ada[i, :B].reshape(B, 1, 6, D)
        mod_x = [mx[:, :, j] for j in range(6)]
        mod_c = [jnp.broadcast_to(ada[i, B].reshape(1, 1, 6, D)[:, :, j], (B, 1, D)) for j in range(6)]

        qa, ka, va, qb, kb, vb, qc, kc, vc, qd, kd, vd = _proj(x, mod_x[0], mod_x[1], rope_x, sw, i, tm_proj)
        qa_c, ka_c, va_c, qb_c, kb_c, vb_c, qc_c, kc_c, vc_c, qd_c, kd_c, vd_c = _proj(
            ctx, mod_c[0], mod_c[1], rope_c, sw, i, C)

        mix_x = (
            _na_attention(qa, ka, va, ka_c, va_c, na_rpb[i]),
            _swa_attention(qb, kb, vb, kb_c, vb_c, swa_sink[i], tq_swa),
            _dense_attention(qc, kc, vc, kc_c, vc_c, MLA_HEADS, 1, tq_dense, tk_dense),
            _dense_attention(qd, kd, vd, kd_c, vd_c, GQA_HEADS, g_gqa, tq_dense, tk_dense),
        )
        x = _out_proj(mix_x, x, mod_x[2], sw, i, alpha, tm_out)
        if need_ctx:
            mix_c = (
                _dense_attention(qa_c, ka_c, va_c, None, None, NA_HEADS, 1, C, C),
                _dense_attention(qb_c, kb_c, vb_c, None, None, SWA_HEADS, g_swa, C, C, sink=swa_sink[i]),
                _dense_attention(qc_c, kc_c, vc_c, None, None, MLA_HEADS, 1, C, C),
                _dense_attention(qd_c, kd_c, vd_c, None, None, GQA_HEADS, g_gqa, C, C),
            )
            ctx = _out_proj(mix_c, ctx, mod_c[2], sw, i, alpha, C)

        x = _ffn(x, mod_x[3], mod_x[4], mod_x[5], sw, i, alpha, tm_ffn, tf)
        if need_ctx:
            ctx = _ffn(ctx, mod_c[3], mod_c[4], mod_c[5], sw, i, alpha, C, tf)
    return x
```

```python
import functools
import math

import numpy as np
import jax
import jax.numpy as jnp
from jax import lax
from jax.experimental import pallas as pl
from jax.experimental.pallas import tpu as pltpu

GRID_W = 64
HEAD_DIM = 128
NA_HEADS = 4
NA_WIN_R = 8
NA_WIN_C = 16
SWA_HEADS = 4
SWA_KV_HEADS = 2
SWA_WINDOW = 128
MLA_HEADS = 4
MLA_Q_LORA = 384
MLA_KV_LORA = 128
MLA_NOPE = 128
MLA_ROPE = 64
MLA_V = 128
GQA_HEADS = 4
GQA_KV_HEADS = 2
CONV_W = 3
ROPE_THETA = 10000.0
EPS = 1e-6
NEG = -1e30
LOG2E = math.log2(math.e)

LANE = 128
BF16_SUBLANES = 16
VMEM_LIMIT = 56 * 1024 * 1024

F32 = jnp.float32
BF16 = jnp.bfloat16

OUT_SUB_ROWS = 256
NA_Q_ROWS = 4
NA_KEY_ROWS = NA_WIN_R + NA_Q_ROWS
NA_TILES_PER_STEP = 8
SWA_TILES_PER_STEP = 4


def _cparams(*sem):
    return pltpu.CompilerParams(dimension_semantics=sem, vmem_limit_bytes=VMEM_LIMIT)


def _dot(a, b):
    return jnp.dot(a, b, preferred_element_type=F32)


def _layer_norm(x):
    mu = jnp.mean(x, axis=-1, keepdims=True)
    xc = x - mu
    var = jnp.mean(xc * xc, axis=-1, keepdims=True)
    return xc * lax.rsqrt(var + EPS)


def _rms_norm(x, g):
    return x * lax.rsqrt(jnp.mean(x * x, axis=-1, keepdims=True) + EPS) * g


def _rope(x, cos, sin_lo, sin_hi, half):
    n = x.shape[-1]
    return x * cos + pltpu.roll(x, n - half, 1) * sin_lo + pltpu.roll(x, half, 1) * sin_hi


def _layer_spec(arr, layer, index_map_rest, block_rest, **kw):
    return pl.BlockSpec((None,) + tuple(block_rest), lambda *g: (layer,) + tuple(index_map_rest(*g)), **kw)


def _ada_kernel(c_ref, w_ref, b_ref, o_ref):
    c = c_ref[...]
    a = (c * jax.nn.sigmoid(c)).astype(BF16)
    o_ref[0] = _dot(a, w_ref[0].astype(BF16)) + b_ref[0]


def _ada(cvec, w_ada, b_ada):
    L, D, N = w_ada.shape
    M = cvec.shape[0]
    tn = 1024
    return pl.pallas_call(
        _ada_kernel,
        out_shape=jax.ShapeDtypeStruct((L, M, N), F32),
        grid=(L, N // tn),
        in_specs=[
            pl.BlockSpec((M, D), lambda l, j: (0, 0)),
            pl.BlockSpec((1, D, tn), lambda l, j: (l, 0, j)),
            pl.BlockSpec((1, 1, tn), lambda l, j: (l, 0, j)),
        ],
        out_specs=pl.BlockSpec((1, M, tn), lambda l, j: (l, 0, j)),
        compiler_params=_cparams("parallel", "parallel"),
        name="ada",
    )(cvec, w_ada, b_ada.reshape(L, 1, N))


_A0 = 0
_B0 = _A0 + 3 * NA_HEADS * HEAD_DIM
_C0 = _B0 + (SWA_HEADS + 2 * SWA_KV_HEADS) * HEAD_DIM
_C_W = MLA_Q_LORA + MLA_KV_LORA + LANE
_D0 = _C0 + _C_W
_D_W = (GQA_HEADS + 2 * GQA_KV_HEADS) * HEAD_DIM
IN_COLS_PAD = _D0 + _D_W
MLA_QK = 2 * LANE


def _proj_kernel(x_ref, shift_ref, scale_ref, w_ref, rope_ref, gq_lora_ref, gkv_lora_ref, gq_ref, gk_ref,
                 wuq_ref, wukv_ref,
                 qa_ref, ka_ref, va_ref, qb_ref, kb_ref, vb_ref, qc_ref, kc_ref, vc_ref, qd_ref, kd_ref, vd_ref):
    h = (_layer_norm(x_ref[0]) * (1.0 + scale_ref[0]) + shift_ref[0]).astype(BF16)
    cos2, slo2, shi2 = rope_ref[0], rope_ref[1], rope_ref[2]
    cos1, slo1, shi1 = rope_ref[3], rope_ref[4], rope_ref[5]
    rope2 = lambda t: _rope(t, cos2, slo2, shi2, HEAD_DIM // 4)
    rope1 = lambda t: _rope(t, cos1, slo1, shi1, MLA_ROPE // 4)
    hd = HEAD_DIM
    sc = hd ** -0.5 * LOG2E

    def store_t(ref, lo, t):
        ref[0, lo:lo + t.shape[1], :] = t.T.astype(BF16)

    w = NA_HEADS * hd
    pa = _dot(h, w_ref[:, _A0:_A0 + 3 * w])
    store_t(qa_ref, 0, pa[:, :w] * sc)
    ka_ref[0] = pa[:, w:2 * w].astype(BF16)
    store_t(va_ref, 0, pa[:, 2 * w:])

    wq, wk = SWA_HEADS * hd, SWA_KV_HEADS * hd
    pb = _dot(h, w_ref[:, _B0:_B0 + wq + 2 * wk])
    for i in range(SWA_HEADS):
        store_t(qb_ref, i * hd, rope2(pb[:, i * hd:(i + 1) * hd]) * sc)
    for i in range(SWA_KV_HEADS):
        kb_ref[0, :, i * hd:(i + 1) * hd] = rope2(pb[:, wq + i * hd:wq + (i + 1) * hd]).astype(BF16)
    store_t(vb_ref, 0, pb[:, wq + wk:])

    pc = _dot(h, w_ref[:, _C0:_C0 + _C_W])
    cq = _rms_norm(pc[:, :MLA_Q_LORA], gq_lora_ref[...]).astype(BF16)
    ckv = _rms_norm(pc[:, MLA_Q_LORA:MLA_Q_LORA + MLA_KV_LORA], gkv_lora_ref[...]).astype(BF16)
    kpe = rope1(pc[:, MLA_Q_LORA + MLA_KV_LORA:]).astype(BF16)
    qup = _dot(cq, wuq_ref[...])
    kvup = _dot(ckv, wukv_ref[...])
    sc_mla = (MLA_NOPE + MLA_ROPE) ** -0.5 * LOG2E
    for i in range(MLA_HEADS):
        o = i * MLA_QK
        store_t(qc_ref, o, qup[:, o:o + LANE] * sc_mla)
        store_t(qc_ref, o + LANE, rope1(qup[:, o + LANE:o + 2 * LANE]) * sc_mla)
        kc_ref[0, :, o:o + LANE] = kvup[:, o:o + LANE].astype(BF16)
        kc_ref[0, :, o + LANE:o + 2 * LANE] = kpe
        store_t(vc_ref, i * MLA_V, kvup[:, o + LANE:o + 2 * LANE])

    wq, wk = GQA_HEADS * hd, GQA_KV_HEADS * hd
    pd = _dot(h, w_ref[:, _D0:_D0 + wq + 2 * wk])
    for i in range(GQA_HEADS):
        store_t(qd_ref, i * hd, rope2(_rms_norm(pd[:, i * hd:(i + 1) * hd], gq_ref[...])) * sc)
    for i in range(GQA_KV_HEADS):
        t = _rms_norm(pd[:, wq + i * hd:wq + (i + 1) * hd], gk_ref[...])
        kd_ref[0, :, i * hd:(i + 1) * hd] = rope2(t).astype(BF16)
    store_t(vd_ref, 0, pd[:, wq + wk:])


def _proj(x, shift, scale, rope_tab, sw, layer, tm):
    B, S, D = x.shape
    widths = (NA_HEADS * HEAD_DIM,) * 3 + (SWA_HEADS * HEAD_DIM, SWA_KV_HEADS * HEAD_DIM, SWA_KV_HEADS * HEAD_DIM) \
        + (MLA_HEADS * MLA_QK, MLA_HEADS * MLA_QK, MLA_HEADS * MLA_V) \
        + (GQA_HEADS * HEAD_DIM, GQA_KV_HEADS * HEAD_DIM, GQA_KV_HEADS * HEAD_DIM)
    transposed = (True, False, True) * 4
    const = lambda arr: _layer_spec(arr, layer, lambda b, i: (0, 0), arr.shape[1:], pipeline_mode=pl.Buffered(1))
    vec = pl.BlockSpec((1, 1, D), lambda b, i: (b, 0, 0))
    out_spec = lambda n, t: (pl.BlockSpec((1, n, tm), lambda b, i: (b, 0, i)) if t
                             else pl.BlockSpec((1, tm, n), lambda b, i: (b, i, 0)))
    params = (sw["w_in"], None, sw["mla_q_norm"], sw["mla_kv_norm"], sw["gqa_q_norm"], sw["gqa_k_norm"],
              sw["w_uq"], sw["w_ukv"])
    in_specs = [pl.BlockSpec((1, tm, D), lambda b, i: (b, i, 0)), vec, vec]
    in_specs += [pl.BlockSpec((6, tm, LANE), lambda b, i: (0, i, 0)) if p is None else const(p) for p in params]
    args = [rope_tab if p is None else p for p in params]
    return pl.pallas_call(
        _proj_kernel,
        out_shape=[jax.ShapeDtypeStruct((B, n, S) if t else (B, S, n), BF16) for n, t in zip(widths, transposed)],
        grid=(B, S // tm),
        in_specs=in_specs,
        out_specs=[out_spec(n, t) for n, t in zip(widths, transposed)],
        compiler_params=_cparams("parallel", "parallel"),
        name="proj",
    )(x, shift, scale, *args)


def _softmax_pv(parts, sink=None):
    m = parts[0][0].max(axis=0, keepdims=True)
    for s, _ in parts[1:]:
        m = jnp.maximum(m, s.max(axis=0, keepdims=True))
    l = 0.0
    if sink is not None:
        m = jnp.maximum(m, sink)
        l = jnp.exp2(sink - m)
    acc = None
    for s, vt in parts:
        p = jnp.exp2(s - m)
        l = l + p.sum(axis=0, keepdims=True)
        pv = _dot(vt, p.astype(BF16))
        acc = pv if acc is None else acc + pv
    return acc / l


def _na_kernel(cls_ref, ws_ref, qt_ref, k_ref, vt_ref, kc_ref, vct_ref, *rest):
    del cls_ref
    bias_refs, o_ref = rest[:-1], rest[-1]
    nk, tq = NA_KEY_ROWS * GRID_W, NA_Q_ROWS * GRID_W
    for t, bias_ref in enumerate(bias_refs):
        start = pl.multiple_of(ws_ref[pl.program_id(2) * len(bias_refs) + t] * GRID_W, tq)
        qt = qt_ref[0, :, t * tq:(t + 1) * tq]
        s = _dot(k_ref[0, pl.ds(start, nk), :], qt) + bias_ref[0, 0]
        s_ctx = _dot(kc_ref[0], qt)
        o = _softmax_pv([(s, vt_ref[0, :, pl.ds(start, nk)]), (s_ctx, vct_ref[0])])
        o_ref[0, t * tq:(t + 1) * tq, :] = o.T.astype(o_ref.dtype)


def _na_tables(rpb, rows):
    nt = rows // NA_Q_ROWS
    R = np.arange(nt) * NA_Q_ROWS
    ws = np.clip(R - NA_WIN_R // 2, 0, rows - NA_KEY_ROWS)
    qr = (R[:, None] + np.arange(NA_Q_ROWS)[None, :])
    kr = ws[:, None] + np.arange(NA_KEY_ROWS)[None, :]
    r0 = np.clip(qr - NA_WIN_R // 2, 0, rows - NA_WIN_R)
    row_ok = (kr[:, None, :] >= r0[:, :, None]) & (kr[:, None, :] < r0[:, :, None] + NA_WIN_R)
    drow = np.clip(kr[:, None, :] - qr[:, :, None] + NA_WIN_R - 1, 0, 2 * NA_WIN_R - 2)
    geom = np.concatenate([row_ok.reshape(nt, -1).astype(np.int64), drow.reshape(nt, -1)], axis=1)
    _, first, cls = np.unique(geom, axis=0, return_index=True, return_inverse=True)
    cq = np.arange(GRID_W)
    c0 = np.clip(cq - NA_WIN_C // 2, 0, GRID_W - NA_WIN_C)
    col_ok = (cq[None, :] >= c0[:, None]) & (cq[None, :] < c0[:, None] + NA_WIN_C)
    dcol = np.clip(cq[None, :] - cq[:, None] + NA_WIN_C - 1, 0, 2 * NA_WIN_C - 2)
    nc, H = len(first), rpb.shape[0]
    cols = jnp.zeros(rpb.shape[:2] + dcol.shape, F32)
    for j in range(rpb.shape[2]):
        cols = jnp.where(jnp.asarray(dcol == j)[None, None], rpb[:, :, j][:, :, None, None] * LOG2E, cols)
    cols = jnp.where(jnp.asarray(col_ok)[None, None], cols, NEG)
    blocks = jnp.stack([cols[:, d] for d in drow[first].reshape(-1)], axis=1)
    blocks = blocks.reshape(H, nc, NA_Q_ROWS, NA_KEY_ROWS, GRID_W, GRID_W)
    blocks = jnp.where(jnp.asarray(row_ok[first])[None, :, :, :, None, None], blocks, NEG)
    bias = blocks.transpose(1, 0, 3, 5, 2, 4).reshape(nc, H, NA_KEY_ROWS * GRID_W, NA_Q_ROWS * GRID_W)
    return bias.astype(F32), jnp.asarray(cls.reshape(-1), jnp.int32), jnp.asarray(ws, jnp.int32)


def _na_attention(qt, k, vt, kc, vct, rpb):
    B, _, S = qt.shape
    C = kc.shape[1]
    rows = S // GRID_W
    assert rows >= NA_KEY_ROWS and rows % NA_Q_ROWS == 0
    bias, cls, ws = _na_tables(rpb, rows)
    tq, nk, hd = NA_Q_ROWS * GRID_W, NA_KEY_ROWS * GRID_W, HEAD_DIM
    n_sub = NA_TILES_PER_STEP if (S // tq) % NA_TILES_PER_STEP == 0 else 1
    bias_spec = lambda t: pl.BlockSpec((1, 1, nk, tq), lambda b, h, i, c, w: (c[i * n_sub + t], h, 0, 0))
    grid_spec = pltpu.PrefetchScalarGridSpec(
        num_scalar_prefetch=2,
        grid=(B, NA_HEADS, S // (tq * n_sub)),
        in_specs=[
            pl.BlockSpec((1, hd, n_sub * tq), lambda b, h, i, c, w: (b, h, i)),
            pl.BlockSpec((1, S, hd), lambda b, h, i, c, w: (b, 0, h)),
            pl.BlockSpec((1, hd, S), lambda b, h, i, c, w: (b, h, 0)),
            pl.BlockSpec((1, C, hd), lambda b, h, i, c, w: (b, 0, h)),
            pl.BlockSpec((1, hd, C), lambda b, h, i, c, w: (b, h, 0)),
        ] + [bias_spec(t) for t in range(n_sub)],
        out_specs=pl.BlockSpec((1, n_sub * tq, hd), lambda b, h, i, c, w: (b, i, h)),
    )
    return pl.pallas_call(
        _na_kernel,
        out_shape=jax.ShapeDtypeStruct((B, S, NA_HEADS * hd), BF16),
        grid_spec=grid_spec,
        compiler_params=_cparams("parallel", "parallel", "arbitrary"),
        name="na_attn",
    )(cls, ws, qt, k, vt, kc, vct, *([bias] * n_sub))


def _swa_kernel(qt_ref, k_ref, vt_ref, kc_ref, vct_ref, sink_ref, o_ref, *, tq, n_sub, seq):
    nk = tq + 2 * SWA_WINDOW
    for t in range(n_sub):
        i = pl.program_id(2) * n_sub + t
        start = pl.multiple_of(jnp.clip(i * tq - SWA_WINDOW, 0, seq - nk), SWA_WINDOW)
        qt = qt_ref[0, :, t * tq:(t + 1) * tq]
        s = _dot(k_ref[0, pl.ds(start, nk), :], qt)
        kpos = start + lax.broadcasted_iota(jnp.int32, (nk, tq), 0)
        qpos = i * tq + lax.broadcasted_iota(jnp.int32, (nk, tq), 1)
        s = jnp.where(jnp.abs(qpos - kpos) <= SWA_WINDOW, s, NEG)
        s_ctx = _dot(kc_ref[0], qt)
        o = _softmax_pv([(s, vt_ref[0, :, pl.ds(start, nk)]), (s_ctx, vct_ref[0])], sink=sink_ref[0])
        o_ref[0, t * tq:(t + 1) * tq, :] = o.T.astype(o_ref.dtype)


def _swa_attention(qt, k, vt, kc, vct, sink, tq):
    B, _, S = qt.shape
    C = kc.shape[1]
    hd, G = HEAD_DIM, SWA_HEADS // SWA_KV_HEADS
    assert S >= tq + 2 * SWA_WINDOW and S % tq == 0 and tq % SWA_WINDOW == 0
    n_sub = SWA_TILES_PER_STEP if (S // tq) % SWA_TILES_PER_STEP == 0 else 1
    return pl.pallas_call(
        functools.partial(_swa_kernel, tq=tq, n_sub=n_sub, seq=S),
        out_shape=jax.ShapeDtypeStruct((B, S, SWA_HEADS * hd), BF16),
        grid=(B, SWA_HEADS, S // (tq * n_sub)),
        in_specs=[
            pl.BlockSpec((1, hd, n_sub * tq), lambda b, h, i: (b, h, i)),
            pl.BlockSpec((1, S, hd), lambda b, h, i: (b, 0, h // G)),
            pl.BlockSpec((1, hd, S), lambda b, h, i: (b, h // G, 0)),
            pl.BlockSpec((1, C, hd), lambda b, h, i: (b, 0, h // G)),
            pl.BlockSpec((1, hd, C), lambda b, h, i: (b, h // G, 0)),
            pl.BlockSpec((1, 1, 1), lambda b, h, i: (h, 0, 0)),
        ],
        out_specs=pl.BlockSpec((1, n_sub * tq, hd), lambda b, h, i: (b, i, h)),
        compiler_params=_cparams("parallel", "parallel", "arbitrary"),
        name="swa_attn",
    )(qt, k, vt, kc, vct, (sink * LOG2E).reshape(SWA_HEADS, 1, 1).astype(F32))


def _dense_kernel(*refs, tk, seq, n_ctx, has_sink):
    refs = list(refs)
    qt_ref, k_ref, vt_ref = refs[:3]
    o_ref = refs[-1]
    kc_ref, vct_ref = refs[3:5] if n_ctx else (None, None)
    qt = qt_ref[0]
    tq = qt.shape[1]
    if has_sink:
        m = jnp.broadcast_to(refs[-2][0], (1, tq))
        l = jnp.ones((1, tq), F32)
    else:
        m = jnp.full((1, tq), -jnp.inf, F32)
        l = jnp.zeros((1, tq), F32)
    acc = jnp.zeros((vt_ref.shape[1], tq), F32)

    chunks = [(k_ref, vt_ref, j * tk, tk) for j in range(seq // tk)]
    if n_ctx:
        chunks.append((kc_ref, vct_ref, 0, n_ctx))
    scores = lambda c: _dot(c[0][0, c[2]:c[2] + c[3], :], qt)

    s_next = scores(chunks[0])
    for j, c in enumerate(chunks):
        s = s_next
        if j + 1 < len(chunks):
            s_next = scores(chunks[j + 1])
        m_new = jnp.maximum(m, s.max(axis=0, keepdims=True))
        alpha = jnp.exp2(m - m_new)
        p = jnp.exp2(s - m_new)
        l = alpha * l + p.sum(axis=0, keepdims=True)
        acc = alpha * acc + _dot(c[1][0, :, c[2]:c[2] + c[3]], p.astype(BF16))
        m = m_new
    o_ref[0] = (acc / l).T.astype(o_ref.dtype)


def _dense_attention(qt, k, vt, kc, vct, n_heads, group, tq, tk, sink=None):
    B, _, S = qt.shape
    Sk = k.shape[1]
    dk = qt.shape[1] // n_heads
    dv = vt.shape[1] // (n_heads // group)
    assert S % tq == 0 and Sk % tk == 0
    n_ctx = 0 if kc is None else kc.shape[1]
    in_specs = [
        pl.BlockSpec((1, dk, tq), lambda b, h, i: (b, h, i)),
        pl.BlockSpec((1, Sk, dk), lambda b, h, i: (b, 0, h // group)),
        pl.BlockSpec((1, dv, Sk), lambda b, h, i: (b, h // group, 0)),
    ]
    args = [qt, k, vt]
    if n_ctx:
        in_specs += [
            pl.BlockSpec((1, n_ctx, dk), lambda b, h, i: (b, 0, h // group)),
            pl.BlockSpec((1, dv, n_ctx), lambda b, h, i: (b, h // group, 0)),
        ]
        args += [kc, vct]
    if sink is not None:
        in_specs.append(pl.BlockSpec((1, 1, 1), lambda b, h, i: (h, 0, 0)))
        args.append((sink * LOG2E).reshape(n_heads, 1, 1).astype(F32))
    return pl.pallas_call(
        functools.partial(_dense_kernel, tk=tk, seq=Sk, n_ctx=n_ctx, has_sink=sink is not None),
        out_shape=jax.ShapeDtypeStruct((B, S, n_heads * dv), BF16),
        grid=(B, n_heads, S // tq),
        in_specs=in_specs,
        out_specs=pl.BlockSpec((1, tq, dv), lambda b, h, i: (b, i, h)),
        compiler_params=_cparams("parallel", "parallel", "arbitrary"),
        name="dense_attn",
    )(*args)


def _out_kernel(oa_ref, ob_ref, oc_ref, od_ref, w_ref, x_ref, gate_ref, g_ref, b_ref, o_ref, *, alpha, sub):
    for r0 in range(0, x_ref.shape[1], sub):
        rows = slice(r0, r0 + sub)
        y = None
        o = 0
        for r in (oa_ref, ob_ref, oc_ref, od_ref):
            n = r.shape[2]
            t = _dot(r[0, rows, :], w_ref[o:o + n, :])
            y = t if y is None else y + t
            o += n
        z = alpha * x_ref[0, rows, :] + gate_ref[0] * y
        o_ref[0, rows, :] = _layer_norm(z) * g_ref[...] + b_ref[...]


def _out_proj(mix, x, gate, sw, layer, alpha, tm):
    B, S, D = x.shape
    const = lambda arr: _layer_spec(arr, layer, lambda bb, i: (0, 0), arr.shape[1:], pipeline_mode=pl.Buffered(1))
    return pl.pallas_call(
        functools.partial(_out_kernel, alpha=alpha, sub=min(OUT_SUB_ROWS, tm)),
        out_shape=jax.ShapeDtypeStruct((B, S, D), F32),
        grid=(B, S // tm),
        in_specs=[pl.BlockSpec((1, tm, m.shape[2]), lambda bb, i: (bb, i, 0)) for m in mix] + [
            const(sw["w_out"]),
            pl.BlockSpec((1, tm, D), lambda bb, i: (bb, i, 0)),
            pl.BlockSpec((1, 1, D), lambda bb, i: (bb, 0, 0)),
            const(sw["ln1_g"]), const(sw["ln1_b"]),
        ],
        out_specs=pl.BlockSpec((1, tm, D), lambda bb, i: (bb, i, 0)),
        compiler_params=_cparams("parallel", "parallel"),
        name="out_proj",
    )(*mix, sw["w_out"], x, gate, sw["ln1_g"], sw["ln1_b"])


HALO = BF16_SUBLANES


def _ffn_kernel(x_ref, xp_ref, xn_ref, shift_ref, scale_ref, gate_ref, wg_ref, wu_ref, cw_ref, cb_ref, wd_ref,
                g_ref, b_ref, o_ref, h_ref, acc_ref, *, alpha, tm):
    i, f = pl.program_id(1), pl.program_id(2)

    @pl.when(f == 0)
    def _():
        mod = lambda t: _layer_norm(t) * (1.0 + scale_ref[0]) + shift_ref[0]
        keep_p = (i > 0).astype(F32)
        keep_n = (i < pl.num_programs(1) - 1).astype(F32)
        h_ref[0:HALO, :] = (mod(xp_ref[0]) * keep_p).astype(BF16)
        h_ref[HALO:HALO + tm, :] = mod(x_ref[0]).astype(BF16)
        h_ref[HALO + tm:, :] = (mod(xn_ref[0]) * keep_n).astype(BF16)
        acc_ref[...] = jnp.zeros(acc_ref.shape, F32)

    n = tm + 2 * HALO
    gt = _dot(h_ref[...], wg_ref[...])
    up = _dot(h_ref[HALO:HALO + tm, :], wu_ref[...])
    a = (pltpu.roll(gt, 1, 0) * cw_ref[0:1, :] + gt * cw_ref[1:2, :]
         + pltpu.roll(gt, n - 1, 0) * cw_ref[2:3, :] + cb_ref[...])[HALO:HALO + tm]
    y = (a * jax.nn.sigmoid(a) * up).astype(BF16)
    acc_ref[...] += _dot(y, wd_ref[...])

    @pl.when(f == pl.num_programs(2) - 1)
    def _():
        z = alpha * x_ref[0] + gate_ref[0] * acc_ref[...]
        o_ref[0] = _layer_norm(z) * g_ref[...] + b_ref[...]


def _ffn(x, shift, scale, gate, sw, layer, alpha, tm, tf):
    B, S, D = x.shape
    F = sw["ffn_w_gate"].shape[2]
    assert S % tm == 0 and F % tf == 0 and tm % HALO == 0
    nh = tm // HALO
    vec = pl.BlockSpec((1, 1, D), lambda bb, i, f: (bb, 0, 0))
    cols = lambda arr, rows: _layer_spec(arr, layer, lambda bb, i, f: (0, f), (rows, tf))
    const = lambda arr: _layer_spec(arr, layer, lambda bb, i, f: (0, 0), arr.shape[1:])
    return pl.pallas_call(
        functools.partial(_ffn_kernel, alpha=alpha, tm=tm),
        out_shape=jax.ShapeDtypeStruct((B, S, D), F32),
        grid=(B, S // tm, F // tf),
        in_specs=[
            pl.BlockSpec((1, tm, D), lambda bb, i, f: (bb, i, 0)),
            pl.BlockSpec((1, HALO, D), lambda bb, i, f: (bb, jnp.maximum(i * nh - 1, 0), 0)),
            pl.BlockSpec((1, HALO, D), lambda bb, i, f: (bb, jnp.minimum((i + 1) * nh, S // HALO - 1), 0)),
            vec, vec, vec,
            cols(sw["ffn_w_gate"], D), cols(sw["ffn_w_up"], D), cols(sw["ffn_conv_w"], CONV_W),
            cols(sw["ffn_conv_b"], 1),
            _layer_spec(sw["ffn_w_down"], layer, lambda bb, i, f: (f, 0), (tf, D)),
            const(sw["ln2_g"]), const(sw["ln2_b"]),
        ],
        out_specs=pl.BlockSpec((1, tm, D), lambda bb, i, f: (bb, i, 0)),
        scratch_shapes=[pltpu.VMEM((tm + 2 * HALO, D), BF16), pltpu.VMEM((tm, D), F32)],
        compiler_params=_cparams("parallel", "parallel", "arbitrary"),
        name="conv_ffn",
    )(x, x, x, shift, scale, gate, sw["ffn_w_gate"], sw["ffn_w_up"], sw["ffn_conv_w"], sw["ffn_conv_b"],
      sw["ffn_w_down"], sw["ln2_g"], sw["ln2_b"])


def _rope_tables(S):
    t = jnp.arange(S)
    row, col = (t // GRID_W).astype(F32), (t % GRID_W).astype(F32)
    lane = np.arange(LANE)

    def tables(dim):
        half = dim // 4
        inv_freq = ROPE_THETA ** (-jnp.arange(half, dtype=F32) / half)
        used = lane < dim
        pos = jnp.where(jnp.asarray((lane % dim) < dim // 2)[None, :], row[:, None], col[:, None])
        ang = pos * inv_freq[jnp.asarray(lane % half)][None, :]
        cos, sin = jnp.cos(ang), jnp.sin(ang)
        lo = jnp.asarray(used & ((lane % (2 * half)) < half))[None, :]
        hi = jnp.asarray(used & ((lane % (2 * half)) >= half))[None, :]
        return [jnp.where(jnp.asarray(used)[None, :], cos, 1.0), jnp.where(lo, -sin, 0.0), jnp.where(hi, sin, 0.0)]

    return jnp.stack(tables(HEAD_DIM) + tables(MLA_ROPE)).astype(F32)


def _identity_rope(C):
    one, zero = jnp.ones((C, LANE), F32), jnp.zeros((C, LANE), F32)
    return jnp.stack([one, zero, zero, one, zero, zero])


def _pad_w_in(w):
    cut = _C0 + MLA_Q_LORA + MLA_KV_LORA + MLA_ROPE
    pad = jnp.zeros(w.shape[:2] + (LANE - MLA_ROPE,), w.dtype)
    return jnp.concatenate([w[..., :cut], pad, w[..., cut:]], axis=-1).astype(BF16)


def _pad_w_uq(w):
    L, r, _ = w.shape
    w = w.reshape(L, r, MLA_HEADS, MLA_NOPE + MLA_ROPE)
    w = jnp.concatenate([w, jnp.zeros((L, r, MLA_HEADS, MLA_QK - MLA_NOPE - MLA_ROPE), w.dtype)], axis=3)
    return w.reshape(L, r, MLA_HEADS * MLA_QK).astype(BF16)


def kernel(x, c, ctx, c_ctx, w_ada, b_ada, w_in, na_rpb, swa_sink, mla_q_norm, mla_kv_norm, mla_w_uq, mla_w_ukv,
           gqa_q_norm, gqa_k_norm, w_out, ln1_g, ln1_b, ffn_w_gate, ffn_w_up, ffn_conv_w, ffn_conv_b, ffn_w_down,
           ln2_g, ln2_b):
    B, S, D = x.shape
    C = ctx.shape[1]
    depth = w_ada.shape[0]
    alpha = (2 * depth) ** 0.25

    cvec = jnp.concatenate([c, c_ctx[None, :], jnp.zeros((8 - B - 1, D), F32)], axis=0)
    ada = _ada(cvec, w_ada, b_ada)
    rope_x, rope_c = _rope_tables(S), _identity_rope(C)

    row = lambda a: a[:, None, :]
    sw = {
        "w_in": _pad_w_in(w_in), "w_uq": _pad_w_uq(mla_w_uq), "w_ukv": mla_w_ukv.astype(BF16),
        "mla_q_norm": row(mla_q_norm), "mla_kv_norm": row(mla_kv_norm),
        "gqa_q_norm": row(gqa_q_norm), "gqa_k_norm": row(gqa_k_norm),
        "w_out": w_out.astype(BF16), "ln1_g": row(ln1_g), "ln1_b": row(ln1_b),
        "ffn_w_gate": ffn_w_gate.astype(BF16), "ffn_w_up": ffn_w_up.astype(BF16), "ffn_conv_w": ffn_conv_w,
        "ffn_conv_b": row(ffn_conv_b), "ffn_w_down": ffn_w_down.astype(BF16), "ln2_g": row(ln2_g), "ln2_b": row(ln2_b),
    }

    tm_proj = min(256, S)
    tm_out = min(512, S)
    tm_ffn, tf = min(512, S), 512
    tq_dense, tk_dense = min(1024, S), min(512, S)
    tq_swa = min(512, S - 2 * SWA_WINDOW)
    g_swa, g_gqa = SWA_HEADS // SWA_KV_HEADS, GQA_HEADS // GQA_KV_HEADS

    for i in range(depth):
        need_ctx = i < depth - 1
        mx = ada[i, :B].reshape(B, 1, 6, D)
        mod_x = [mx[:, :, j] for j in range(6)]
        mod_c = [jnp.broadcast_to(ada[i, B].reshape(1, 1, 6, D)[:, :, j], (B, 1, D)) for j in range(6)]

        qa, ka, va, qb, kb, vb, qc, kc, vc, qd, kd, vd = _proj(x, mod_x[0], mod_x[1], rope_x, sw, i, tm_proj)
        qa_c, ka_c, va_c, qb_c, kb_c, vb_c, qc_c, kc_c, vc_c, qd_c, kd_c, vd_c = _proj(
            ctx, mod_c[0], mod_c[1], rope_c, sw, i, C)

        mix_x = (
            _na_attention(qa, ka, va, ka_c, va_c, na_rpb[i]),
            _swa_attention(qb, kb, vb, kb_c, vb_c, swa_sink[i], tq_swa),
            _dense_attention(qc, kc, vc, kc_c, vc_c, MLA_HEADS, 1, tq_dense, tk_dense),
            _dense_attention(qd, kd, vd, kd_c, vd_c, GQA_HEADS, g_gqa, tq_dense, tk_dense),
        )
        x = _out_proj(mix_x, x, mod_x[2], sw, i, alpha, tm_out)
        if need_ctx:
            mix_c = (
                _dense_attention(qa_c, ka_c, va_c, None, None, NA_HEADS, 1, C, C),
                _dense_attention(qb_c, kb_c, vb_c, None, None, SWA_HEADS, g_swa, C, C, sink=swa_sink[i]),
                _dense_attention(qc_c, kc_c, vc_c, None, None, MLA_HEADS, 1, C, C),
                _dense_attention(qd_c, kd_c, vd_c, None, None, GQA_HEADS, g_gqa, C, C),
            )
            ctx = _out_proj(mix_c, ctx, mod_c[2], sw, i, alpha, C)

        x = _ffn(x, mod_x[3], mod_x[4], mod_x[5], sw, i, alpha, tm_ffn, tf)
        if need_ctx:
            ctx = _ffn(ctx, mod_c[3], mod_c[4], mod_c[5], sw, i, alpha, C, tf)
    return x
```

```python
import functools
import math

import numpy as np
import jax
import jax.numpy as jnp
from jax import lax
from jax.experimental import pallas as pl
from jax.experimental.pallas import tpu as pltpu

GRID_W = 64
HEAD_DIM = 128
NA_HEADS = 4
NA_WIN_R = 8
NA_WIN_C = 16
SWA_HEADS = 4
SWA_KV_HEADS = 2
SWA_WINDOW = 128
MLA_HEADS = 4
MLA_Q_LORA = 384
MLA_KV_LORA = 128
MLA_NOPE = 128
MLA_ROPE = 64
MLA_V = 128
GQA_HEADS = 4
GQA_KV_HEADS = 2
CONV_W = 3
ROPE_THETA = 10000.0
EPS = 1e-6
NEG = -1e30
LOG2E = math.log2(math.e)

LANE = 128
BF16_SUBLANES = 16
VMEM_LIMIT = 56 * 1024 * 1024

F32 = jnp.float32
BF16 = jnp.bfloat16

OUT_SUB_ROWS = 256
NA_Q_ROWS = 4
NA_KEY_ROWS = NA_WIN_R + NA_Q_ROWS
NA_TILES_PER_STEP = 8
SWA_TILES_PER_STEP = 8


def _cparams(*sem):
    return pltpu.CompilerParams(dimension_semantics=sem, vmem_limit_bytes=VMEM_LIMIT)


def _dot(a, b):
    return jnp.dot(a, b, preferred_element_type=F32)


def _layer_norm(x):
    mu = jnp.mean(x, axis=-1, keepdims=True)
    xc = x - mu
    var = jnp.mean(xc * xc, axis=-1, keepdims=True)
    return xc * lax.rsqrt(var + EPS)


def _rms_norm(x, g):
    return x * lax.rsqrt(jnp.mean(x * x, axis=-1, keepdims=True) + EPS) * g


def _rope(x, cos, sin_lo, sin_hi, half):
    n = x.shape[-1]
    return x * cos + pltpu.roll(x, n - half, 1) * sin_lo + pltpu.roll(x, half, 1) * sin_hi


def _layer_spec(arr, layer, index_map_rest, block_rest, **kw):
    return pl.BlockSpec((None,) + tuple(block_rest), lambda *g: (layer,) + tuple(index_map_rest(*g)), **kw)


def _ada_kernel(c_ref, w_ref, b_ref, o_ref):
    c = c_ref[...]
    a = (c * jax.nn.sigmoid(c)).astype(BF16)
    o_ref[0] = _dot(a, w_ref[0].astype(BF16)) + b_ref[0]


def _ada(cvec, w_ada, b_ada):
    L, D, N = w_ada.shape
    M = cvec.shape[0]
    tn = 1024
    return pl.pallas_call(
        _ada_kernel,
        out_shape=jax.ShapeDtypeStruct((L, M, N), F32),
        grid=(L, N // tn),
        in_specs=[
            pl.BlockSpec((M, D), lambda l, j: (0, 0)),
            pl.BlockSpec((1, D, tn), lambda l, j: (l, 0, j)),
            pl.BlockSpec((1, 1, tn), lambda l, j: (l, 0, j)),
        ],
        out_specs=pl.BlockSpec((1, M, tn), lambda l, j: (l, 0, j)),
        compiler_params=_cparams("parallel", "parallel"),
        name="ada",
    )(cvec, w_ada, b_ada.reshape(L, 1, N))


_A0 = 0
_B0 = _A0 + 3 * NA_HEADS * HEAD_DIM
_C0 = _B0 + (SWA_HEADS + 2 * SWA_KV_HEADS) * HEAD_DIM
_C_W = MLA_Q_LORA + MLA_KV_LORA + LANE
_D0 = _C0 + _C_W
_D_W = (GQA_HEADS + 2 * GQA_KV_HEADS) * HEAD_DIM
IN_COLS_PAD = _D0 + _D_W
MLA_QK = 2 * LANE


def _proj_kernel(x_ref, shift_ref, scale_ref, w_ref, rope_ref, gq_lora_ref, gkv_lora_ref, gq_ref, gk_ref,
                 wuq_ref, wukv_ref,
                 qa_ref, ka_ref, va_ref, qb_ref, kb_ref, vb_ref, qc_ref, kc_ref, vc_ref, qd_ref, kd_ref, vd_ref):
    h = (_layer_norm(x_ref[0]) * (1.0 + scale_ref[0]) + shift_ref[0]).astype(BF16)
    cos2, slo2, shi2 = rope_ref[0], rope_ref[1], rope_ref[2]
    cos1, slo1, shi1 = rope_ref[3], rope_ref[4], rope_ref[5]
    rope2 = lambda t: _rope(t, cos2, slo2, shi2, HEAD_DIM // 4)
    rope1 = lambda t: _rope(t, cos1, slo1, shi1, MLA_ROPE // 4)
    hd = HEAD_DIM
    sc = hd ** -0.5 * LOG2E

    def store_t(ref, lo, t):
        ref[0, lo:lo + t.shape[1], :] = t.T.astype(BF16)


    wq, wk = GQA_HEADS * hd, GQA_KV_HEADS * hd
    pd = _dot(h, w_ref[:, _D0:_D0 + wq + 2 * wk])
    for i in range(GQA_HEADS):
        store_t(qd_ref, i * hd, rope2(_rms_norm(pd[:, i * hd:(i + 1) * hd], gq_ref[...])) * sc)
    for i in range(GQA_KV_HEADS):
        t = _rms_norm(pd[:, wq + i * hd:wq + (i + 1) * hd], gk_ref[...])
        kd_ref[0, :, i * hd:(i + 1) * hd] = rope2(t).astype(BF16)
    store_t(vd_ref, 0, pd[:, wq + wk:])

    pc = _dot(h, w_ref[:, _C0:_C0 + _C_W])
    cq = _rms_norm(pc[:, :MLA_Q_LORA], gq_lora_ref[...]).astype(BF16)
    ckv = _rms_norm(pc[:, MLA_Q_LORA:MLA_Q_LORA + MLA_KV_LORA], gkv_lora_ref[...]).astype(BF16)
    kpe = rope1(pc[:, MLA_Q_LORA + MLA_KV_LORA:]).astype(BF16)
    qup = _dot(cq, wuq_ref[...])
    kvup = _dot(ckv, wukv_ref[...])
    sc_mla = (MLA_NOPE + MLA_ROPE) ** -0.5 * LOG2E
    for i in range(MLA_HEADS):
        o = i * MLA_QK
        store_t(qc_ref, o, qup[:, o:o + LANE] * sc_mla)
        store_t(qc_ref, o + LANE, rope1(qup[:, o + LANE:o + 2 * LANE]) * sc_mla)
        kc_ref[0, :, o:o + LANE] = kvup[:, o:o + LANE].astype(BF16)
        kc_ref[0, :, o + LANE:o + 2 * LANE] = kpe
        store_t(vc_ref, i * MLA_V, kvup[:, o + LANE:o + 2 * LANE])

    wq, wk = SWA_HEADS * hd, SWA_KV_HEADS * hd
    pb = _dot(h, w_ref[:, _B0:_B0 + wq + 2 * wk])
    for i in range(SWA_HEADS):
        store_t(qb_ref, i * hd, rope2(pb[:, i * hd:(i + 1) * hd]) * sc)
    for i in range(SWA_KV_HEADS):
        kb_ref[0, :, i * hd:(i + 1) * hd] = rope2(pb[:, wq + i * hd:wq + (i + 1) * hd]).astype(BF16)
    store_t(vb_ref, 0, pb[:, wq + wk:])

    w = NA_HEADS * hd
    pa = _dot(h, w_ref[:, _A0:_A0 + 3 * w])
    store_t(qa_ref, 0, pa[:, :w] * sc)
    store_t(va_ref, 0, pa[:, 2 * w:])
    ka_ref[0] = pa[:, w:2 * w].astype(BF16)


def _proj(x, shift, scale, rope_tab, sw, layer, tm):
    B, S, D = x.shape
    widths = (NA_HEADS * HEAD_DIM,) * 3 + (SWA_HEADS * HEAD_DIM, SWA_KV_HEADS * HEAD_DIM, SWA_KV_HEADS * HEAD_DIM) \
        + (MLA_HEADS * MLA_QK, MLA_HEADS * MLA_QK, MLA_HEADS * MLA_V) \
        + (GQA_HEADS * HEAD_DIM, GQA_KV_HEADS * HEAD_DIM, GQA_KV_HEADS * HEAD_DIM)
    transposed = (True, False, True) * 4
    const = lambda arr: _layer_spec(arr, layer, lambda b, i: (0, 0), arr.shape[1:], pipeline_mode=pl.Buffered(1))
    vec = pl.BlockSpec((1, 1, D), lambda b, i: (b, 0, 0))
    out_spec = lambda n, t: (pl.BlockSpec((1, n, tm), lambda b, i: (b, 0, i)) if t
                             else pl.BlockSpec((1, tm, n), lambda b, i: (b, i, 0)))
    params = (sw["w_in"], None, sw["mla_q_norm"], sw["mla_kv_norm"], sw["gqa_q_norm"], sw["gqa_k_norm"],
              sw["w_uq"], sw["w_ukv"])
    in_specs = [pl.BlockSpec((1, tm, D), lambda b, i: (b, i, 0)), vec, vec]
    in_specs += [pl.BlockSpec((6, tm, LANE), lambda b, i: (0, i, 0)) if p is None else const(p) for p in params]
    args = [rope_tab if p is None else p for p in params]
    return pl.pallas_call(
        _proj_kernel,
        out_shape=[jax.ShapeDtypeStruct((B, n, S) if t else (B, S, n), BF16) for n, t in zip(widths, transposed)],
        grid=(B, S // tm),
        in_specs=in_specs,
        out_specs=[out_spec(n, t) for n, t in zip(widths, transposed)],
        compiler_params=_cparams("parallel", "parallel"),
        name="proj",
    )(x, shift, scale, *args)


def _softmax_pv(parts, sink=None):
    m = parts[0][0].max(axis=0, keepdims=True)
    for s, _ in parts[1:]:
        m = jnp.maximum(m, s.max(axis=0, keepdims=True))
    l = 0.0
    if sink is not None:
        m = jnp.maximum(m, sink)
        l = jnp.exp2(sink - m)
    acc = None
    for s, vt in parts:
        p = jnp.exp2(s - m)
        l = l + p.sum(axis=0, keepdims=True)
        pv = _dot(vt, p.astype(BF16))
        acc = pv if acc is None else acc + pv
    return acc / l


def _pipelined_tiles(n_tiles, scores, finish):
    nxt = scores(0)
    for t in range(n_tiles):
        cur = nxt
        if t + 1 < n_tiles:
            nxt = scores(t + 1)
        finish(t, cur)


def _na_kernel(cls_ref, ws_ref, qt_ref, k_ref, vt_ref, kc_ref, vct_ref, *rest):
    del cls_ref
    bias_refs, o_ref = rest[:-1], rest[-1]
    nk, tq = NA_KEY_ROWS * GRID_W, NA_Q_ROWS * GRID_W
    start = lambda t: pl.multiple_of(ws_ref[pl.program_id(2) * len(bias_refs) + t] * GRID_W, tq)

    def scores(t):
        qt = qt_ref[0, :, t * tq:(t + 1) * tq]
        return _dot(k_ref[0, pl.ds(start(t), nk), :], qt), _dot(kc_ref[0], qt)

    def finish(t, s):
        o = _softmax_pv([(s[0] + bias_refs[t][0, 0], vt_ref[0, :, pl.ds(start(t), nk)]), (s[1], vct_ref[0])])
        o_ref[0, t * tq:(t + 1) * tq, :] = o.T.astype(o_ref.dtype)

    _pipelined_tiles(len(bias_refs), scores, finish)


def _na_tables(rpb, rows):
    nt = rows // NA_Q_ROWS
    R = np.arange(nt) * NA_Q_ROWS
    ws = np.clip(R - NA_WIN_R // 2, 0, rows - NA_KEY_ROWS)
    qr = (R[:, None] + np.arange(NA_Q_ROWS)[None, :])
    kr = ws[:, None] + np.arange(NA_KEY_ROWS)[None, :]
    r0 = np.clip(qr - NA_WIN_R // 2, 0, rows - NA_WIN_R)
    row_ok = (kr[:, None, :] >= r0[:, :, None]) & (kr[:, None, :] < r0[:, :, None] + NA_WIN_R)
    drow = np.clip(kr[:, None, :] - qr[:, :, None] + NA_WIN_R - 1, 0, 2 * NA_WIN_R - 2)
    geom = np.concatenate([row_ok.reshape(nt, -1).astype(np.int64), drow.reshape(nt, -1)], axis=1)
    _, first, cls = np.unique(geom, axis=0, return_index=True, return_inverse=True)
    cq = np.arange(GRID_W)
    c0 = np.clip(cq - NA_WIN_C // 2, 0, GRID_W - NA_WIN_C)
    col_ok = (cq[None, :] >= c0[:, None]) & (cq[None, :] < c0[:, None] + NA_WIN_C)
    dcol = np.clip(cq[None, :] - cq[:, None] + NA_WIN_C - 1, 0, 2 * NA_WIN_C - 2)
    nc, H = len(first), rpb.shape[0]
    cols = jnp.zeros(rpb.shape[:2] + dcol.shape, F32)
    for j in range(rpb.shape[2]):
        cols = jnp.where(jnp.asarray(dcol == j)[None, None], rpb[:, :, j][:, :, None, None] * LOG2E, cols)
    cols = jnp.where(jnp.asarray(col_ok)[None, None], cols, NEG)
    cols_t = cols.transpose(0, 1, 3, 2)[None, :, :, None, :, None, :]
    row_sel = row_ok[first].transpose(0, 2, 1)
    drow_sel = drow[first].transpose(0, 2, 1)
    bias = jnp.full((nc, H, NA_KEY_ROWS, GRID_W, NA_Q_ROWS, GRID_W), NEG, F32)
    for d in range(rpb.shape[1]):
        hit = jnp.asarray(row_sel & (drow_sel == d))[:, None, :, None, :, None]
        bias = jnp.where(hit, cols_t[:, :, d], bias)
    bias = bias.reshape(nc, H, NA_KEY_ROWS * GRID_W, NA_Q_ROWS * GRID_W)
    return bias, jnp.asarray(cls.reshape(-1), jnp.int32), jnp.asarray(ws, jnp.int32)


def _na_attention(qt, k, vt, kc, vct, rpb):
    B, _, S = qt.shape
    C = kc.shape[1]
    rows = S // GRID_W
    assert rows >= NA_KEY_ROWS and rows % NA_Q_ROWS == 0
    bias, cls, ws = _na_tables(rpb, rows)
    tq, nk, hd = NA_Q_ROWS * GRID_W, NA_KEY_ROWS * GRID_W, HEAD_DIM
    n_sub = NA_TILES_PER_STEP if (S // tq) % NA_TILES_PER_STEP == 0 else 1
    bias_spec = lambda t: pl.BlockSpec((1, 1, nk, tq), lambda b, h, i, c, w: (c[i * n_sub + t], h, 0, 0))
    grid_spec = pltpu.PrefetchScalarGridSpec(
        num_scalar_prefetch=2,
        grid=(B, NA_HEADS, S // (tq * n_sub)),
        in_specs=[
            pl.BlockSpec((1, hd, n_sub * tq), lambda b, h, i, c, w: (b, h, i)),
            pl.BlockSpec((1, S, hd), lambda b, h, i, c, w: (b, 0, h)),
            pl.BlockSpec((1, hd, S), lambda b, h, i, c, w: (b, h, 0)),
            pl.BlockSpec((1, C, hd), lambda b, h, i, c, w: (b, 0, h)),
            pl.BlockSpec((1, hd, C), lambda b, h, i, c, w: (b, h, 0)),
        ] + [bias_spec(t) for t in range(n_sub)],
        out_specs=pl.BlockSpec((1, n_sub * tq, hd), lambda b, h, i, c, w: (b, i, h)),
    )
    return pl.pallas_call(
        _na_kernel,
        out_shape=jax.ShapeDtypeStruct((B, S, NA_HEADS * hd), BF16),
        grid_spec=grid_spec,
        compiler_params=_cparams("parallel", "parallel", "arbitrary"),
        name="na_attn",
    )(cls, ws, qt, k, vt, kc, vct, *([bias] * n_sub))


def _swa_kernel(qt_ref, k_ref, vt_ref, kc_ref, vct_ref, sink_ref, o_ref, *, tq, n_sub, seq):
    nk = tq + 2 * SWA_WINDOW
    tile = lambda t: pl.program_id(2) * n_sub + t
    start = lambda t: pl.multiple_of(jnp.clip(tile(t) * tq - SWA_WINDOW, 0, seq - nk), SWA_WINDOW)

    def scores(t):
        qt = qt_ref[0, :, t * tq:(t + 1) * tq]
        return _dot(k_ref[0, pl.ds(start(t), nk), :], qt), _dot(kc_ref[0], qt)

    def finish(t, s):
        kpos = start(t) + lax.broadcasted_iota(jnp.int32, (nk, tq), 0)
        qpos = tile(t) * tq + lax.broadcasted_iota(jnp.int32, (nk, tq), 1)
        s_win = jnp.where(jnp.abs(qpos - kpos) <= SWA_WINDOW, s[0], NEG)
        o = _softmax_pv([(s_win, vt_ref[0, :, pl.ds(start(t), nk)]), (s[1], vct_ref[0])], sink=sink_ref[0])
        o_ref[0, t * tq:(t + 1) * tq, :] = o.T.astype(o_ref.dtype)

    _pipelined_tiles(n_sub, scores, finish)


def _swa_attention(qt, k, vt, kc, vct, sink, tq):
    B, _, S = qt.shape
    C = kc.shape[1]
    hd, G = HEAD_DIM, SWA_HEADS // SWA_KV_HEADS
    assert S >= tq + 2 * SWA_WINDOW and S % tq == 0 and tq % SWA_WINDOW == 0
    n_sub = SWA_TILES_PER_STEP if (S // tq) % SWA_TILES_PER_STEP == 0 else 1
    return pl.pallas_call(
        functools.partial(_swa_kernel, tq=tq, n_sub=n_sub, seq=S),
        out_shape=jax.ShapeDtypeStruct((B, S, SWA_HEADS * hd), BF16),
        grid=(B, SWA_HEADS, S // (tq * n_sub)),
        in_specs=[
            pl.BlockSpec((1, hd, n_sub * tq), lambda b, h, i: (b, h, i)),
            pl.BlockSpec((1, S, hd), lambda b, h, i: (b, 0, h // G)),
            pl.BlockSpec((1, hd, S), lambda b, h, i: (b, h // G, 0)),
            pl.BlockSpec((1, C, hd), lambda b, h, i: (b, 0, h // G)),
            pl.BlockSpec((1, hd, C), lambda b, h, i: (b, h // G, 0)),
            pl.BlockSpec((1, 1, 1), lambda b, h, i: (h, 0, 0)),
        ],
        out_specs=pl.BlockSpec((1, n_sub * tq, hd), lambda b, h, i: (b, i, h)),
        compiler_params=_cparams("parallel", "parallel", "arbitrary"),
        name="swa_attn",
    )(qt, k, vt, kc, vct, (sink * LOG2E).reshape(SWA_HEADS, 1, 1).astype(F32))


def _dense_kernel(*refs, tk, seq, n_ctx, has_sink):
    refs = list(refs)
    qt_ref, k_ref, vt_ref = refs[:3]
    o_ref = refs[-1]
    kc_ref, vct_ref = refs[3:5] if n_ctx else (None, None)
    qt = qt_ref[0]
    tq = qt.shape[1]
    if has_sink:
        m = jnp.broadcast_to(refs[-2][0], (1, tq))
        l = jnp.ones((1, tq), F32)
    else:
        m = jnp.full((1, tq), -jnp.inf, F32)
        l = jnp.zeros((1, tq), F32)
    acc = jnp.zeros((vt_ref.shape[1], tq), F32)

    chunks = [(k_ref, vt_ref, j * tk, tk) for j in range(seq // tk)]
    if n_ctx:
        chunks.append((kc_ref, vct_ref, 0, n_ctx))
    scores = lambda c: _dot(c[0][0, c[2]:c[2] + c[3], :], qt)

    s_next = scores(chunks[0])
    for j, c in enumerate(chunks):
        s = s_next
        if j + 1 < len(chunks):
            s_next = scores(chunks[j + 1])
        m_new = jnp.maximum(m, s.max(axis=0, keepdims=True))
        alpha = jnp.exp2(m - m_new)
        p = jnp.exp2(s - m_new)
        l = alpha * l + p.sum(axis=0, keepdims=True)
        acc = alpha * acc + _dot(c[1][0, :, c[2]:c[2] + c[3]], p.astype(BF16))
        m = m_new
    o_ref[0] = (acc / l).T.astype(o_ref.dtype)


def _dense_attention(qt, k, vt, kc, vct, n_heads, group, tq, tk, sink=None):
    B, _, S = qt.shape
    Sk = k.shape[1]
    dk = qt.shape[1] // n_heads
    dv = vt.shape[1] // (n_heads // group)
    assert S % tq == 0 and Sk % tk == 0
    n_ctx = 0 if kc is None else kc.shape[1]
    in_specs = [
        pl.BlockSpec((1, dk, tq), lambda b, h, i: (b, h, i)),
        pl.BlockSpec((1, Sk, dk), lambda b, h, i: (b, 0, h // group)),
        pl.BlockSpec((1, dv, Sk), lambda b, h, i: (b, h // group, 0)),
    ]
    args = [qt, k, vt]
    if n_ctx:
        in_specs += [
            pl.BlockSpec((1, n_ctx, dk), lambda b, h, i: (b, 0, h // group)),
            pl.BlockSpec((1, dv, n_ctx), lambda b, h, i: (b, h // group, 0)),
        ]
        args += [kc, vct]
    if sink is not None:
        in_specs.append(pl.BlockSpec((1, 1, 1), lambda b, h, i: (h, 0, 0)))
        args.append((sink * LOG2E).reshape(n_heads, 1, 1).astype(F32))
    return pl.pallas_call(
        functools.partial(_dense_kernel, tk=tk, seq=Sk, n_ctx=n_ctx, has_sink=sink is not None),
        out_shape=jax.ShapeDtypeStruct((B, S, n_heads * dv), BF16),
        grid=(B, n_heads, S // tq),
        in_specs=in_specs,
        out_specs=pl.BlockSpec((1, tq, dv), lambda b, h, i: (b, i, h)),
        compiler_params=_cparams("parallel", "parallel", "arbitrary"),
        name="dense_attn",
    )(*args)


def _out_kernel(oa_ref, ob_ref, oc_ref, od_ref, w_ref, x_ref, gate_ref, g_ref, b_ref, o_ref, *, alpha, sub):
    for r0 in range(0, x_ref.shape[1], sub):
        rows = slice(r0, r0 + sub)
        y = None
        o = 0
        for r in (oa_ref, ob_ref, oc_ref, od_ref):
            n = r.shape[2]
            t = _dot(r[0, rows, :], w_ref[o:o + n, :])
            y = t if y is None else y + t
            o += n
        z = alpha * x_ref[0, rows, :] + gate_ref[0] * y
        o_ref[0, rows, :] = _layer_norm(z) * g_ref[...] + b_ref[...]


def _out_proj(mix, x, gate, sw, layer, alpha, tm):
    B, S, D = x.shape
    const = lambda arr: _layer_spec(arr, layer, lambda bb, i: (0, 0), arr.shape[1:], pipeline_mode=pl.Buffered(1))
    return pl.pallas_call(
        functools.partial(_out_kernel, alpha=alpha, sub=min(OUT_SUB_ROWS, tm)),
        out_shape=jax.ShapeDtypeStruct((B, S, D), F32),
        grid=(B, S // tm),
        in_specs=[pl.BlockSpec((1, tm, m.shape[2]), lambda bb, i: (bb, i, 0)) for m in mix] + [
            const(sw["w_out"]),
            pl.BlockSpec((1, tm, D), lambda bb, i: (bb, i, 0)),
            pl.BlockSpec((1, 1, D), lambda bb, i: (bb, 0, 0)),
            const(sw["ln1_g"]), const(sw["ln1_b"]),
        ],
        out_specs=pl.BlockSpec((1, tm, D), lambda bb, i: (bb, i, 0)),
        compiler_params=_cparams("parallel", "parallel"),
        name="out_proj",
    )(*mix, sw["w_out"], x, gate, sw["ln1_g"], sw["ln1_b"])


HALO = BF16_SUBLANES


def _ffn_kernel(x_ref, xp_ref, xn_ref, shift_ref, scale_ref, gate_ref, wg_ref, wu_ref, cw_ref, cb_ref, wd_ref,
                g_ref, b_ref, o_ref, h_ref, acc_ref, *, alpha, tm):
    i, f = pl.program_id(1), pl.program_id(2)

    @pl.when(f == 0)
    def _():
        mod = lambda t: _layer_norm(t) * (1.0 + scale_ref[0]) + shift_ref[0]
        keep_p = (i > 0).astype(F32)
        keep_n = (i < pl.num_programs(1) - 1).astype(F32)
        h_ref[0:HALO, :] = (mod(xp_ref[0]) * keep_p).astype(BF16)
        h_ref[HALO:HALO + tm, :] = mod(x_ref[0]).astype(BF16)
        h_ref[HALO + tm:, :] = (mod(xn_ref[0]) * keep_n).astype(BF16)
        acc_ref[...] = jnp.zeros(acc_ref.shape, F32)

    n = tm + 2 * HALO
    gt = _dot(h_ref[...], wg_ref[...])
    up = _dot(h_ref[HALO:HALO + tm, :], wu_ref[...])
    a = (pltpu.roll(gt, 1, 0) * cw_ref[0:1, :] + gt * cw_ref[1:2, :]
         + pltpu.roll(gt, n - 1, 0) * cw_ref[2:3, :] + cb_ref[...])[HALO:HALO + tm]
    y = (a * jax.nn.sigmoid(a) * up).astype(BF16)
    acc_ref[...] += _dot(y, wd_ref[...])

    @pl.when(f == pl.num_programs(2) - 1)
    def _():
        z = alpha * x_ref[0] + gate_ref[0] * acc_ref[...]
        o_ref[0] = _layer_norm(z) * g_ref[...] + b_ref[...]


def _ffn(x, shift, scale, gate, sw, layer, alpha, tm, tf):
    B, S, D = x.shape
    F = sw["ffn_w_gate"].shape[2]
    assert S % tm == 0 and F % tf == 0 and tm % HALO == 0
    nh = tm // HALO
    vec = pl.BlockSpec((1, 1, D), lambda bb, i, f: (bb, 0, 0))
    cols = lambda arr, rows: _layer_spec(arr, layer, lambda bb, i, f: (0, f), (rows, tf))
    const = lambda arr: _layer_spec(arr, layer, lambda bb, i, f: (0, 0), arr.shape[1:])
    return pl.pallas_call(
        functools.partial(_ffn_kernel, alpha=alpha, tm=tm),
        out_shape=jax.ShapeDtypeStruct((B, S, D), F32),
        grid=(B, S // tm, F // tf),
        in_specs=[
            pl.BlockSpec((1, tm, D), lambda bb, i, f: (bb, i, 0)),
            pl.BlockSpec((1, HALO, D), lambda bb, i, f: (bb, jnp.maximum(i * nh - 1, 0), 0)),
            pl.BlockSpec((1, HALO, D), lambda bb, i, f: (bb, jnp.minimum((i + 1) * nh, S // HALO - 1), 0)),
            vec, vec, vec,
            cols(sw["ffn_w_gate"], D), cols(sw["ffn_w_up"], D), cols(sw["ffn_conv_w"], CONV_W),
            cols(sw["ffn_conv_b"], 1),
            _layer_spec(sw["ffn_w_down"], layer, lambda bb, i, f: (f, 0), (tf, D)),
            const(sw["ln2_g"]), const(sw["ln2_b"]),
        ],
        out_specs=pl.BlockSpec((1, tm, D), lambda bb, i, f: (bb, i, 0)),
        scratch_shapes=[pltpu.VMEM((tm + 2 * HALO, D), BF16), pltpu.VMEM((tm, D), F32)],
        compiler_params=_cparams("parallel", "parallel", "arbitrary"),
        name="conv_ffn",
    )(x, x, x, shift, scale, gate, sw["ffn_w_gate"], sw["ffn_w_up"], sw["ffn_conv_w"], sw["ffn_conv_b"],
      sw["ffn_w_down"], sw["ln2_g"], sw["ln2_b"])


def _rope_tables(S):
    t = jnp.arange(S)
    row, col = (t // GRID_W).astype(F32), (t % GRID_W).astype(F32)
    lane = np.arange(LANE)

    def tables(dim):
        half = dim // 4
        inv_freq = ROPE_THETA ** (-jnp.arange(half, dtype=F32) / half)
        used = lane < dim
        pos = jnp.where(jnp.asarray((lane % dim) < dim // 2)[None, :], row[:, None], col[:, None])
        ang = pos * inv_freq[jnp.asarray(lane % half)][None, :]
        cos, sin = jnp.cos(ang), jnp.sin(ang)
        lo = jnp.asarray(used & ((lane % (2 * half)) < half))[None, :]
        hi = jnp.asarray(used & ((lane % (2 * half)) >= half))[None, :]
        return [jnp.where(jnp.asarray(used)[None, :], cos, 1.0), jnp.where(lo, -sin, 0.0), jnp.where(hi, sin, 0.0)]

    return jnp.stack(tables(HEAD_DIM) + tables(MLA_ROPE)).astype(F32)


def _identity_rope(C):
    one, zero = jnp.ones((C, LANE), F32), jnp.zeros((C, LANE), F32)
    return jnp.stack([one, zero, zero, one, zero, zero])


def _pad_w_in(w):
    cut = _C0 + MLA_Q_LORA + MLA_KV_LORA + MLA_ROPE
    pad = jnp.zeros(w.shape[:2] + (LANE - MLA_ROPE,), w.dtype)
    return jnp.concatenate([w[..., :cut], pad, w[..., cut:]], axis=-1).astype(BF16)


def _pad_w_uq(w):
    L, r, _ = w.shape
    w = w.reshape(L, r, MLA_HEADS, MLA_NOPE + MLA_ROPE)
    w = jnp.concatenate([w, jnp.zeros((L, r, MLA_HEADS, MLA_QK - MLA_NOPE - MLA_ROPE), w.dtype)], axis=3)
    return w.reshape(L, r, MLA_HEADS * MLA_QK).astype(BF16)


def kernel(x, c, ctx, c_ctx, w_ada, b_ada, w_in, na_rpb, swa_sink, mla_q_norm, mla_kv_norm, mla_w_uq, mla_w_ukv,
           gqa_q_norm, gqa_k_norm, w_out, ln1_g, ln1_b, ffn_w_gate, ffn_w_up, ffn_conv_w, ffn_conv_b, ffn_w_down,
           ln2_g, ln2_b):
    B, S, D = x.shape
    C = ctx.shape[1]
    depth = w_ada.shape[0]
    alpha = (2 * depth) ** 0.25

    cvec = jnp.concatenate([c, c_ctx[None, :], jnp.zeros((8 - B - 1, D), F32)], axis=0)
    ada = _ada(cvec, w_ada, b_ada)
    rope_x, rope_c = _rope_tables(S), _identity_rope(C)

    row = lambda a: a[:, None, :]
    sw = {
        "w_in": _pad_w_in(w_in), "w_uq": _pad_w_uq(mla_w_uq), "w_ukv": mla_w_ukv.astype(BF16),
        "mla_q_norm": row(mla_q_norm), "mla_kv_norm": row(mla_kv_norm),
        "gqa_q_norm": row(gqa_q_norm), "gqa_k_norm": row(gqa_k_norm),
        "w_out": w_out.astype(BF16), "ln1_g": row(ln1_g), "ln1_b": row(ln1_b),
        "ffn_w_gate": ffn_w_gate.astype(BF16), "ffn_w_up": ffn_w_up.astype(BF16), "ffn_conv_w": ffn_conv_w,
        "ffn_conv_b": row(ffn_conv_b), "ffn_w_down": ffn_w_down.astype(BF16), "ln2_g": row(ln2_g), "ln2_b": row(ln2_b),
    }

    tm_proj = min(256, S)
    tm_out = min(512, S)
    tm_ffn, tf = min(512, S), 512
    tq_dense, tk_dense = min(1024, S), min(512, S)
    tq_swa = min(256, S - 2 * SWA_WINDOW)
    g_swa, g_gqa = SWA_HEADS // SWA_KV_HEADS, GQA_HEADS // GQA_KV_HEADS

    for i in range(depth):
        need_ctx = i < depth - 1
        mx = ada[i, :B].reshape(B, 1, 6, D)
        mod_x = [mx[:, :, j] for j in range(6)]
        mod_c = [jnp.broadcast_to(ada[i, B].reshape(1, 1, 6, D)[:, :, j], (B, 1, D)) for j in range(6)]

        qa, ka, va, qb, kb, vb, qc, kc, vc, qd, kd, vd = _proj(x, mod_x[0], mod_x[1], rope_x, sw, i, tm_proj)
        qa_c, ka_c, va_c, qb_c, kb_c, vb_c, qc_c, kc_c, vc_c, qd_c, kd_c, vd_c = _proj(
            ctx, mod_c[0], mod_c[1], rope_c, sw, i, C)

        mix_x = (
            _na_attention(qa, ka, va, ka_c, va_c, na_rpb[i]),
            _swa_attention(qb, kb, vb, kb_c, vb_c, swa_sink[i], tq_swa),
            _dense_attention(qc, kc, vc, kc_c, vc_c, MLA_HEADS, 1, tq_dense, tk_dense),
            _dense_attention(qd, kd, vd, kd_c, vd_c, GQA_HEADS, g_gqa, tq_dense, tk_dense),
        )
        x = _out_proj(mix_x, x, mod_x[2], sw, i, alpha, tm_out)
        if need_ctx:
            mix_c = (
                _dense_attention(qa_c, ka_c, va_c, None, None, NA_HEADS, 1, C, C),
                _dense_attention(qb_c, kb_c, vb_c, None, None, SWA_HEADS, g_swa, C, C, sink=swa_sink[i]),
                _dense_attention(qc_c, kc_c, vc_c, None, None, MLA_HEADS, 1, C, C),
                _dense_attention(qd_c, kd_c, vd_c, None, None, GQA_HEADS, g_gqa, C, C),
            )
            ctx = _out_proj(mix_c, ctx, mod_c[2], sw, i, alpha, C)

        x = _ffn(x, mod_x[3], mod_x[4], mod_x[5], sw, i, alpha, tm_ffn, tf)
        if need_ctx:
            ctx = _ffn(ctx, mod_c[3], mod_c[4], mod_c[5], sw, i, alpha, C, tf)
    return x
```

```python
import functools
import math

import numpy as np
import jax
import jax.numpy as jnp
from jax import lax
from jax.experimental import pallas as pl
from jax.experimental.pallas import tpu as pltpu

GRID_W = 64
HEAD_DIM = 128
NA_HEADS = 4
NA_WIN_R = 8
NA_WIN_C = 16
SWA_HEADS = 4
SWA_KV_HEADS = 2
SWA_WINDOW = 128
MLA_HEADS = 4
MLA_Q_LORA = 384
MLA_KV_LORA = 128
MLA_NOPE = 128
MLA_ROPE = 64
MLA_V = 128
GQA_HEADS = 4
GQA_KV_HEADS = 2
CONV_W = 3
ROPE_THETA = 10000.0
EPS = 1e-6
NEG = -1e30
LOG2E = math.log2(math.e)

LANE = 128
BF16_SUBLANES = 16
VMEM_LIMIT = 56 * 1024 * 1024

F32 = jnp.float32
BF16 = jnp.bfloat16

OUT_SUB_ROWS = 256
NA_Q_ROWS = 4
NA_KEY_ROWS = NA_WIN_R + NA_Q_ROWS
NA_TILES_PER_STEP = 8
SWA_TILES_PER_STEP = 8


def _cparams(*sem):
    return pltpu.CompilerParams(dimension_semantics=sem, vmem_limit_bytes=VMEM_LIMIT)


def _dot(a, b):
    return jnp.dot(a, b, preferred_element_type=F32)


def _layer_norm(x):
    mu = jnp.mean(x, axis=-1, keepdims=True)
    xc = x - mu
    var = jnp.mean(xc * xc, axis=-1, keepdims=True)
    return xc * lax.rsqrt(var + EPS)


def _rms_norm(x, g):
    return x * lax.rsqrt(jnp.mean(x * x, axis=-1, keepdims=True) + EPS) * g


def _rope(x, cos, sin_lo, sin_hi, half):
    n = x.shape[-1]
    return x * cos + pltpu.roll(x, n - half, 1) * sin_lo + pltpu.roll(x, half, 1) * sin_hi


def _layer_spec(arr, layer, index_map_rest, block_rest, **kw):
    return pl.BlockSpec((None,) + tuple(block_rest), lambda *g: (layer,) + tuple(index_map_rest(*g)), **kw)


def _ada_kernel(c_ref, w_ref, b_ref, o_ref):
    c = c_ref[...]
    a = (c * jax.nn.sigmoid(c)).astype(BF16)
    o_ref[0] = _dot(a, w_ref[0].astype(BF16)) + b_ref[0]


def _ada(cvec, w_ada, b_ada):
    L, D, N = w_ada.shape
    M = cvec.shape[0]
    tn = 1024
    return pl.pallas_call(
        _ada_kernel,
        out_shape=jax.ShapeDtypeStruct((L, M, N), F32),
        grid=(L, N // tn),
        in_specs=[
            pl.BlockSpec((M, D), lambda l, j: (0, 0)),
            pl.BlockSpec((1, D, tn), lambda l, j: (l, 0, j)),
            pl.BlockSpec((1, 1, tn), lambda l, j: (l, 0, j)),
        ],
        out_specs=pl.BlockSpec((1, M, tn), lambda l, j: (l, 0, j)),
        compiler_params=_cparams("parallel", "parallel"),
        name="ada",
    )(cvec, w_ada, b_ada.reshape(L, 1, N))


_A0 = 0
_B0 = _A0 + 3 * NA_HEADS * HEAD_DIM
_C0 = _B0 + (SWA_HEADS + 2 * SWA_KV_HEADS) * HEAD_DIM
_C_W = MLA_Q_LORA + MLA_KV_LORA + LANE
_D0 = _C0 + _C_W
_D_W = (GQA_HEADS + 2 * GQA_KV_HEADS) * HEAD_DIM
IN_COLS_PAD = _D0 + _D_W
MLA_QK = 2 * LANE


def _proj_kernel(x_ref, shift_ref, scale_ref, w_ref, rope_ref, gq_lora_ref, gkv_lora_ref, gq_ref, gk_ref,
                 wuq_ref, wukv_ref,
                 qa_ref, ka_ref, va_ref, qb_ref, kb_ref, vb_ref, qc_ref, kc_ref, vc_ref, qd_ref, kd_ref, vd_ref):
    h = (_layer_norm(x_ref[0]) * (1.0 + scale_ref[0]) + shift_ref[0]).astype(BF16)
    cos2, slo2, shi2 = rope_ref[0], rope_ref[1], rope_ref[2]
    cos1, slo1, shi1 = rope_ref[3], rope_ref[4], rope_ref[5]
    rope2 = lambda t: _rope(t, cos2, slo2, shi2, HEAD_DIM // 4)
    rope1 = lambda t: _rope(t, cos1, slo1, shi1, MLA_ROPE // 4)
    hd = HEAD_DIM
    sc = hd ** -0.5 * LOG2E

    def store_t(ref, lo, t):
        ref[0, lo:lo + t.shape[1], :] = t.T.astype(BF16)


    wq, wk = GQA_HEADS * hd, GQA_KV_HEADS * hd
    pd = _dot(h, w_ref[:, _D0:_D0 + wq + 2 * wk])
    for i in range(GQA_HEADS):
        store_t(qd_ref, i * hd, rope2(_rms_norm(pd[:, i * hd:(i + 1) * hd], gq_ref[...])) * sc)
    for i in range(GQA_KV_HEADS):
        t = _rms_norm(pd[:, wq + i * hd:wq + (i + 1) * hd], gk_ref[...])
        kd_ref[0, :, i * hd:(i + 1) * hd] = rope2(t).astype(BF16)
    store_t(vd_ref, 0, pd[:, wq + wk:])

    pc = _dot(h, w_ref[:, _C0:_C0 + _C_W])
    cq = _rms_norm(pc[:, :MLA_Q_LORA], gq_lora_ref[...]).astype(BF16)
    ckv = _rms_norm(pc[:, MLA_Q_LORA:MLA_Q_LORA + MLA_KV_LORA], gkv_lora_ref[...]).astype(BF16)
    kpe = rope1(pc[:, MLA_Q_LORA + MLA_KV_LORA:]).astype(BF16)
    qup = _dot(cq, wuq_ref[...])
    kvup = _dot(ckv, wukv_ref[...])
    sc_mla = (MLA_NOPE + MLA_ROPE) ** -0.5 * LOG2E
    for i in range(MLA_HEADS):
        o = i * MLA_QK
        store_t(qc_ref, o, qup[:, o:o + LANE] * sc_mla)
        store_t(qc_ref, o + LANE, rope1(qup[:, o + LANE:o + 2 * LANE]) * sc_mla)
        kc_ref[0, :, o:o + LANE] = kvup[:, o:o + LANE].astype(BF16)
        kc_ref[0, :, o + LANE:o + 2 * LANE] = kpe
        store_t(vc_ref, i * MLA_V, kvup[:, o + LANE:o + 2 * LANE])

    wq, wk = SWA_HEADS * hd, SWA_KV_HEADS * hd
    pb = _dot(h, w_ref[:, _B0:_B0 + wq + 2 * wk])
    for i in range(SWA_HEADS):
        store_t(qb_ref, i * hd, rope2(pb[:, i * hd:(i + 1) * hd]) * sc)
    for i in range(SWA_KV_HEADS):
        kb_ref[0, :, i * hd:(i + 1) * hd] = rope2(pb[:, wq + i * hd:wq + (i + 1) * hd]).astype(BF16)
    store_t(vb_ref, 0, pb[:, wq + wk:])

    w = NA_HEADS * hd
    pa = _dot(h, w_ref[:, _A0:_A0 + 3 * w])
    store_t(qa_ref, 0, pa[:, :w] * sc)
    store_t(va_ref, 0, pa[:, 2 * w:])
    ka_ref[0] = pa[:, w:2 * w].astype(BF16)


def _proj(x, shift, scale, rope_tab, sw, layer, tm):
    B, S, D = x.shape
    widths = (NA_HEADS * HEAD_DIM,) * 3 + (SWA_HEADS * HEAD_DIM, SWA_KV_HEADS * HEAD_DIM, SWA_KV_HEADS * HEAD_DIM) \
        + (MLA_HEADS * MLA_QK, MLA_HEADS * MLA_QK, MLA_HEADS * MLA_V) \
        + (GQA_HEADS * HEAD_DIM, GQA_KV_HEADS * HEAD_DIM, GQA_KV_HEADS * HEAD_DIM)
    transposed = (True, False, True) * 4
    const = lambda arr: _layer_spec(arr, layer, lambda b, i: (0, 0), arr.shape[1:], pipeline_mode=pl.Buffered(1))
    vec = pl.BlockSpec((1, 1, D), lambda b, i: (b, 0, 0))
    out_spec = lambda n, t: (pl.BlockSpec((1, n, tm), lambda b, i: (b, 0, i)) if t
                             else pl.BlockSpec((1, tm, n), lambda b, i: (b, i, 0)))
    params = (sw["w_in"], None, sw["mla_q_norm"], sw["mla_kv_norm"], sw["gqa_q_norm"], sw["gqa_k_norm"],
              sw["w_uq"], sw["w_ukv"])
    in_specs = [pl.BlockSpec((1, tm, D), lambda b, i: (b, i, 0)), vec, vec]
    in_specs += [pl.BlockSpec((6, tm, LANE), lambda b, i: (0, i, 0)) if p is None else const(p) for p in params]
    args = [rope_tab if p is None else p for p in params]
    return pl.pallas_call(
        _proj_kernel,
        out_shape=[jax.ShapeDtypeStruct((B, n, S) if t else (B, S, n), BF16) for n, t in zip(widths, transposed)],
        grid=(B, S // tm),
        in_specs=in_specs,
        out_specs=[out_spec(n, t) for n, t in zip(widths, transposed)],
        compiler_params=_cparams("parallel", "parallel"),
        name="proj",
    )(x, shift, scale, *args)


def _with_ones_rows(vt):
    return jnp.concatenate([vt, jnp.ones((BF16_SUBLANES, vt.shape[1]), BF16)], axis=0)


def _softmax_pv(parts, sink=None):
    m = parts[0][0].max(axis=0, keepdims=True)
    for s, _ in parts[1:]:
        m = jnp.maximum(m, s.max(axis=0, keepdims=True))
    l = 0.0
    if sink is not None:
        m = jnp.maximum(m, sink)
        l = jnp.exp2(sink - m)
    acc = None
    for s, vt in parts:
        p = jnp.exp2(s - m)
        l = l + p.sum(axis=0, keepdims=True)
        pv = _dot(vt, p.astype(BF16))
        acc = pv if acc is None else acc + pv
    return acc / l


def _pipelined_tiles(n_tiles, scores, finish):
    nxt = scores(0)
    for t in range(n_tiles):
        cur = nxt
        if t + 1 < n_tiles:
            nxt = scores(t + 1)
        finish(t, cur)


def _na_kernel(cls_ref, ws_ref, qt_ref, k_ref, vt_ref, kc_ref, vct_ref, *rest):
    del cls_ref
    bias_refs, o_ref = rest[:-1], rest[-1]
    nk, tq = NA_KEY_ROWS * GRID_W, NA_Q_ROWS * GRID_W
    start = lambda t: pl.multiple_of(ws_ref[pl.program_id(2) * len(bias_refs) + t] * GRID_W, tq)

    def scores(t):
        qt = qt_ref[0, :, t * tq:(t + 1) * tq]
        return _dot(k_ref[0, pl.ds(start(t), nk), :], qt), _dot(kc_ref[0], qt)

    def finish(t, s):
        o = _softmax_pv([(s[0] + bias_refs[t][0, 0], vt_ref[0, :, pl.ds(start(t), nk)]), (s[1], vct_ref[0])])
        o_ref[0, t * tq:(t + 1) * tq, :] = o.T.astype(o_ref.dtype)

    _pipelined_tiles(len(bias_refs), scores, finish)


def _na_tables(rpb, rows):
    nt = rows // NA_Q_ROWS
    R = np.arange(nt) * NA_Q_ROWS
    ws = np.clip(R - NA_WIN_R // 2, 0, rows - NA_KEY_ROWS)
    qr = (R[:, None] + np.arange(NA_Q_ROWS)[None, :])
    kr = ws[:, None] + np.arange(NA_KEY_ROWS)[None, :]
    r0 = np.clip(qr - NA_WIN_R // 2, 0, rows - NA_WIN_R)
    row_ok = (kr[:, None, :] >= r0[:, :, None]) & (kr[:, None, :] < r0[:, :, None] + NA_WIN_R)
    drow = np.clip(kr[:, None, :] - qr[:, :, None] + NA_WIN_R - 1, 0, 2 * NA_WIN_R - 2)
    geom = np.concatenate([row_ok.reshape(nt, -1).astype(np.int64), drow.reshape(nt, -1)], axis=1)
    _, first, cls = np.unique(geom, axis=0, return_index=True, return_inverse=True)
    cq = np.arange(GRID_W)
    c0 = np.clip(cq - NA_WIN_C // 2, 0, GRID_W - NA_WIN_C)
    col_ok = (cq[None, :] >= c0[:, None]) & (cq[None, :] < c0[:, None] + NA_WIN_C)
    dcol = np.clip(cq[None, :] - cq[:, None] + NA_WIN_C - 1, 0, 2 * NA_WIN_C - 2)
    nc, H = len(first), rpb.shape[0]
    cols = jnp.zeros(rpb.shape[:2] + dcol.shape, F32)
    for j in range(rpb.shape[2]):
        cols = jnp.where(jnp.asarray(dcol == j)[None, None], rpb[:, :, j][:, :, None, None] * LOG2E, cols)
    cols = jnp.where(jnp.asarray(col_ok)[None, None], cols, NEG)
    nd = rpb.shape[1]
    pad = jnp.full((H, NA_KEY_ROWS * GRID_W, GRID_W), NEG, F32)
    flat = jnp.concatenate([pad, cols.transpose(0, 1, 3, 2).reshape(H, nd * GRID_W, GRID_W), pad], axis=1)
    per_class = []
    for c in first:
        strips = []
        for q in range(NA_Q_ROWS):
            d0 = int(kr[c, 0] - qr[c, q]) + NA_WIN_R - 1 + NA_KEY_ROWS
            strip = flat[:, d0 * GRID_W:(d0 + NA_KEY_ROWS) * GRID_W, :]
            keep = np.repeat(row_ok[c, q], GRID_W)
            strips.append(jnp.where(jnp.asarray(keep)[None, :, None], strip, NEG))
        per_class.append(jnp.concatenate(strips, axis=-1))
    bias = jnp.stack(per_class)
    return bias, jnp.asarray(cls.reshape(-1), jnp.int32), jnp.asarray(ws, jnp.int32)


def _na_attention(qt, k, vt, kc, vct, rpb):
    B, _, S = qt.shape
    C = kc.shape[1]
    rows = S // GRID_W
    assert rows >= NA_KEY_ROWS and rows % NA_Q_ROWS == 0
    bias, cls, ws = _na_tables(rpb, rows)
    tq, nk, hd = NA_Q_ROWS * GRID_W, NA_KEY_ROWS * GRID_W, HEAD_DIM
    n_sub = NA_TILES_PER_STEP if (S // tq) % NA_TILES_PER_STEP == 0 else 1
    bias_spec = lambda t: pl.BlockSpec((1, 1, nk, tq), lambda b, h, i, c, w: (c[i * n_sub + t], h, 0, 0))
    grid_spec = pltpu.PrefetchScalarGridSpec(
        num_scalar_prefetch=2,
        grid=(B, NA_HEADS, S // (tq * n_sub)),
        in_specs=[
            pl.BlockSpec((1, hd, n_sub * tq), lambda b, h, i, c, w: (b, h, i)),
            pl.BlockSpec((1, S, hd), lambda b, h, i, c, w: (b, 0, h)),
            pl.BlockSpec((1, hd, S), lambda b, h, i, c, w: (b, h, 0)),
            pl.BlockSpec((1, C, hd), lambda b, h, i, c, w: (b, 0, h)),
            pl.BlockSpec((1, hd, C), lambda b, h, i, c, w: (b, h, 0)),
        ] + [bias_spec(t) for t in range(n_sub)],
        out_specs=pl.BlockSpec((1, n_sub * tq, hd), lambda b, h, i, c, w: (b, i, h)),
    )
    return pl.pallas_call(
        _na_kernel,
        out_shape=jax.ShapeDtypeStruct((B, S, NA_HEADS * hd), BF16),
        grid_spec=grid_spec,
        compiler_params=_cparams("parallel", "parallel", "arbitrary"),
        name="na_attn",
    )(cls, ws, qt, k, vt, kc, vct, *([bias] * n_sub))


def _swa_kernel(qt_ref, k_ref, vt_ref, kc_ref, vct_ref, sink_ref, o_ref, *, tq, n_sub, seq):
    nk = tq + 2 * SWA_WINDOW
    tile = lambda t: pl.program_id(2) * n_sub + t
    start = lambda t: pl.multiple_of(jnp.clip(tile(t) * tq - SWA_WINDOW, 0, seq - nk), SWA_WINDOW)

    def scores(t):
        qt = qt_ref[0, :, t * tq:(t + 1) * tq]
        return _dot(k_ref[0, pl.ds(start(t), nk), :], qt), _dot(kc_ref[0], qt)

    def finish(t, s):
        kpos = start(t) + lax.broadcasted_iota(jnp.int32, (nk, tq), 0)
        qpos = tile(t) * tq + lax.broadcasted_iota(jnp.int32, (nk, tq), 1)
        s_win = jnp.where(jnp.abs(qpos - kpos) <= SWA_WINDOW, s[0], NEG)
        o = _softmax_pv([(s_win, vt_ref[0, :, pl.ds(start(t), nk)]), (s[1], vct_ref[0])], sink=sink_ref[0])
        o_ref[0, t * tq:(t + 1) * tq, :] = o.T.astype(o_ref.dtype)

    _pipelined_tiles(n_sub, scores, finish)


def _swa_attention(qt, k, vt, kc, vct, sink, tq):
    B, _, S = qt.shape
    C = kc.shape[1]
    hd, G = HEAD_DIM, SWA_HEADS // SWA_KV_HEADS
    assert S >= tq + 2 * SWA_WINDOW and S % tq == 0 and tq % SWA_WINDOW == 0
    n_sub = SWA_TILES_PER_STEP if (S // tq) % SWA_TILES_PER_STEP == 0 else 1
    return pl.pallas_call(
        functools.partial(_swa_kernel, tq=tq, n_sub=n_sub, seq=S),
        out_shape=jax.ShapeDtypeStruct((B, S, SWA_HEADS * hd), BF16),
        grid=(B, SWA_HEADS, S // (tq * n_sub)),
        in_specs=[
            pl.BlockSpec((1, hd, n_sub * tq), lambda b, h, i: (b, h, i)),
            pl.BlockSpec((1, S, hd), lambda b, h, i: (b, 0, h // G)),
            pl.BlockSpec((1, hd, S), lambda b, h, i: (b, h // G, 0)),
            pl.BlockSpec((1, C, hd), lambda b, h, i: (b, 0, h // G)),
            pl.BlockSpec((1, hd, C), lambda b, h, i: (b, h // G, 0)),
            pl.BlockSpec((1, 1, 1), lambda b, h, i: (h, 0, 0)),
        ],
        out_specs=pl.BlockSpec((1, n_sub * tq, hd), lambda b, h, i: (b, i, h)),
        compiler_params=_cparams("parallel", "parallel", "arbitrary"),
        name="swa_attn",
    )(qt, k, vt, kc, vct, (sink * LOG2E).reshape(SWA_HEADS, 1, 1).astype(F32))


def _dense_kernel(*refs, tk, seq, n_ctx, has_sink):
    refs = list(refs)
    qt_ref, k_ref, vt_ref = refs[:3]
    o_ref = refs[-1]
    kc_ref, vct_ref = refs[3:5] if n_ctx else (None, None)
    qt = qt_ref[0]
    tq = qt.shape[1]
    dv = vt_ref.shape[1]
    if has_sink:
        m = jnp.broadcast_to(refs[-2][0], (1, tq))
        acc = jnp.concatenate([jnp.zeros((dv, tq), F32), jnp.ones((BF16_SUBLANES, tq), F32)], axis=0)
    else:
        m = jnp.full((1, tq), -jnp.inf, F32)
        acc = jnp.zeros((dv + BF16_SUBLANES, tq), F32)

    chunks = [(k_ref, vt_ref, j * tk, tk) for j in range(seq // tk)]
    if n_ctx:
        chunks.append((kc_ref, vct_ref, 0, n_ctx))
    scores = lambda c: _dot(c[0][0, c[2]:c[2] + c[3], :], qt)

    s_next = scores(chunks[0])
    for j, c in enumerate(chunks):
        s = s_next
        if j + 1 < len(chunks):
            s_next = scores(chunks[j + 1])
        m_new = jnp.maximum(m, s.max(axis=0, keepdims=True))
        alpha = jnp.exp2(m - m_new)
        p = jnp.exp2(s - m_new).astype(BF16)
        acc = alpha * acc + _dot(_with_ones_rows(c[1][0, :, c[2]:c[2] + c[3]]), p)
        m = m_new
    o_ref[0] = (acc[:dv] / acc[dv:dv + 1]).T.astype(o_ref.dtype)


def _dense_attention(qt, k, vt, kc, vct, n_heads, group, tq, tk, sink=None):
    B, _, S = qt.shape
    Sk = k.shape[1]
    dk = qt.shape[1] // n_heads
    dv = vt.shape[1] // (n_heads // group)
    assert S % tq == 0 and Sk % tk == 0
    n_ctx = 0 if kc is None else kc.shape[1]
    in_specs = [
        pl.BlockSpec((1, dk, tq), lambda b, h, i: (b, h, i)),
        pl.BlockSpec((1, Sk, dk), lambda b, h, i: (b, 0, h // group)),
        pl.BlockSpec((1, dv, Sk), lambda b, h, i: (b, h // group, 0)),
    ]
    args = [qt, k, vt]
    if n_ctx:
        in_specs += [
            pl.BlockSpec((1, n_ctx, dk), lambda b, h, i: (b, 0, h // group)),
            pl.BlockSpec((1, dv, n_ctx), lambda b, h, i: (b, h // group, 0)),
        ]
        args += [kc, vct]
    if sink is not None:
        in_specs.append(pl.BlockSpec((1, 1, 1), lambda b, h, i: (h, 0, 0)))
        args.append((sink * LOG2E).reshape(n_heads, 1, 1).astype(F32))
    return pl.pallas_call(
        functools.partial(_dense_kernel, tk=tk, seq=Sk, n_ctx=n_ctx, has_sink=sink is not None),
        out_shape=jax.ShapeDtypeStruct((B, S, n_heads * dv), BF16),
        grid=(B, n_heads, S // tq),
        in_specs=in_specs,
        out_specs=pl.BlockSpec((1, tq, dv), lambda b, h, i: (b, i, h)),
        compiler_params=_cparams("parallel", "parallel", "arbitrary"),
        name="dense_attn",
    )(*args)


def _out_kernel(oa_ref, ob_ref, oc_ref, od_ref, w_ref, x_ref, gate_ref, g_ref, b_ref, o_ref, *, alpha, sub):
    for r0 in range(0, x_ref.shape[1], sub):
        rows = slice(r0, r0 + sub)
        y = None
        o = 0
        for r in (oa_ref, ob_ref, oc_ref, od_ref):
            n = r.shape[2]
            t = _dot(r[0, rows, :], w_ref[o:o + n, :])
            y = t if y is None else y + t
            o += n
        z = alpha * x_ref[0, rows, :] + gate_ref[0] * y
        o_ref[0, rows, :] = _layer_norm(z) * g_ref[...] + b_ref[...]


def _out_proj(mix, x, gate, sw, layer, alpha, tm):
    B, S, D = x.shape
    const = lambda arr: _layer_spec(arr, layer, lambda bb, i: (0, 0), arr.shape[1:], pipeline_mode=pl.Buffered(1))
    return pl.pallas_call(
        functools.partial(_out_kernel, alpha=alpha, sub=min(OUT_SUB_ROWS, tm)),
        out_shape=jax.ShapeDtypeStruct((B, S, D), F32),
        grid=(B, S // tm),
        in_specs=[pl.BlockSpec((1, tm, m.shape[2]), lambda bb, i: (bb, i, 0)) for m in mix] + [
            const(sw["w_out"]),
            pl.BlockSpec((1, tm, D), lambda bb, i: (bb, i, 0)),
            pl.BlockSpec((1, 1, D), lambda bb, i: (bb, 0, 0)),
            const(sw["ln1_g"]), const(sw["ln1_b"]),
        ],
        out_specs=pl.BlockSpec((1, tm, D), lambda bb, i: (bb, i, 0)),
        compiler_params=_cparams("parallel", "parallel"),
        name="out_proj",
    )(*mix, sw["w_out"], x, gate, sw["ln1_g"], sw["ln1_b"])


HALO = BF16_SUBLANES


def _ffn_kernel(x_ref, xp_ref, xn_ref, shift_ref, scale_ref, gate_ref, wg_ref, wu_ref, cw_ref, cb_ref, wd_ref,
                g_ref, b_ref, o_ref, h_ref, acc_ref, *, alpha, tm):
    i, f = pl.program_id(1), pl.program_id(2)

    @pl.when(f == 0)
    def _():
        mod = lambda t: _layer_norm(t) * (1.0 + scale_ref[0]) + shift_ref[0]
        keep_p = (i > 0).astype(F32)
        keep_n = (i < pl.num_programs(1) - 1).astype(F32)
        h_ref[0:HALO, :] = (mod(xp_ref[0]) * keep_p).astype(BF16)
        h_ref[HALO:HALO + tm, :] = mod(x_ref[0]).astype(BF16)
        h_ref[HALO + tm:, :] = (mod(xn_ref[0]) * keep_n).astype(BF16)
        acc_ref[...] = jnp.zeros(acc_ref.shape, F32)

    n = tm + 2 * HALO
    gt = _dot(h_ref[...], wg_ref[...])
    up = _dot(h_ref[HALO:HALO + tm, :], wu_ref[...])
    a = (pltpu.roll(gt, 1, 0) * cw_ref[0:1, :] + gt * cw_ref[1:2, :]
         + pltpu.roll(gt, n - 1, 0) * cw_ref[2:3, :] + cb_ref[...])[HALO:HALO + tm]
    y = (a * jax.nn.sigmoid(a) * up).astype(BF16)
    acc_ref[...] += _dot(y, wd_ref[...])

    @pl.when(f == pl.num_programs(2) - 1)
    def _():
        z = alpha * x_ref[0] + gate_ref[0] * acc_ref[...]
        o_ref[0] = _layer_norm(z) * g_ref[...] + b_ref[...]


def _ffn(x, shift, scale, gate, sw, layer, alpha, tm, tf):
    B, S, D = x.shape
    F = sw["ffn_w_gate"].shape[2]
    assert S % tm == 0 and F % tf == 0 and tm % HALO == 0
    nh = tm // HALO
    vec = pl.BlockSpec((1, 1, D), lambda bb, i, f: (bb, 0, 0))
    cols = lambda arr, rows: _layer_spec(arr, layer, lambda bb, i, f: (0, f), (rows, tf))
    const = lambda arr: _layer_spec(arr, layer, lambda bb, i, f: (0, 0), arr.shape[1:])
    return pl.pallas_call(
        functools.partial(_ffn_kernel, alpha=alpha, tm=tm),
        out_shape=jax.ShapeDtypeStruct((B, S, D), F32),
        grid=(B, S // tm, F // tf),
        in_specs=[
            pl.BlockSpec((1, tm, D), lambda bb, i, f: (bb, i, 0)),
            pl.BlockSpec((1, HALO, D), lambda bb, i, f: (bb, jnp.maximum(i * nh - 1, 0), 0)),
            pl.BlockSpec((1, HALO, D), lambda bb, i, f: (bb, jnp.minimum((i + 1) * nh, S // HALO - 1), 0)),
            vec, vec, vec,
            cols(sw["ffn_w_gate"], D), cols(sw["ffn_w_up"], D), cols(sw["ffn_conv_w"], CONV_W),
            cols(sw["ffn_conv_b"], 1),
            _layer_spec(sw["ffn_w_down"], layer, lambda bb, i, f: (f, 0), (tf, D)),
            const(sw["ln2_g"]), const(sw["ln2_b"]),
        ],
        out_specs=pl.BlockSpec((1, tm, D), lambda bb, i, f: (bb, i, 0)),
        scratch_shapes=[pltpu.VMEM((tm + 2 * HALO, D), BF16), pltpu.VMEM((tm, D), F32)],
        compiler_params=_cparams("parallel", "parallel", "arbitrary"),
        name="conv_ffn",
    )(x, x, x, shift, scale, gate, sw["ffn_w_gate"], sw["ffn_w_up"], sw["ffn_conv_w"], sw["ffn_conv_b"],
      sw["ffn_w_down"], sw["ln2_g"], sw["ln2_b"])


def _rope_tables(S):
    t = jnp.arange(S)
    row, col = (t // GRID_W).astype(F32), (t % GRID_W).astype(F32)
    lane = np.arange(LANE)

    def tables(dim):
        half = dim // 4
        inv_freq = ROPE_THETA ** (-jnp.arange(half, dtype=F32) / half)
        used = lane < dim
        pos = jnp.where(jnp.asarray((lane % dim) < dim // 2)[None, :], row[:, None], col[:, None])
        ang = pos * inv_freq[jnp.asarray(lane % half)][None, :]
        cos, sin = jnp.cos(ang), jnp.sin(ang)
        lo = jnp.asarray(used & ((lane % (2 * half)) < half))[None, :]
        hi = jnp.asarray(used & ((lane % (2 * half)) >= half))[None, :]
        return [jnp.where(jnp.asarray(used)[None, :], cos, 1.0), jnp.where(lo, -sin, 0.0), jnp.where(hi, sin, 0.0)]

    return jnp.stack(tables(HEAD_DIM) + tables(MLA_ROPE)).astype(F32)


def _identity_rope(C):
    one, zero = jnp.ones((C, LANE), F32), jnp.zeros((C, LANE), F32)
    return jnp.stack([one, zero, zero, one, zero, zero])


def _pad_w_in(w):
    cut = _C0 + MLA_Q_LORA + MLA_KV_LORA + MLA_ROPE
    pad = jnp.zeros(w.shape[:2] + (LANE - MLA_ROPE,), w.dtype)
    return jnp.concatenate([w[..., :cut], pad, w[..., cut:]], axis=-1).astype(BF16)


def _pad_w_uq(w):
    L, r, _ = w.shape
    w = w.reshape(L, r, MLA_HEADS, MLA_NOPE + MLA_ROPE)
    w = jnp.concatenate([w, jnp.zeros((L, r, MLA_HEADS, MLA_QK - MLA_NOPE - MLA_ROPE), w.dtype)], axis=3)
    return w.reshape(L, r, MLA_HEADS * MLA_QK).astype(BF16)


def kernel(x, c, ctx, c_ctx, w_ada, b_ada, w_in, na_rpb, swa_sink, mla_q_norm, mla_kv_norm, mla_w_uq, mla_w_ukv,
           gqa_q_norm, gqa_k_norm, w_out, ln1_g, ln1_b, ffn_w_gate, ffn_w_up, ffn_conv_w, ffn_conv_b, ffn_w_down,
           ln2_g, ln2_b):
    B, S, D = x.shape
    C = ctx.shape[1]
    depth = w_ada.shape[0]
    alpha = (2 * depth) ** 0.25

    cvec = jnp.concatenate([c, c_ctx[None, :], jnp.zeros((8 - B - 1, D), F32)], axis=0)
    ada = _ada(cvec, w_ada, b_ada)
    rope_x, rope_c = _rope_tables(S), _identity_rope(C)

    row = lambda a: a[:, None, :]
    sw = {
        "w_in": _pad_w_in(w_in), "w_uq": _pad_w_uq(mla_w_uq), "w_ukv": mla_w_ukv.astype(BF16),
        "mla_q_norm": row(mla_q_norm), "mla_kv_norm": row(mla_kv_norm),
        "gqa_q_norm": row(gqa_q_norm), "gqa_k_norm": row(gqa_k_norm),
        "w_out": w_out.astype(BF16), "ln1_g": row(ln1_g), "ln1_b": row(ln1_b),
        "ffn_w_gate": ffn_w_gate.astype(BF16), "ffn_w_up": ffn_w_up.astype(BF16), "ffn_conv_w": ffn_conv_w,
        "ffn_conv_b": row(ffn_conv_b), "ffn_w_down": ffn_w_down.astype(BF16), "ln2_g": row(ln2_g), "ln2_b": row(ln2_b),
    }

    tm_proj = min(256, S)
    tm_out = min(512, S)
    tm_ffn, tf = min(512, S), 512
    tq_dense, tk_dense = min(1024, S), min(512, S)
    tq_swa = min(256, S - 2 * SWA_WINDOW)
    g_swa, g_gqa = SWA_HEADS // SWA_KV_HEADS, GQA_HEADS // GQA_KV_HEADS

    for i in range(depth):
        need_ctx = i < depth - 1
        mx = ada[i, :B].reshape(B, 1, 6, D)
        mod_x = [mx[:, :, j] for j in range(6)]
        mod_c = [jnp.broadcast_to(ada[i, B].reshape(1, 1, 6, D)[:, :, j], (B, 1, D)) for j in range(6)]

        qa, ka, va, qb, kb, vb, qc, kc, vc, qd, kd, vd = _proj(x, mod_x[0], mod_x[1], rope_x, sw, i, tm_proj)
        qa_c, ka_c, va_c, qb_c, kb_c, vb_c, qc_c, kc_c, vc_c, qd_c, kd_c, vd_c = _proj(
            ctx, mod_c[0], mod_c[1], rope_c, sw, i, C)

        mix_x = (
            _na_attention(qa, ka, va, ka_c, va_c, na_rpb[i]),
            _swa_attention(qb, kb, vb, kb_c, vb_c, swa_sink[i], tq_swa),
            _dense_attention(qc, kc, vc, kc_c, vc_c, MLA_HEADS, 1, tq_dense, tk_dense),
            _dense_attention(qd, kd, vd, kd_c, vd_c, GQA_HEADS, g_gqa, tq_dense, tk_dense),
        )
        x = _out_proj(mix_x, x, mod_x[2], sw, i, alpha, tm_out)
        if need_ctx:
            mix_c = (
                _dense_attention(qa_c, ka_c, va_c, None, None, NA_HEADS, 1, C, C),
                _dense_attention(qb_c, kb_c, vb_c, None, None, SWA_HEADS, g_swa, C, C, sink=swa_sink[i]),
                _dense_attention(qc_c, kc_c, vc_c, None, None, MLA_HEADS, 1, C, C),
                _dense_attention(qd_c, kd_c, vd_c, None, None, GQA_HEADS, g_gqa, C, C),
            )
            ctx = _out_proj(mix_c, ctx, mod_c[2], sw, i, alpha, C)

        x = _ffn(x, mod_x[3], mod_x[4], mod_x[5], sw, i, alpha, tm_ffn, tf)
        if need_ctx:
            ctx = _ffn(ctx, mod_c[3], mod_c[4], mod_c[5], sw, i, alpha, C, tf)
    return x
```

```python
import functools
import math

import numpy as np
import jax
import jax.numpy as jnp
from jax import lax
from jax.experimental import pallas as pl
from jax.experimental.pallas import tpu as pltpu

GRID_W = 64
HEAD_DIM = 128
NA_HEADS = 4
NA_WIN_R = 8
NA_WIN_C = 16
SWA_HEADS = 4
SWA_KV_HEADS = 2
SWA_WINDOW = 128
MLA_HEADS = 4
MLA_Q_LORA = 384
MLA_KV_LORA = 128
MLA_NOPE = 128
MLA_ROPE = 64
MLA_V = 128
GQA_HEADS = 4
GQA_KV_HEADS = 2
CONV_W = 3
ROPE_THETA = 10000.0
EPS = 1e-6
NEG = -1e30
LOG2E = math.log2(math.e)

LANE = 128
BF16_SUBLANES = 16
VMEM_LIMIT = 56 * 1024 * 1024

F32 = jnp.float32
BF16 = jnp.bfloat16

OUT_SUB_ROWS = 256
NA_Q_ROWS = 4
NA_KEY_ROWS = NA_WIN_R + NA_Q_ROWS
NA_TILES_PER_STEP = 8
SWA_TILES_PER_STEP = 8


def _cparams(*sem):
    return pltpu.CompilerParams(dimension_semantics=sem, vmem_limit_bytes=VMEM_LIMIT)


def _dot(a, b):
    return jnp.dot(a, b, preferred_element_type=F32)


def _layer_norm(x):
    mu = jnp.mean(x, axis=-1, keepdims=True)
    xc = x - mu
    var = jnp.mean(xc * xc, axis=-1, keepdims=True)
    return xc * lax.rsqrt(var + EPS)


def _rms_norm(x, g):
    return x * lax.rsqrt(jnp.mean(x * x, axis=-1, keepdims=True) + EPS) * g


def _rope(x, cos, sin_lo, sin_hi, half):
    n = x.shape[-1]
    return x * cos + pltpu.roll(x, n - half, 1) * sin_lo + pltpu.roll(x, half, 1) * sin_hi


def _layer_spec(arr, layer, index_map_rest, block_rest, **kw):
    return pl.BlockSpec((None,) + tuple(block_rest), lambda *g: (layer,) + tuple(index_map_rest(*g)), **kw)


def _ada_kernel(c_ref, w_ref, b_ref, o_ref):
    c = c_ref[...]
    a = (c * jax.nn.sigmoid(c)).astype(BF16)
    o_ref[0] = _dot(a, w_ref[0].astype(BF16)) + b_ref[0]


def _ada(cvec, w_ada, b_ada):
    L, D, N = w_ada.shape
    M = cvec.shape[0]
    tn = 1024
    return pl.pallas_call(
        _ada_kernel,
        out_shape=jax.ShapeDtypeStruct((L, M, N), F32),
        grid=(L, N // tn),
        in_specs=[
            pl.BlockSpec((M, D), lambda l, j: (0, 0)),
            pl.BlockSpec((1, D, tn), lambda l, j: (l, 0, j)),
            pl.BlockSpec((1, 1, tn), lambda l, j: (l, 0, j)),
        ],
        out_specs=pl.BlockSpec((1, M, tn), lambda l, j: (l, 0, j)),
        compiler_params=_cparams("parallel", "parallel"),
        name="ada",
    )(cvec, w_ada, b_ada.reshape(L, 1, N))


_A0 = 0
_B0 = _A0 + 3 * NA_HEADS * HEAD_DIM
_C0 = _B0 + (SWA_HEADS + 2 * SWA_KV_HEADS) * HEAD_DIM
_C_W = MLA_Q_LORA + MLA_KV_LORA + LANE
_D0 = _C0 + _C_W
_D_W = (GQA_HEADS + 2 * GQA_KV_HEADS) * HEAD_DIM
IN_COLS_PAD = _D0 + _D_W
MLA_QK = 2 * LANE


def _proj_kernel(x_ref, shift_ref, scale_ref, w_ref, rope_ref, gq_lora_ref, gkv_lora_ref, gq_ref, gk_ref,
                 wuq_ref, wukv_ref,
                 qa_ref, ka_ref, va_ref, qb_ref, kb_ref, vb_ref, qc_ref, kc_ref, vc_ref, qd_ref, kd_ref, vd_ref):
    h = (_layer_norm(x_ref[0]) * (1.0 + scale_ref[0]) + shift_ref[0]).astype(BF16)
    cos2, slo2, shi2 = rope_ref[0], rope_ref[1], rope_ref[2]
    cos1, slo1, shi1 = rope_ref[3], rope_ref[4], rope_ref[5]
    rope2 = lambda t: _rope(t, cos2, slo2, shi2, HEAD_DIM // 4)
    rope1 = lambda t: _rope(t, cos1, slo1, shi1, MLA_ROPE // 4)
    hd = HEAD_DIM
    sc = hd ** -0.5 * LOG2E

    def store_t(ref, lo, t):
        ref[0, lo:lo + t.shape[1], :] = t.T.astype(BF16)


    wq, wk = GQA_HEADS * hd, GQA_KV_HEADS * hd
    pd = _dot(h, w_ref[:, _D0:_D0 + wq + 2 * wk])
    for i in range(GQA_HEADS):
        store_t(qd_ref, i * hd, rope2(_rms_norm(pd[:, i * hd:(i + 1) * hd], gq_ref[...])) * sc)
    for i in range(GQA_KV_HEADS):
        t = _rms_norm(pd[:, wq + i * hd:wq + (i + 1) * hd], gk_ref[...])
        kd_ref[0, :, i * hd:(i + 1) * hd] = rope2(t).astype(BF16)
    store_t(vd_ref, 0, pd[:, wq + wk:])

    pc = _dot(h, w_ref[:, _C0:_C0 + _C_W])
    cq = _rms_norm(pc[:, :MLA_Q_LORA], gq_lora_ref[...]).astype(BF16)
    ckv = _rms_norm(pc[:, MLA_Q_LORA:MLA_Q_LORA + MLA_KV_LORA], gkv_lora_ref[...]).astype(BF16)
    kpe = rope1(pc[:, MLA_Q_LORA + MLA_KV_LORA:]).astype(BF16)
    qup = _dot(cq, wuq_ref[...])
    kvup = _dot(ckv, wukv_ref[...])
    sc_mla = (MLA_NOPE + MLA_ROPE) ** -0.5 * LOG2E
    for i in range(MLA_HEADS):
        o = i * MLA_QK
        store_t(qc_ref, o, qup[:, o:o + LANE] * sc_mla)
        store_t(qc_ref, o + LANE, rope1(qup[:, o + LANE:o + 2 * LANE]) * sc_mla)
        kc_ref[0, :, o:o + LANE] = kvup[:, o:o + LANE].astype(BF16)
        kc_ref[0, :, o + LANE:o + 2 * LANE] = kpe
        store_t(vc_ref, i * MLA_V, kvup[:, o + LANE:o + 2 * LANE])

    wq, wk = SWA_HEADS * hd, SWA_KV_HEADS * hd
    pb = _dot(h, w_ref[:, _B0:_B0 + wq + 2 * wk])
    for i in range(SWA_HEADS):
        store_t(qb_ref, i * hd, rope2(pb[:, i * hd:(i + 1) * hd]) * sc)
    for i in range(SWA_KV_HEADS):
        kb_ref[0, :, i * hd:(i + 1) * hd] = rope2(pb[:, wq + i * hd:wq + (i + 1) * hd]).astype(BF16)
    store_t(vb_ref, 0, pb[:, wq + wk:])

    w = NA_HEADS * hd
    pa = _dot(h, w_ref[:, _A0:_A0 + 3 * w])
    store_t(qa_ref, 0, pa[:, :w] * sc)
    store_t(va_ref, 0, pa[:, 2 * w:])
    ka_ref[0] = pa[:, w:2 * w].astype(BF16)


def _proj(x, shift, scale, rope_tab, sw, layer, tm):
    B, S, D = x.shape
    widths = (NA_HEADS * HEAD_DIM,) * 3 + (SWA_HEADS * HEAD_DIM, SWA_KV_HEADS * HEAD_DIM, SWA_KV_HEADS * HEAD_DIM) \
        + (MLA_HEADS * MLA_QK, MLA_HEADS * MLA_QK, MLA_HEADS * MLA_V) \
        + (GQA_HEADS * HEAD_DIM, GQA_KV_HEADS * HEAD_DIM, GQA_KV_HEADS * HEAD_DIM)
    transposed = (True, False, True) * 4
    const = lambda arr: _layer_spec(arr, layer, lambda b, i: (0, 0), arr.shape[1:], pipeline_mode=pl.Buffered(1))
    vec = pl.BlockSpec((1, 1, D), lambda b, i: (b, 0, 0))
    out_spec = lambda n, t: (pl.BlockSpec((1, n, tm), lambda b, i: (b, 0, i)) if t
                             else pl.BlockSpec((1, tm, n), lambda b, i: (b, i, 0)))
    params = (sw["w_in"], None, sw["mla_q_norm"], sw["mla_kv_norm"], sw["gqa_q_norm"], sw["gqa_k_norm"],
              sw["w_uq"], sw["w_ukv"])
    in_specs = [pl.BlockSpec((1, tm, D), lambda b, i: (b, i, 0)), vec, vec]
    in_specs += [pl.BlockSpec((6, tm, LANE), lambda b, i: (0, i, 0)) if p is None else const(p) for p in params]
    args = [rope_tab if p is None else p for p in params]
    return pl.pallas_call(
        _proj_kernel,
        out_shape=[jax.ShapeDtypeStruct((B, n, S) if t else (B, S, n), BF16) for n, t in zip(widths, transposed)],
        grid=(B, S // tm),
        in_specs=in_specs,
        out_specs=[out_spec(n, t) for n, t in zip(widths, transposed)],
        compiler_params=_cparams("parallel", "parallel"),
        name="proj",
    )(x, shift, scale, *args)


def _with_ones_rows(vt):
    return jnp.concatenate([vt, jnp.ones((BF16_SUBLANES, vt.shape[1]), BF16)], axis=0)


def _softmax_pv(parts, sink=None):
    m = parts[0][0].max(axis=0, keepdims=True)
    for s, _ in parts[1:]:
        m = jnp.maximum(m, s.max(axis=0, keepdims=True))
    l = 0.0
    if sink is not None:
        m = jnp.maximum(m, sink)
        l = jnp.exp2(sink - m)
    acc = None
    for s, vt in parts:
        p = jnp.exp2(s - m)
        l = l + p.sum(axis=0, keepdims=True)
        pv = _dot(vt, p.astype(BF16))
        acc = pv if acc is None else acc + pv
    return acc / l


def _pipelined_tiles(n_tiles, scores, finish):
    nxt = scores(0)
    for t in range(n_tiles):
        cur = nxt
        if t + 1 < n_tiles:
            nxt = scores(t + 1)
        finish(t, cur)


def _na_kernel(cls_ref, ws_ref, qt_ref, k_ref, vt_ref, kc_ref, vct_ref, *rest):
    del cls_ref
    bias_refs, o_ref = rest[:-1], rest[-1]
    nk, tq = NA_KEY_ROWS * GRID_W, NA_Q_ROWS * GRID_W
    start = lambda t: pl.multiple_of(ws_ref[pl.program_id(2) * len(bias_refs) + t] * GRID_W, tq)

    def scores(t):
        qt = qt_ref[0, :, t * tq:(t + 1) * tq]
        return _dot(k_ref[0, pl.ds(start(t), nk), :], qt), _dot(kc_ref[0], qt)

    def finish(t, s):
        o = _softmax_pv([(s[0] + bias_refs[t][0, 0], vt_ref[0, :, pl.ds(start(t), nk)]), (s[1], vct_ref[0])])
        o_ref[0, t * tq:(t + 1) * tq, :] = o.T.astype(o_ref.dtype)

    _pipelined_tiles(len(bias_refs), scores, finish)


def _na_tables(rpb, rows):
    nt = rows // NA_Q_ROWS
    R = np.arange(nt) * NA_Q_ROWS
    ws = np.clip(R - NA_WIN_R // 2, 0, rows - NA_KEY_ROWS)
    qr = (R[:, None] + np.arange(NA_Q_ROWS)[None, :])
    kr = ws[:, None] + np.arange(NA_KEY_ROWS)[None, :]
    r0 = np.clip(qr - NA_WIN_R // 2, 0, rows - NA_WIN_R)
    row_ok = (kr[:, None, :] >= r0[:, :, None]) & (kr[:, None, :] < r0[:, :, None] + NA_WIN_R)
    drow = np.clip(kr[:, None, :] - qr[:, :, None] + NA_WIN_R - 1, 0, 2 * NA_WIN_R - 2)
    geom = np.concatenate([row_ok.reshape(nt, -1).astype(np.int64), drow.reshape(nt, -1)], axis=1)
    _, first, cls = np.unique(geom, axis=0, return_index=True, return_inverse=True)
    cq = np.arange(GRID_W)
    c0 = np.clip(cq - NA_WIN_C // 2, 0, GRID_W - NA_WIN_C)
    col_ok = (cq[None, :] >= c0[:, None]) & (cq[None, :] < c0[:, None] + NA_WIN_C)
    dcol = np.clip(cq[None, :] - cq[:, None] + NA_WIN_C - 1, 0, 2 * NA_WIN_C - 2)
    nc, H = len(first), rpb.shape[0]
    cols = jnp.zeros(rpb.shape[:2] + dcol.shape, F32)
    for j in range(rpb.shape[2]):
        cols = jnp.where(jnp.asarray(dcol == j)[None, None], rpb[:, :, j][:, :, None, None] * LOG2E, cols)
    cols = jnp.where(jnp.asarray(col_ok)[None, None], cols, NEG)
    nd = rpb.shape[1]
    pad = jnp.full((H, NA_KEY_ROWS * GRID_W, GRID_W), NEG, F32)
    flat = jnp.concatenate([pad, cols.transpose(0, 1, 3, 2).reshape(H, nd * GRID_W, GRID_W), pad], axis=1)
    per_class = []
    for c in first:
        strips = []
        for q in range(NA_Q_ROWS):
            d0 = int(kr[c, 0] - qr[c, q]) + NA_WIN_R - 1 + NA_KEY_ROWS
            strip = flat[:, d0 * GRID_W:(d0 + NA_KEY_ROWS) * GRID_W, :]
            keep = np.repeat(row_ok[c, q], GRID_W)
            strips.append(jnp.where(jnp.asarray(keep)[None, :, None], strip, NEG))
        per_class.append(jnp.concatenate(strips, axis=-1))
    bias = jnp.stack(per_class)
    return bias, jnp.asarray(cls.reshape(-1), jnp.int32), jnp.asarray(ws, jnp.int32)


def _na_attention(qt, k, vt, kc, vct, rpb):
    B, _, S = qt.shape
    C = kc.shape[1]
    rows = S // GRID_W
    assert rows >= NA_KEY_ROWS and rows % NA_Q_ROWS == 0
    bias, cls, ws = _na_tables(rpb, rows)
    tq, nk, hd = NA_Q_ROWS * GRID_W, NA_KEY_ROWS * GRID_W, HEAD_DIM
    n_sub = NA_TILES_PER_STEP if (S // tq) % NA_TILES_PER_STEP == 0 else 1
    bias_spec = lambda t: pl.BlockSpec((1, 1, nk, tq), lambda b, h, i, c, w: (c[i * n_sub + t], h, 0, 0))
    grid_spec = pltpu.PrefetchScalarGridSpec(
        num_scalar_prefetch=2,
        grid=(B, NA_HEADS, S // (tq * n_sub)),
        in_specs=[
            pl.BlockSpec((1, hd, n_sub * tq), lambda b, h, i, c, w: (b, h, i)),
            pl.BlockSpec((1, S, hd), lambda b, h, i, c, w: (b, 0, h)),
            pl.BlockSpec((1, hd, S), lambda b, h, i, c, w: (b, h, 0)),
            pl.BlockSpec((1, C, hd), lambda b, h, i, c, w: (b, 0, h)),
            pl.BlockSpec((1, hd, C), lambda b, h, i, c, w: (b, h, 0)),
        ] + [bias_spec(t) for t in range(n_sub)],
        out_specs=pl.BlockSpec((1, n_sub * tq, hd), lambda b, h, i, c, w: (b, i, h)),
    )
    return pl.pallas_call(
        _na_kernel,
        out_shape=jax.ShapeDtypeStruct((B, S, NA_HEADS * hd), BF16),
        grid_spec=grid_spec,
        compiler_params=_cparams("parallel", "parallel", "arbitrary"),
        name="na_attn",
    )(cls, ws, qt, k, vt, kc, vct, *([bias] * n_sub))


def _swa_kernel(qt_ref, k_ref, vt_ref, kc_ref, vct_ref, sink_ref, o_ref, *, tq, n_sub, seq):
    nk = tq + 2 * SWA_WINDOW
    tile = lambda t: pl.program_id(2) * n_sub + t
    start = lambda t: pl.multiple_of(jnp.clip(tile(t) * tq - SWA_WINDOW, 0, seq - nk), SWA_WINDOW)

    def scores(t):
        qt = qt_ref[0, :, t * tq:(t + 1) * tq]
        return _dot(k_ref[0, pl.ds(start(t), nk), :], qt), _dot(kc_ref[0], qt)

    def finish(t, s):
        kpos = start(t) + lax.broadcasted_iota(jnp.int32, (nk, tq), 0)
        qpos = tile(t) * tq + lax.broadcasted_iota(jnp.int32, (nk, tq), 1)
        s_win = jnp.where(jnp.abs(qpos - kpos) <= SWA_WINDOW, s[0], NEG)
        o = _softmax_pv([(s_win, vt_ref[0, :, pl.ds(start(t), nk)]), (s[1], vct_ref[0])], sink=sink_ref[0])
        o_ref[0, t * tq:(t + 1) * tq, :] = o.T.astype(o_ref.dtype)

    _pipelined_tiles(n_sub, scores, finish)


def _swa_attention(qt, k, vt, kc, vct, sink, tq):
    B, _, S = qt.shape
    C = kc.shape[1]
    hd, G = HEAD_DIM, SWA_HEADS // SWA_KV_HEADS
    assert S >= tq + 2 * SWA_WINDOW and S % tq == 0 and tq % SWA_WINDOW == 0
    n_sub = SWA_TILES_PER_STEP if (S // tq) % SWA_TILES_PER_STEP == 0 else 1
    return pl.pallas_call(
        functools.partial(_swa_kernel, tq=tq, n_sub=n_sub, seq=S),
        out_shape=jax.ShapeDtypeStruct((B, S, SWA_HEADS * hd), BF16),
        grid=(B, SWA_HEADS, S // (tq * n_sub)),
        in_specs=[
            pl.BlockSpec((1, hd, n_sub * tq), lambda b, h, i: (b, h, i)),
            pl.BlockSpec((1, S, hd), lambda b, h, i: (b, 0, h // G)),
            pl.BlockSpec((1, hd, S), lambda b, h, i: (b, h // G, 0)),
            pl.BlockSpec((1, C, hd), lambda b, h, i: (b, 0, h // G)),
            pl.BlockSpec((1, hd, C), lambda b, h, i: (b, h // G, 0)),
            pl.BlockSpec((1, 1, 1), lambda b, h, i: (h, 0, 0)),
        ],
        out_specs=pl.BlockSpec((1, n_sub * tq, hd), lambda b, h, i: (b, i, h)),
        compiler_params=_cparams("parallel", "parallel", "arbitrary"),
        name="swa_attn",
    )(qt, k, vt, kc, vct, (sink * LOG2E).reshape(SWA_HEADS, 1, 1).astype(F32))


def _dense_kernel(*refs, tk, seq, n_ctx, has_sink):
    refs = list(refs)
    qt_ref, k_ref, vt_ref = refs[:3]
    o_ref = refs[-1]
    kc_ref, vct_ref = refs[3:5] if n_ctx else (None, None)
    qt = qt_ref[0]
    tq = qt.shape[1]
    dv = vt_ref.shape[1]
    if has_sink:
        m = jnp.broadcast_to(refs[-2][0], (1, tq))
        acc = jnp.concatenate([jnp.zeros((dv, tq), F32), jnp.ones((BF16_SUBLANES, tq), F32)], axis=0)
    else:
        m = jnp.full((1, tq), -jnp.inf, F32)
        acc = jnp.zeros((dv + BF16_SUBLANES, tq), F32)

    chunks = [(k_ref, vt_ref, j * tk, tk) for j in range(seq // tk)]
    if n_ctx:
        chunks.append((kc_ref, vct_ref, 0, n_ctx))
    scores = lambda c: _dot(c[0][0, c[2]:c[2] + c[3], :], qt)

    s_next = scores(chunks[0])
    for j, c in enumerate(chunks):
        s = s_next
        if j + 1 < len(chunks):
            s_next = scores(chunks[j + 1])
        m_new = jnp.maximum(m, s.max(axis=0, keepdims=True))
        alpha = jnp.exp2(m - m_new)
        p = jnp.exp2(s - m_new).astype(BF16)
        acc = alpha * acc + _dot(_with_ones_rows(c[1][0, :, c[2]:c[2] + c[3]]), p)
        m = m_new
    o_ref[0] = (acc[:dv] / acc[dv:dv + 1]).T.astype(o_ref.dtype)


def _dense_attention(qt, k, vt, kc, vct, n_heads, group, tq, tk, sink=None):
    B, _, S = qt.shape
    Sk = k.shape[1]
    dk = qt.shape[1] // n_heads
    dv = vt.shape[1] // (n_heads // group)
    assert S % tq == 0 and Sk % tk == 0
    n_ctx = 0 if kc is None else kc.shape[1]
    in_specs = [
        pl.BlockSpec((1, dk, tq), lambda b, h, i: (b, h, i)),
        pl.BlockSpec((1, Sk, dk), lambda b, h, i: (b, 0, h // group)),
        pl.BlockSpec((1, dv, Sk), lambda b, h, i: (b, h // group, 0)),
    ]
    args = [qt, k, vt]
    if n_ctx:
        in_specs += [
            pl.BlockSpec((1, n_ctx, dk), lambda b, h, i: (b, 0, h // group)),
            pl.BlockSpec((1, dv, n_ctx), lambda b, h, i: (b, h // group, 0)),
        ]
        args += [kc, vct]
    if sink is not None:
        in_specs.append(pl.BlockSpec((1, 1, 1), lambda b, h, i: (h, 0, 0)))
        args.append((sink * LOG2E).reshape(n_heads, 1, 1).astype(F32))
    return pl.pallas_call(
        functools.partial(_dense_kernel, tk=tk, seq=Sk, n_ctx=n_ctx, has_sink=sink is not None),
        out_shape=jax.ShapeDtypeStruct((B, S, n_heads * dv), BF16),
        grid=(B, n_heads, S // tq),
        in_specs=in_specs,
        out_specs=pl.BlockSpec((1, tq, dv), lambda b, h, i: (b, i, h)),
        compiler_params=_cparams("parallel", "parallel", "arbitrary"),
        name="dense_attn",
    )(*args)


def _out_kernel(oa_ref, ob_ref, oc_ref, od_ref, w_ref, x_ref, gate_ref, g_ref, b_ref, o_ref, *, alpha, sub):
    for r0 in range(0, x_ref.shape[1], sub):
        rows = slice(r0, r0 + sub)
        y = None
        o = 0
        for r in (oa_ref, ob_ref, oc_ref, od_ref):
            n = r.shape[2]
            t = _dot(r[0, rows, :], w_ref[o:o + n, :])
            y = t if y is None else y + t
            o += n
        z = alpha * x_ref[0, rows, :] + gate_ref[0] * y
        o_ref[0, rows, :] = _layer_norm(z) * g_ref[...] + b_ref[...]


def _out_proj(mix, x, gate, sw, layer, alpha, tm):
    B, S, D = x.shape
    const = lambda arr: _layer_spec(arr, layer, lambda bb, i: (0, 0), arr.shape[1:], pipeline_mode=pl.Buffered(1))
    return pl.pallas_call(
        functools.partial(_out_kernel, alpha=alpha, sub=min(OUT_SUB_ROWS, tm)),
        out_shape=jax.ShapeDtypeStruct((B, S, D), F32),
        grid=(B, S // tm),
        in_specs=[pl.BlockSpec((1, tm, m.shape[2]), lambda bb, i: (bb, i, 0)) for m in mix] + [
            const(sw["w_out"]),
            pl.BlockSpec((1, tm, D), lambda bb, i: (bb, i, 0)),
            pl.BlockSpec((1, 1, D), lambda bb, i: (bb, 0, 0)),
            const(sw["ln1_g"]), const(sw["ln1_b"]),
        ],
        out_specs=pl.BlockSpec((1, tm, D), lambda bb, i: (bb, i, 0)),
        compiler_params=_cparams("parallel", "parallel"),
        name="out_proj",
    )(*mix, sw["w_out"], x, gate, sw["ln1_g"], sw["ln1_b"])


HALO = BF16_SUBLANES


def _ffn_kernel(x_hbm, xp_ref, xn_ref, shift_ref, scale_ref, gate_ref, wg_ref, wu_ref, cw_ref, cb_ref, wd_ref,
                g_ref, b_ref, o_ref, h_ref, x_buf, x_sem, *, alpha, tm):
    bb, i, f = pl.program_id(0), pl.program_id(1), pl.program_id(2)
    n_b, n_i = pl.num_programs(0), pl.num_programs(1)
    acc_ref = o_ref.at[0]

    def x_copy(b_idx, i_idx):
        return pltpu.make_async_copy(x_hbm.at[b_idx, pl.ds(i_idx * tm, tm), :], x_buf, x_sem)

    @pl.when(f == 0)
    def _():
        @pl.when((bb == 0) & (i == 0))
        def _():
            x_copy(0, 0).start()

        x_copy(bb, i).wait()
        xv = x_buf[...]
        mod = lambda t: _layer_norm(t) * (1.0 + scale_ref[0]) + shift_ref[0]
        keep_p = (i > 0).astype(F32)
        keep_n = (i < n_i - 1).astype(F32)
        h_ref[0:HALO, :] = (mod(xp_ref[0]) * keep_p).astype(BF16)
        h_ref[HALO:HALO + tm, :] = mod(xv).astype(BF16)
        h_ref[HALO + tm:, :] = (mod(xn_ref[0]) * keep_n).astype(BF16)
        acc_ref[...] = alpha * xv

    @pl.when((f == 1) & ((bb < n_b - 1) | (i < n_i - 1)))
    def _():
        wrap = i == n_i - 1
        x_copy(jnp.where(wrap, bb + 1, bb), jnp.where(wrap, 0, i + 1)).start()

    n = tm + 2 * HALO
    gt = _dot(h_ref[...], wg_ref[...])
    up = _dot(h_ref[HALO:HALO + tm, :], wu_ref[...])
    a = (pltpu.roll(gt, 1, 0) * cw_ref[0:1, :] + gt * cw_ref[1:2, :]
         + pltpu.roll(gt, n - 1, 0) * cw_ref[2:3, :] + cb_ref[...])[HALO:HALO + tm]
    y = (a * jax.nn.sigmoid(a) * up).astype(BF16)
    acc_ref[...] += gate_ref[0] * _dot(y, wd_ref[...])

    @pl.when(f == pl.num_programs(2) - 1)
    def _():
        o_ref[0] = _layer_norm(acc_ref[...]) * g_ref[...] + b_ref[...]


def _ffn(x, shift, scale, gate, sw, layer, alpha, tm, tf):
    B, S, D = x.shape
    F = sw["ffn_w_gate"].shape[2]
    assert S % tm == 0 and F % tf == 0 and tm % HALO == 0 and F // tf >= 2
    nh = tm // HALO
    vec = pl.BlockSpec((1, 1, D), lambda bb, i, f: (bb, 0, 0))
    cols = lambda arr, rows: _layer_spec(arr, layer, lambda bb, i, f: (0, f), (rows, tf))
    const = lambda arr: _layer_spec(arr, layer, lambda bb, i, f: (0, 0), arr.shape[1:])
    return pl.pallas_call(
        functools.partial(_ffn_kernel, alpha=alpha, tm=tm),
        out_shape=jax.ShapeDtypeStruct((B, S, D), F32),
        grid=(B, S // tm, F // tf),
        in_specs=[
            pl.BlockSpec(memory_space=pl.ANY),
            pl.BlockSpec((1, HALO, D), lambda bb, i, f: (bb, jnp.maximum(i * nh - 1, 0), 0)),
            pl.BlockSpec((1, HALO, D), lambda bb, i, f: (bb, jnp.minimum((i + 1) * nh, S // HALO - 1), 0)),
            vec, vec, vec,
            cols(sw["ffn_w_gate"], D), cols(sw["ffn_w_up"], D), cols(sw["ffn_conv_w"], CONV_W),
            cols(sw["ffn_conv_b"], 1),
            _layer_spec(sw["ffn_w_down"], layer, lambda bb, i, f: (f, 0), (tf, D)),
            const(sw["ln2_g"]), const(sw["ln2_b"]),
        ],
        out_specs=pl.BlockSpec((1, tm, D), lambda bb, i, f: (bb, i, 0)),
        scratch_shapes=[pltpu.VMEM((tm + 2 * HALO, D), BF16), pltpu.VMEM((tm, D), F32), pltpu.SemaphoreType.DMA(())],
        compiler_params=_cparams("arbitrary", "arbitrary", "arbitrary"),
        name="conv_ffn",
    )(x, x, x, shift, scale, gate, sw["ffn_w_gate"], sw["ffn_w_up"], sw["ffn_conv_w"], sw["ffn_conv_b"],
      sw["ffn_w_down"], sw["ln2_g"], sw["ln2_b"])


def _rope_tables(S):
    t = jnp.arange(S)
    row, col = (t // GRID_W).astype(F32), (t % GRID_W).astype(F32)
    lane = np.arange(LANE)

    def tables(dim):
        half = dim // 4
        inv_freq = ROPE_THETA ** (-jnp.arange(half, dtype=F32) / half)
        used = lane < dim
        pos = jnp.where(jnp.asarray((lane % dim) < dim // 2)[None, :], row[:, None], col[:, None])
        ang = pos * inv_freq[jnp.asarray(lane % half)][None, :]
        cos, sin = jnp.cos(ang), jnp.sin(ang)
        lo = jnp.asarray(used & ((lane % (2 * half)) < half))[None, :]
        hi = jnp.asarray(used & ((lane % (2 * half)) >= half))[None, :]
        return [jnp.where(jnp.asarray(used)[None, :], cos, 1.0), jnp.where(lo, -sin, 0.0), jnp.where(hi, sin, 0.0)]

    return jnp.stack(tables(HEAD_DIM) + tables(MLA_ROPE)).astype(F32)


def _identity_rope(C):
    one, zero = jnp.ones((C, LANE), F32), jnp.zeros((C, LANE), F32)
    return jnp.stack([one, zero, zero, one, zero, zero])


def _pad_w_in(w):
    cut = _C0 + MLA_Q_LORA + MLA_KV_LORA + MLA_ROPE
    pad = jnp.zeros(w.shape[:2] + (LANE - MLA_ROPE,), w.dtype)
    return jnp.concatenate([w[..., :cut], pad, w[..., cut:]], axis=-1).astype(BF16)


def _pad_w_uq(w):
    L, r, _ = w.shape
    w = w.reshape(L, r, MLA_HEADS, MLA_NOPE + MLA_ROPE)
    w = jnp.concatenate([w, jnp.zeros((L, r, MLA_HEADS, MLA_QK - MLA_NOPE - MLA_ROPE), w.dtype)], axis=3)
    return w.reshape(L, r, MLA_HEADS * MLA_QK).astype(BF16)


def kernel(x, c, ctx, c_ctx, w_ada, b_ada, w_in, na_rpb, swa_sink, mla_q_norm, mla_kv_norm, mla_w_uq, mla_w_ukv,
           gqa_q_norm, gqa_k_norm, w_out, ln1_g, ln1_b, ffn_w_gate, ffn_w_up, ffn_conv_w, ffn_conv_b, ffn_w_down,
           ln2_g, ln2_b):
    B, S, D = x.shape
    C = ctx.shape[1]
    depth = w_ada.shape[0]
    alpha = (2 * depth) ** 0.25

    cvec = jnp.concatenate([c, c_ctx[None, :], jnp.zeros((8 - B - 1, D), F32)], axis=0)
    ada = _ada(cvec, w_ada, b_ada)
    rope_x, rope_c = _rope_tables(S), _identity_rope(C)

    row = lambda a: a[:, None, :]
    sw = {
        "w_in": _pad_w_in(w_in), "w_uq": _pad_w_uq(mla_w_uq), "w_ukv": mla_w_ukv.astype(BF16),
        "mla_q_norm": row(mla_q_norm), "mla_kv_norm": row(mla_kv_norm),
        "gqa_q_norm": row(gqa_q_norm), "gqa_k_norm": row(gqa_k_norm),
        "w_out": w_out.astype(BF16), "ln1_g": row(ln1_g), "ln1_b": row(ln1_b),
        "ffn_w_gate": ffn_w_gate.astype(BF16), "ffn_w_up": ffn_w_up.astype(BF16), "ffn_conv_w": ffn_conv_w,
        "ffn_conv_b": row(ffn_conv_b), "ffn_w_down": ffn_w_down.astype(BF16), "ln2_g": row(ln2_g), "ln2_b": row(ln2_b),
    }

    tm_proj = min(256, S)
    tm_out = min(512, S)
    tm_ffn, tf = min(1024, S), 512
    tq_dense, tk_dense = min(1024, S), min(512, S)
    tq_swa = min(256, S - 2 * SWA_WINDOW)
    g_swa, g_gqa = SWA_HEADS // SWA_KV_HEADS, GQA_HEADS // GQA_KV_HEADS

    for i in range(depth):
        need_ctx = i < depth - 1
        mx = ada[i, :B].reshape(B, 1, 6, D)
        mod_x = [mx[:, :, j] for j in range(6)]
        mod_c = [jnp.broadcast_to(ada[i, B].reshape(1, 1, 6, D)[:, :, j], (B, 1, D)) for j in range(6)]

        qa, ka, va, qb, kb, vb, qc, kc, vc, qd, kd, vd = _proj(x, mod_x[0], mod_x[1], rope_x, sw, i, tm_proj)
        qa_c, ka_c, va_c, qb_c, kb_c, vb_c, qc_c, kc_c, vc_c, qd_c, kd_c, vd_c = _proj(
            ctx, mod_c[0], mod_c[1], rope_c, sw, i, C)

        mix_x = (
            _na_attention(qa, ka, va, ka_c, va_c, na_rpb[i]),
            _swa_attention(qb, kb, vb, kb_c, vb_c, swa_sink[i], tq_swa),
            _dense_attention(qc, kc, vc, kc_c, vc_c, MLA_HEADS, 1, tq_dense, tk_dense),
            _dense_attention(qd, kd, vd, kd_c, vd_c, GQA_HEADS, g_gqa, tq_dense, tk_dense),
        )
        x = _out_proj(mix_x, x, mod_x[2], sw, i, alpha, tm_out)
        if need_ctx:
            mix_c = (
                _dense_attention(qa_c, ka_c, va_c, None, None, NA_HEADS, 1, C, C),
                _dense_attention(qb_c, kb_c, vb_c, None, None, SWA_HEADS, g_swa, C, C, sink=swa_sink[i]),
                _dense_attention(qc_c, kc_c, vc_c, None, None, MLA_HEADS, 1, C, C),
                _dense_attention(qd_c, kd_c, vd_c, None, None, GQA_HEADS, g_gqa, C, C),
            )
            ctx = _out_proj(mix_c, ctx, mod_c[2], sw, i, alpha, C)

        x = _ffn(x, mod_x[3], mod_x[4], mod_x[5], sw, i, alpha, tm_ffn, tf)
        if need_ctx:
            ctx = _ffn(ctx, mod_c[3], mod_c[4], mod_c[5], sw, i, alpha, C, tf)
    return x
```

```python
import functools
import math

import numpy as np
import jax
import jax.numpy as jnp
from jax import lax
from jax.experimental import pallas as pl
from jax.experimental.pallas import tpu as pltpu

GRID_W = 64
HEAD_DIM = 128
NA_HEADS = 4
NA_WIN_R = 8
NA_WIN_C = 16
SWA_HEADS = 4
SWA_KV_HEADS = 2
SWA_WINDOW = 128
MLA_HEADS = 4
MLA_Q_LORA = 384
MLA_KV_LORA = 128
MLA_NOPE = 128
MLA_ROPE = 64
MLA_V = 128
GQA_HEADS = 4
GQA_KV_HEADS = 2
CONV_W = 3
ROPE_THETA = 10000.0
EPS = 1e-6
NEG = -1e30
LOG2E = math.log2(math.e)

LANE = 128
BF16_SUBLANES = 16
VMEM_LIMIT = 56 * 1024 * 1024

F32 = jnp.float32
BF16 = jnp.bfloat16

OUT_SUB_ROWS = 256
NA_Q_ROWS = 4
NA_KEY_ROWS = NA_WIN_R + NA_Q_ROWS
NA_TILES_PER_STEP = 8
SWA_TILES_PER_STEP = 8


def _cparams(*sem):
    return pltpu.CompilerParams(dimension_semantics=sem, vmem_limit_bytes=VMEM_LIMIT)


def _dot(a, b):
    return jnp.dot(a, b, preferred_element_type=F32)


def _layer_norm(x):
    mu = jnp.mean(x, axis=-1, keepdims=True)
    xc = x - mu
    var = jnp.mean(xc * xc, axis=-1, keepdims=True)
    return xc * lax.rsqrt(var + EPS)


def _rms_norm(x, g):
    return x * lax.rsqrt(jnp.mean(x * x, axis=-1, keepdims=True) + EPS) * g


def _rope(x, cos, sin_lo, sin_hi, half):
    n = x.shape[-1]
    return x * cos + pltpu.roll(x, n - half, 1) * sin_lo + pltpu.roll(x, half, 1) * sin_hi


def _layer_spec(arr, layer, index_map_rest, block_rest, **kw):
    return pl.BlockSpec((None,) + tuple(block_rest), lambda *g: (layer,) + tuple(index_map_rest(*g)), **kw)


def _ada_kernel(c_ref, w_ref, b_ref, o_ref):
    c = c_ref[...]
    a = (c * jax.nn.sigmoid(c)).astype(BF16)
    o_ref[0] = _dot(a, w_ref[0].astype(BF16)) + b_ref[0]


def _ada(cvec, w_ada, b_ada):
    L, D, N = w_ada.shape
    M = cvec.shape[0]
    tn = 1024
    return pl.pallas_call(
        _ada_kernel,
        out_shape=jax.ShapeDtypeStruct((L, M, N), F32),
        grid=(L, N // tn),
        in_specs=[
            pl.BlockSpec((M, D), lambda l, j: (0, 0)),
            pl.BlockSpec((1, D, tn), lambda l, j: (l, 0, j)),
            pl.BlockSpec((1, 1, tn), lambda l, j: (l, 0, j)),
        ],
        out_specs=pl.BlockSpec((1, M, tn), lambda l, j: (l, 0, j)),
        compiler_params=_cparams("parallel", "parallel"),
        name="ada",
    )(cvec, w_ada, b_ada.reshape(L, 1, N))


_A0 = 0
_B0 = _A0 + 3 * NA_HEADS * HEAD_DIM
_C0 = _B0 + (SWA_HEADS + 2 * SWA_KV_HEADS) * HEAD_DIM
_KPE0 = _C0 + MLA_Q_LORA + MLA_KV_LORA
_ABC_W = _KPE0 + MLA_ROPE
_D_W = (GQA_HEADS + 2 * GQA_KV_HEADS) * HEAD_DIM
MLA_QK = 2 * LANE


def _proj_kernel(x_ref, shift_ref, scale_ref, w_ref, wd_ref, rope_ref, gq_lora_ref, gkv_lora_ref, gq_ref, gk_ref,
                 wuq_ref, wukv_ref,
                 qa_ref, ka_ref, va_ref, qb_ref, kb_ref, vb_ref, qc_ref, kc_ref, vc_ref, qd_ref, kd_ref, vd_ref):
    h = (_layer_norm(x_ref[0]) * (1.0 + scale_ref[0]) + shift_ref[0]).astype(BF16)
    cos2, slo2, shi2 = rope_ref[0], rope_ref[1], rope_ref[2]
    cos1, slo1, shi1 = rope_ref[3], rope_ref[4], rope_ref[5]
    rope2 = lambda t: _rope(t, cos2, slo2, shi2, HEAD_DIM // 4)
    rope1 = lambda t: _rope(t, cos1, slo1, shi1, MLA_ROPE // 4)
    hd = HEAD_DIM
    sc = hd ** -0.5 * LOG2E

    def store_t(ref, lo, t):
        ref[0, lo:lo + t.shape[1], :] = t.T.astype(BF16)


    wq, wk = GQA_HEADS * hd, GQA_KV_HEADS * hd
    pd = _dot(h, wd_ref[...])
    for i in range(GQA_HEADS):
        store_t(qd_ref, i * hd, rope2(_rms_norm(pd[:, i * hd:(i + 1) * hd], gq_ref[...])) * sc)
    for i in range(GQA_KV_HEADS):
        t = _rms_norm(pd[:, wq + i * hd:wq + (i + 1) * hd], gk_ref[...])
        kd_ref[0, :, i * hd:(i + 1) * hd] = rope2(t).astype(BF16)
    store_t(vd_ref, 0, pd[:, wq + wk:])

    pc = _dot(h, w_ref[:, _C0:_KPE0])
    cq = _rms_norm(pc[:, :MLA_Q_LORA], gq_lora_ref[...]).astype(BF16)
    ckv = _rms_norm(pc[:, MLA_Q_LORA:], gkv_lora_ref[...]).astype(BF16)
    kpe = _dot(h, w_ref[:, _KPE0:_ABC_W])
    kpe = rope1(jnp.concatenate([kpe, jnp.zeros((kpe.shape[0], LANE - MLA_ROPE), F32)], axis=1)).astype(BF16)
    qup = _dot(cq, wuq_ref[...])
    kvup = _dot(ckv, wukv_ref[...])
    sc_mla = (MLA_NOPE + MLA_ROPE) ** -0.5 * LOG2E
    for i in range(MLA_HEADS):
        o = i * MLA_QK
        store_t(qc_ref, o, qup[:, o:o + LANE] * sc_mla)
        store_t(qc_ref, o + LANE, rope1(qup[:, o + LANE:o + 2 * LANE]) * sc_mla)
        kc_ref[0, :, o:o + LANE] = kvup[:, o:o + LANE].astype(BF16)
        kc_ref[0, :, o + LANE:o + 2 * LANE] = kpe
        store_t(vc_ref, i * MLA_V, kvup[:, o + LANE:o + 2 * LANE])

    wq, wk = SWA_HEADS * hd, SWA_KV_HEADS * hd
    pb = _dot(h, w_ref[:, _B0:_B0 + wq + 2 * wk])
    for i in range(SWA_HEADS):
        store_t(qb_ref, i * hd, rope2(pb[:, i * hd:(i + 1) * hd]) * sc)
    for i in range(SWA_KV_HEADS):
        kb_ref[0, :, i * hd:(i + 1) * hd] = rope2(pb[:, wq + i * hd:wq + (i + 1) * hd]).astype(BF16)
    store_t(vb_ref, 0, pb[:, wq + wk:])

    w = NA_HEADS * hd
    pa = _dot(h, w_ref[:, _A0:_A0 + 3 * w])
    store_t(qa_ref, 0, pa[:, :w] * sc)
    store_t(va_ref, 0, pa[:, 2 * w:])
    ka_ref[0] = pa[:, w:2 * w].astype(BF16)


def _proj(x, shift, scale, rope_tab, sw, layer, tm):
    B, S, D = x.shape
    widths = (NA_HEADS * HEAD_DIM,) * 3 + (SWA_HEADS * HEAD_DIM, SWA_KV_HEADS * HEAD_DIM, SWA_KV_HEADS * HEAD_DIM) \
        + (MLA_HEADS * MLA_QK, MLA_HEADS * MLA_QK, MLA_HEADS * MLA_V) \
        + (GQA_HEADS * HEAD_DIM, GQA_KV_HEADS * HEAD_DIM, GQA_KV_HEADS * HEAD_DIM)
    transposed = (True, False, True) * 4
    const = lambda arr: _layer_spec(arr, layer, lambda b, i: (0, 0), arr.shape[1:], pipeline_mode=pl.Buffered(1))
    vec = pl.BlockSpec((1, 1, D), lambda b, i: (b, 0, 0))
    out_spec = lambda n, t: (pl.BlockSpec((1, n, tm), lambda b, i: (b, 0, i)) if t
                             else pl.BlockSpec((1, tm, n), lambda b, i: (b, i, 0)))
    params = (sw["w_in_abc"], sw["w_in_d"], None, sw["mla_q_norm"], sw["mla_kv_norm"], sw["gqa_q_norm"],
              sw["gqa_k_norm"], sw["w_uq"], sw["w_ukv"])
    in_specs = [pl.BlockSpec((1, tm, D), lambda b, i: (b, i, 0)), vec, vec]
    in_specs += [pl.BlockSpec((6, tm, LANE), lambda b, i: (0, i, 0)) if p is None else const(p) for p in params]
    args = [rope_tab if p is None else p for p in params]
    return pl.pallas_call(
        _proj_kernel,
        out_shape=[jax.ShapeDtypeStruct((B, n, S) if t else (B, S, n), BF16) for n, t in zip(widths, transposed)],
        grid=(B, S // tm),
        in_specs=in_specs,
        out_specs=[out_spec(n, t) for n, t in zip(widths, transposed)],
        compiler_params=_cparams("parallel", "parallel"),
        name="proj",
    )(x, shift, scale, *args)


def _with_ones_rows(vt):
    return jnp.concatenate([vt, jnp.ones((BF16_SUBLANES, vt.shape[1]), BF16)], axis=0)


def _softmax_pv(parts, sink=None):
    m = parts[0][0].max(axis=0, keepdims=True)
    for s, _ in parts[1:]:
        m = jnp.maximum(m, s.max(axis=0, keepdims=True))
    l = 0.0
    if sink is not None:
        m = jnp.maximum(m, sink)
        l = jnp.exp2(sink - m)
    acc = None
    for s, vt in parts:
        p = jnp.exp2(s - m)
        l = l + p.sum(axis=0, keepdims=True)
        pv = _dot(vt, p.astype(BF16))
        acc = pv if acc is None else acc + pv
    return acc / l


def _pipelined_tiles(n_tiles, scores, finish):
    nxt = scores(0)
    for t in range(n_tiles):
        cur = nxt
        if t + 1 < n_tiles:
            nxt = scores(t + 1)
        finish(t, cur)


def _na_kernel(cls_ref, ws_ref, qt_ref, k_ref, vt_ref, kc_ref, vct_ref, *rest):
    del cls_ref
    bias_refs, o_ref = rest[:-1], rest[-1]
    nk, tq = NA_KEY_ROWS * GRID_W, NA_Q_ROWS * GRID_W
    start = lambda t: pl.multiple_of(ws_ref[pl.program_id(2) * len(bias_refs) + t] * GRID_W, tq)

    def scores(t):
        qt = qt_ref[0, :, t * tq:(t + 1) * tq]
        return _dot(k_ref[0, pl.ds(start(t), nk), :], qt), _dot(kc_ref[0], qt)

    def finish(t, s):
        o = _softmax_pv([(s[0] + bias_refs[t][0, 0], vt_ref[0, :, pl.ds(start(t), nk)]), (s[1], vct_ref[0])])
        o_ref[0, t * tq:(t + 1) * tq, :] = o.T.astype(o_ref.dtype)

    _pipelined_tiles(len(bias_refs), scores, finish)


def _na_tables(rpb, rows):
    nt = rows // NA_Q_ROWS
    R = np.arange(nt) * NA_Q_ROWS
    ws = np.clip(R - NA_WIN_R // 2, 0, rows - NA_KEY_ROWS)
    qr = (R[:, None] + np.arange(NA_Q_ROWS)[None, :])
    kr = ws[:, None] + np.arange(NA_KEY_ROWS)[None, :]
    r0 = np.clip(qr - NA_WIN_R // 2, 0, rows - NA_WIN_R)
    row_ok = (kr[:, None, :] >= r0[:, :, None]) & (kr[:, None, :] < r0[:, :, None] + NA_WIN_R)
    drow = np.clip(kr[:, None, :] - qr[:, :, None] + NA_WIN_R - 1, 0, 2 * NA_WIN_R - 2)
    geom = np.concatenate([row_ok.reshape(nt, -1).astype(np.int64), drow.reshape(nt, -1)], axis=1)
    _, first, cls = np.unique(geom, axis=0, return_index=True, return_inverse=True)
    cq = np.arange(GRID_W)
    c0 = np.clip(cq - NA_WIN_C // 2, 0, GRID_W - NA_WIN_C)
    col_ok = (cq[None, :] >= c0[:, None]) & (cq[None, :] < c0[:, None] + NA_WIN_C)
    dcol = np.clip(cq[None, :] - cq[:, None] + NA_WIN_C - 1, 0, 2 * NA_WIN_C - 2)
    nc, H = len(first), rpb.shape[0]
    cols = jnp.zeros(rpb.shape[:2] + dcol.shape, F32)
    for j in range(rpb.shape[2]):
        cols = jnp.where(jnp.asarray(dcol == j)[None, None], rpb[:, :, j][:, :, None, None] * LOG2E, cols)
    cols = jnp.where(jnp.asarray(col_ok)[None, None], cols, NEG)
    nd = rpb.shape[1]
    pad = jnp.full((H, NA_KEY_ROWS * GRID_W, GRID_W), NEG, F32)
    flat = jnp.concatenate([pad, cols.transpose(0, 1, 3, 2).reshape(H, nd * GRID_W, GRID_W), pad], axis=1)
    per_class = []
    for c in first:
        strips = []
        for q in range(NA_Q_ROWS):
            d0 = int(kr[c, 0] - qr[c, q]) + NA_WIN_R - 1 + NA_KEY_ROWS
            strip = flat[:, d0 * GRID_W:(d0 + NA_KEY_ROWS) * GRID_W, :]
            keep = np.repeat(row_ok[c, q], GRID_W)
            strips.append(jnp.where(jnp.asarray(keep)[None, :, None], strip, NEG))
        per_class.append(jnp.concatenate(strips, axis=-1))
    bias = jnp.stack(per_class)
    return bias, jnp.asarray(cls.reshape(-1), jnp.int32), jnp.asarray(ws, jnp.int32)


def _na_attention(qt, k, vt, kc, vct, rpb):
    B, _, S = qt.shape
    C = kc.shape[1]
    rows = S // GRID_W
    assert rows >= NA_KEY_ROWS and rows % NA_Q_ROWS == 0
    bias, cls, ws = _na_tables(rpb, rows)
    tq, nk, hd = NA_Q_ROWS * GRID_W, NA_KEY_ROWS * GRID_W, HEAD_DIM
    n_sub = NA_TILES_PER_STEP if (S // tq) % NA_TILES_PER_STEP == 0 else 1
    bias_spec = lambda t: pl.BlockSpec((1, 1, nk, tq), lambda b, h, i, c, w: (c[i * n_sub + t], h, 0, 0))
    grid_spec = pltpu.PrefetchScalarGridSpec(
        num_scalar_prefetch=2,
        grid=(B, NA_HEADS, S // (tq * n_sub)),
        in_specs=[
            pl.BlockSpec((1, hd, n_sub * tq), lambda b, h, i, c, w: (b, h, i)),
            pl.BlockSpec((1, S, hd), lambda b, h, i, c, w: (b, 0, h)),
            pl.BlockSpec((1, hd, S), lambda b, h, i, c, w: (b, h, 0)),
            pl.BlockSpec((1, C, hd), lambda b, h, i, c, w: (b, 0, h)),
            pl.BlockSpec((1, hd, C), lambda b, h, i, c, w: (b, h, 0)),
        ] + [bias_spec(t) for t in range(n_sub)],
        out_specs=pl.BlockSpec((1, n_sub * tq, hd), lambda b, h, i, c, w: (b, i, h)),
    )
    return pl.pallas_call(
        _na_kernel,
        out_shape=jax.ShapeDtypeStruct((B, S, NA_HEADS * hd), BF16),
        grid_spec=grid_spec,
        compiler_params=_cparams("parallel", "parallel", "arbitrary"),
        name="na_attn",
    )(cls, ws, qt, k, vt, kc, vct, *([bias] * n_sub))


def _swa_kernel(qt_ref, k_ref, vt_ref, kc_ref, vct_ref, sink_ref, o_ref, *, tq, n_sub, seq):
    nk = tq + 2 * SWA_WINDOW
    tile = lambda t: pl.program_id(2) * n_sub + t
    start = lambda t: pl.multiple_of(jnp.clip(tile(t) * tq - SWA_WINDOW, 0, seq - nk), SWA_WINDOW)

    def scores(t):
        qt = qt_ref[0, :, t * tq:(t + 1) * tq]
        return _dot(k_ref[0, pl.ds(start(t), nk), :], qt), _dot(kc_ref[0], qt)

    def finish(t, s):
        kpos = start(t) + lax.broadcasted_iota(jnp.int32, (nk, tq), 0)
        qpos = tile(t) * tq + lax.broadcasted_iota(jnp.int32, (nk, tq), 1)
        s_win = jnp.where(jnp.abs(qpos - kpos) <= SWA_WINDOW, s[0], NEG)
        o = _softmax_pv([(s_win, vt_ref[0, :, pl.ds(start(t), nk)]), (s[1], vct_ref[0])], sink=sink_ref[0])
        o_ref[0, t * tq:(t + 1) * tq, :] = o.T.astype(o_ref.dtype)

    _pipelined_tiles(n_sub, scores, finish)


def _swa_attention(qt, k, vt, kc, vct, sink, tq):
    B, _, S = qt.shape
    C = kc.shape[1]
    hd, G = HEAD_DIM, SWA_HEADS // SWA_KV_HEADS
    assert S >= tq + 2 * SWA_WINDOW and S % tq == 0 and tq % SWA_WINDOW == 0
    n_sub = SWA_TILES_PER_STEP if (S // tq) % SWA_TILES_PER_STEP == 0 else 1
    return pl.pallas_call(
        functools.partial(_swa_kernel, tq=tq, n_sub=n_sub, seq=S),
        out_shape=jax.ShapeDtypeStruct((B, S, SWA_HEADS * hd), BF16),
        grid=(B, SWA_HEADS, S // (tq * n_sub)),
        in_specs=[
            pl.BlockSpec((1, hd, n_sub * tq), lambda b, h, i: (b, h, i)),
            pl.BlockSpec((1, S, hd), lambda b, h, i: (b, 0, h // G)),
            pl.BlockSpec((1, hd, S), lambda b, h, i: (b, h // G, 0)),
            pl.BlockSpec((1, C, hd), lambda b, h, i: (b, 0, h // G)),
            pl.BlockSpec((1, hd, C), lambda b, h, i: (b, h // G, 0)),
            pl.BlockSpec((1, 1, 1), lambda b, h, i: (h, 0, 0)),
        ],
        out_specs=pl.BlockSpec((1, n_sub * tq, hd), lambda b, h, i: (b, i, h)),
        compiler_params=_cparams("parallel", "parallel", "arbitrary"),
        name="swa_attn",
    )(qt, k, vt, kc, vct, (sink * LOG2E).reshape(SWA_HEADS, 1, 1).astype(F32))


def _dense_kernel(*refs, tk, seq, n_ctx, has_sink):
    refs = list(refs)
    qt_ref, k_ref, vt_ref = refs[:3]
    o_ref = refs[-1]
    kc_ref, vct_ref = refs[3:5] if n_ctx else (None, None)
    qt = qt_ref[0]
    tq = qt.shape[1]
    dv = vt_ref.shape[1]
    if has_sink:
        m = jnp.broadcast_to(refs[-2][0], (1, tq))
        acc = jnp.concatenate([jnp.zeros((dv, tq), F32), jnp.ones((BF16_SUBLANES, tq), F32)], axis=0)
    else:
        m = jnp.full((1, tq), -jnp.inf, F32)
        acc = jnp.zeros((dv + BF16_SUBLANES, tq), F32)

    chunks = [(k_ref, vt_ref, j * tk, tk) for j in range(seq // tk)]
    if n_ctx:
        chunks.append((kc_ref, vct_ref, 0, n_ctx))
    scores = lambda c: _dot(c[0][0, c[2]:c[2] + c[3], :], qt)

    s_next = scores(chunks[0])
    for j, c in enumerate(chunks):
        s = s_next
        if j + 1 < len(chunks):
            s_next = scores(chunks[j + 1])
        m_new = jnp.maximum(m, s.max(axis=0, keepdims=True))
        alpha = jnp.exp2(m - m_new)
        p = jnp.exp2(s - m_new).astype(BF16)
        acc = alpha * acc + _dot(_with_ones_rows(c[1][0, :, c[2]:c[2] + c[3]]), p)
        m = m_new
    o_ref[0] = (acc[:dv] / acc[dv:dv + 1]).T.astype(o_ref.dtype)


def _dense_attention(qt, k, vt, kc, vct, n_heads, group, tq, tk, sink=None):
    B, _, S = qt.shape
    Sk = k.shape[1]
    dk = qt.shape[1] // n_heads
    dv = vt.shape[1] // (n_heads // group)
    assert S % tq == 0 and Sk % tk == 0
    n_ctx = 0 if kc is None else kc.shape[1]
    in_specs = [
        pl.BlockSpec((1, dk, tq), lambda b, h, i: (b, h, i)),
        pl.BlockSpec((1, Sk, dk), lambda b, h, i: (b, 0, h // group)),
        pl.BlockSpec((1, dv, Sk), lambda b, h, i: (b, h // group, 0)),
    ]
    args = [qt, k, vt]
    if n_ctx:
        in_specs += [
            pl.BlockSpec((1, n_ctx, dk), lambda b, h, i: (b, 0, h // group)),
            pl.BlockSpec((1, dv, n_ctx), lambda b, h, i: (b, h // group, 0)),
        ]
        args += [kc, vct]
    if sink is not None:
        in_specs.append(pl.BlockSpec((1, 1, 1), lambda b, h, i: (h, 0, 0)))
        args.append((sink * LOG2E).reshape(n_heads, 1, 1).astype(F32))
    return pl.pallas_call(
        functools.partial(_dense_kernel, tk=tk, seq=Sk, n_ctx=n_ctx, has_sink=sink is not None),
        out_shape=jax.ShapeDtypeStruct((B, S, n_heads * dv), BF16),
        grid=(B, n_heads, S // tq),
        in_specs=in_specs,
        out_specs=pl.BlockSpec((1, tq, dv), lambda b, h, i: (b, i, h)),
        compiler_params=_cparams("parallel", "parallel", "arbitrary"),
        name="dense_attn",
    )(*args)


def _out_kernel(oa_ref, ob_ref, oc_ref, od_ref, w_ref, x_ref, gate_ref, g_ref, b_ref, o_ref, *, alpha, sub):
    for r0 in range(0, x_ref.shape[1], sub):
        rows = slice(r0, r0 + sub)
        y = None
        o = 0
        for r in (oa_ref, ob_ref, oc_ref, od_ref):
            n = r.shape[2]
            t = _dot(r[0, rows, :], w_ref[o:o + n, :])
            y = t if y is None else y + t
            o += n
        z = alpha * x_ref[0, rows, :] + gate_ref[0] * y
        o_ref[0, rows, :] = _layer_norm(z) * g_ref[...] + b_ref[...]


def _out_proj(mix, x, gate, sw, layer, alpha, tm):
    B, S, D = x.shape
    const = lambda arr: _layer_spec(arr, layer, lambda bb, i: (0, 0), arr.shape[1:], pipeline_mode=pl.Buffered(1))
    return pl.pallas_call(
        functools.partial(_out_kernel, alpha=alpha, sub=min(OUT_SUB_ROWS, tm)),
        out_shape=jax.ShapeDtypeStruct((B, S, D), F32),
        grid=(B, S // tm),
        in_specs=[pl.BlockSpec((1, tm, m.shape[2]), lambda bb, i: (bb, i, 0)) for m in mix] + [
            const(sw["w_out"]),
            pl.BlockSpec((1, tm, D), lambda bb, i: (bb, i, 0)),
            pl.BlockSpec((1, 1, D), lambda bb, i: (bb, 0, 0)),
            const(sw["ln1_g"]), const(sw["ln1_b"]),
        ],
        out_specs=pl.BlockSpec((1, tm, D), lambda bb, i: (bb, i, 0)),
        compiler_params=_cparams("parallel", "parallel"),
        name="out_proj",
    )(*mix, sw["w_out"], x, gate, sw["ln1_g"], sw["ln1_b"])


HALO = BF16_SUBLANES


def _ffn_kernel(x_hbm, xp_ref, xn_ref, shift_ref, scale_ref, gate_ref, wg_ref, wu_ref, cw_ref, cb_ref, wd_ref,
                g_ref, b_ref, o_ref, h_ref, x_buf, x_sem, *, alpha, tm):
    bb, i, f = pl.program_id(0), pl.program_id(1), pl.program_id(2)
    n_b, n_i = pl.num_programs(0), pl.num_programs(1)
    acc_ref = o_ref.at[0]

    def x_copy(b_idx, i_idx):
        return pltpu.make_async_copy(x_hbm.at[b_idx, pl.ds(i_idx * tm, tm), :], x_buf, x_sem)

    @pl.when(f == 0)
    def _():
        @pl.when((bb == 0) & (i == 0))
        def _():
            x_copy(0, 0).start()

        x_copy(bb, i).wait()
        xv = x_buf[...]
        mod = lambda t: _layer_norm(t) * (1.0 + scale_ref[0]) + shift_ref[0]
        keep_p = (i > 0).astype(F32)
        keep_n = (i < n_i - 1).astype(F32)
        h_ref[0:HALO, :] = (mod(xp_ref[0]) * keep_p).astype(BF16)
        h_ref[HALO:HALO + tm, :] = mod(xv).astype(BF16)
        h_ref[HALO + tm:, :] = (mod(xn_ref[0]) * keep_n).astype(BF16)
        acc_ref[...] = alpha * xv

    @pl.when((f == 1) & ((bb < n_b - 1) | (i < n_i - 1)))
    def _():
        wrap = i == n_i - 1
        x_copy(jnp.where(wrap, bb + 1, bb), jnp.where(wrap, 0, i + 1)).start()

    n = tm + 2 * HALO
    gt = _dot(h_ref[...], wg_ref[...])
    up = _dot(h_ref[HALO:HALO + tm, :], wu_ref[...])
    a = (pltpu.roll(gt, 1, 0) * cw_ref[0:1, :] + gt * cw_ref[1:2, :]
         + pltpu.roll(gt, n - 1, 0) * cw_ref[2:3, :] + cb_ref[...])[HALO:HALO + tm]
    y = (a * jax.nn.sigmoid(a) * up).astype(BF16)
    acc_ref[...] += gate_ref[0] * _dot(y, wd_ref[...])

    @pl.when(f == pl.num_programs(2) - 1)
    def _():
        o_ref[0] = _layer_norm(acc_ref[...]) * g_ref[...] + b_ref[...]


def _ffn(x, shift, scale, gate, sw, layer, alpha, tm, tf):
    B, S, D = x.shape
    F = sw["ffn_w_gate"].shape[2]
    assert S % tm == 0 and F % tf == 0 and tm % HALO == 0 and F // tf >= 2
    nh = tm // HALO
    vec = pl.BlockSpec((1, 1, D), lambda bb, i, f: (bb, 0, 0))
    cols = lambda arr, rows: _layer_spec(arr, layer, lambda bb, i, f: (0, f), (rows, tf))
    const = lambda arr: _layer_spec(arr, layer, lambda bb, i, f: (0, 0), arr.shape[1:])
    return pl.pallas_call(
        functools.partial(_ffn_kernel, alpha=alpha, tm=tm),
        out_shape=jax.ShapeDtypeStruct((B, S, D), F32),
        grid=(B, S // tm, F // tf),
        in_specs=[
            pl.BlockSpec(memory_space=pl.ANY),
            pl.BlockSpec((1, HALO, D), lambda bb, i, f: (bb, jnp.maximum(i * nh - 1, 0), 0)),
            pl.BlockSpec((1, HALO, D), lambda bb, i, f: (bb, jnp.minimum((i + 1) * nh, S // HALO - 1), 0)),
            vec, vec, vec,
            cols(sw["ffn_w_gate"], D), cols(sw["ffn_w_up"], D), cols(sw["ffn_conv_w"], CONV_W),
            cols(sw["ffn_conv_b"], 1),
            _layer_spec(sw["ffn_w_down"], layer, lambda bb, i, f: (f, 0), (tf, D)),
            const(sw["ln2_g"]), const(sw["ln2_b"]),
        ],
        out_specs=pl.BlockSpec((1, tm, D), lambda bb, i, f: (bb, i, 0)),
        scratch_shapes=[pltpu.VMEM((tm + 2 * HALO, D), BF16), pltpu.VMEM((tm, D), F32), pltpu.SemaphoreType.DMA(())],
        compiler_params=_cparams("arbitrary", "arbitrary", "arbitrary"),
        name="conv_ffn",
    )(x, x, x, shift, scale, gate, sw["ffn_w_gate"], sw["ffn_w_up"], sw["ffn_conv_w"], sw["ffn_conv_b"],
      sw["ffn_w_down"], sw["ln2_g"], sw["ln2_b"])


def _rope_tables(S):
    t = jnp.arange(S)
    row, col = (t // GRID_W).astype(F32), (t % GRID_W).astype(F32)
    lane = np.arange(LANE)

    def tables(dim):
        half = dim // 4
        inv_freq = ROPE_THETA ** (-jnp.arange(half, dtype=F32) / half)
        used = lane < dim
        pos = jnp.where(jnp.asarray((lane % dim) < dim // 2)[None, :], row[:, None], col[:, None])
        ang = pos * inv_freq[jnp.asarray(lane % half)][None, :]
        cos, sin = jnp.cos(ang), jnp.sin(ang)
        lo = jnp.asarray(used & ((lane % (2 * half)) < half))[None, :]
        hi = jnp.asarray(used & ((lane % (2 * half)) >= half))[None, :]
        return [jnp.where(jnp.asarray(used)[None, :], cos, 1.0), jnp.where(lo, -sin, 0.0), jnp.where(hi, sin, 0.0)]

    return jnp.stack(tables(HEAD_DIM) + tables(MLA_ROPE)).astype(F32)


def _identity_rope(C):
    one, zero = jnp.ones((C, LANE), F32), jnp.zeros((C, LANE), F32)
    return jnp.stack([one, zero, zero, one, zero, zero])


def _pad_w_uq(w):
    L, r, _ = w.shape
    w = w.reshape(L, r, MLA_HEADS, MLA_NOPE + MLA_ROPE)
    w = jnp.concatenate([w, jnp.zeros((L, r, MLA_HEADS, MLA_QK - MLA_NOPE - MLA_ROPE), w.dtype)], axis=3)
    return w.reshape(L, r, MLA_HEADS * MLA_QK).astype(BF16)


def kernel(x, c, ctx, c_ctx, w_ada, b_ada, w_in, na_rpb, swa_sink, mla_q_norm, mla_kv_norm, mla_w_uq, mla_w_ukv,
           gqa_q_norm, gqa_k_norm, w_out, ln1_g, ln1_b, ffn_w_gate, ffn_w_up, ffn_conv_w, ffn_conv_b, ffn_w_down,
           ln2_g, ln2_b):
    B, S, D = x.shape
    C = ctx.shape[1]
    depth = w_ada.shape[0]
    alpha = (2 * depth) ** 0.25

    cvec = jnp.concatenate([c, c_ctx[None, :], jnp.zeros((8 - B - 1, D), F32)], axis=0)
    ada = _ada(cvec, w_ada, b_ada)
    rope_x, rope_c = _rope_tables(S), _identity_rope(C)

    row = lambda a: a[:, None, :]
    sw = {
        "w_in_abc": w_in[..., :_ABC_W].astype(BF16), "w_in_d": w_in[..., _ABC_W:].astype(BF16),
        "w_uq": _pad_w_uq(mla_w_uq), "w_ukv": mla_w_ukv.astype(BF16),
        "mla_q_norm": row(mla_q_norm), "mla_kv_norm": row(mla_kv_norm),
        "gqa_q_norm": row(gqa_q_norm), "gqa_k_norm": row(gqa_k_norm),
        "w_out": w_out.astype(BF16), "ln1_g": row(ln1_g), "ln1_b": row(ln1_b),
        "ffn_w_gate": ffn_w_gate.astype(BF16), "ffn_w_up": ffn_w_up.astype(BF16), "ffn_conv_w": ffn_conv_w,
        "ffn_conv_b": row(ffn_conv_b), "ffn_w_down": ffn_w_down.astype(BF16), "ln2_g": row(ln2_g), "ln2_b": row(ln2_b),
    }

    tm_proj = min(256, S)
    tm_out = min(512, S)
    tm_ffn, tf = min(1024, S), 512
    tq_dense, tk_dense = min(1024, S), min(512, S)
    tq_swa = min(256, S - 2 * SWA_WINDOW)
    g_swa, g_gqa = SWA_HEADS // SWA_KV_HEADS, GQA_HEADS // GQA_KV_HEADS

    for i in range(depth):
        need_ctx = i < depth - 1
        mx = ada[i, :B].reshape(B, 1, 6, D)
        mod_x = [mx[:, :, j] for j in range(6)]
        mod_c = [jnp.broadcast_to(ada[i, B].reshape(1, 1, 6, D)[:, :, j], (B, 1, D)) for j in range(6)]

        qa, ka, va, qb, kb, vb, qc, kc, vc, qd, kd, vd = _proj(x, mod_x[0], mod_x[1], rope_x, sw, i, tm_proj)
        qa_c, ka_c, va_c, qb_c, kb_c, vb_c, qc_c, kc_c, vc_c, qd_c, kd_c, vd_c = _proj(
            ctx, mod_c[0], mod_c[1], rope_c, sw, i, C)

        mix_x = (
            _na_attention(qa, ka, va, ka_c, va_c, na_rpb[i]),
            _swa_attention(qb, kb, vb, kb_c, vb_c, swa_sink[i], tq_swa),
            _dense_attention(qc, kc, vc, kc_c, vc_c, MLA_HEADS, 1, tq_dense, tk_dense),
            _dense_attention(qd, kd, vd, kd_c, vd_c, GQA_HEADS, g_gqa, tq_dense, tk_dense),
        )
        x = _out_proj(mix_x, x, mod_x[2], sw, i, alpha, tm_out)
        if need_ctx:
            mix_c = (
                _dense_attention(qa_c, ka_c, va_c, None, None, NA_HEADS, 1, C, C),
                _dense_attention(qb_c, kb_c, vb_c, None, None, SWA_HEADS, g_swa, C, C, sink=swa_sink[i]),
                _dense_attention(qc_c, kc_c, vc_c, None, None, MLA_HEADS, 1, C, C),
                _dense_attention(qd_c, kd_c, vd_c, None, None, GQA_HEADS, g_gqa, C, C),
            )
            ctx = _out_proj(mix_c, ctx, mod_c[2], sw, i, alpha, C)

        x = _ffn(x, mod_x[3], mod_x[4], mod_x[5], sw, i, alpha, tm_ffn, tf)
        if need_ctx:
            ctx = _ffn(ctx, mod_c[3], mod_c[4], mod_c[5], sw, i, alpha, C, tf)
    return x
```

```python
import functools
import math

import numpy as np
import jax
import jax.numpy as jnp
from jax import lax
from jax.experimental import pallas as pl
from jax.experimental.pallas import tpu as pltpu

GRID_W = 64
HEAD_DIM = 128
NA_HEADS = 4
NA_WIN_R = 8
NA_WIN_C = 16
SWA_HEADS = 4
SWA_KV_HEADS = 2
SWA_WINDOW = 128
MLA_HEADS = 4
MLA_Q_LORA = 384
MLA_KV_LORA = 128
MLA_NOPE = 128
MLA_ROPE = 64
MLA_V = 128
GQA_HEADS = 4
GQA_KV_HEADS = 2
CONV_W = 3
ROPE_THETA = 10000.0
EPS = 1e-6
NEG = -1e30
LOG2E = math.log2(math.e)

LANE = 128
BF16_SUBLANES = 16
VMEM_LIMIT = 56 * 1024 * 1024

F32 = jnp.float32
BF16 = jnp.bfloat16

OUT_SUB_ROWS = 256
NA_Q_ROWS = 4
NA_KEY_ROWS = NA_WIN_R + NA_Q_ROWS
NA_TILES_PER_STEP = 8
SWA_TILES_PER_STEP = 8
NA_LOOKAHEAD, SWA_LOOKAHEAD = 1, 2
DENSE_LOOKAHEAD = 2


def _cparams(*sem):
    return pltpu.CompilerParams(dimension_semantics=sem, vmem_limit_bytes=VMEM_LIMIT)


def _dot(a, b):
    return jnp.dot(a, b, preferred_element_type=F32)


def _layer_norm(x):
    mu = jnp.mean(x, axis=-1, keepdims=True)
    xc = x - mu
    var = jnp.mean(xc * xc, axis=-1, keepdims=True)
    return xc * lax.rsqrt(var + EPS)


def _rms_norm(x, g):
    return x * lax.rsqrt(jnp.mean(x * x, axis=-1, keepdims=True) + EPS) * g


def _rope(x, cos, sin_lo, sin_hi, half):
    n = x.shape[-1]
    return x * cos + pltpu.roll(x, n - half, 1) * sin_lo + pltpu.roll(x, half, 1) * sin_hi


def _layer_spec(arr, layer, index_map_rest, block_rest, **kw):
    return pl.BlockSpec((None,) + tuple(block_rest), lambda *g: (layer,) + tuple(index_map_rest(*g)), **kw)


def _ada_kernel(c_ref, w_ref, b_ref, o_ref):
    c = c_ref[...]
    a = (c * jax.nn.sigmoid(c)).astype(BF16)
    o_ref[0] = _dot(a, w_ref[0].astype(BF16)) + b_ref[0]


def _ada(cvec, w_ada, b_ada):
    L, D, N = w_ada.shape
    M = cvec.shape[0]
    tn = 1024
    return pl.pallas_call(
        _ada_kernel,
        out_shape=jax.ShapeDtypeStruct((L, M, N), F32),
        grid=(L, N // tn),
        in_specs=[
            pl.BlockSpec((M, D), lambda l, j: (0, 0)),
            pl.BlockSpec((1, D, tn), lambda l, j: (l, 0, j)),
            pl.BlockSpec((1, 1, tn), lambda l, j: (l, 0, j)),
        ],
        out_specs=pl.BlockSpec((1, M, tn), lambda l, j: (l, 0, j)),
        compiler_params=_cparams("parallel", "parallel"),
        name="ada",
    )(cvec, w_ada, b_ada.reshape(L, 1, N))


_A0 = 0
_B0 = _A0 + 3 * NA_HEADS * HEAD_DIM
_C0 = _B0 + (SWA_HEADS + 2 * SWA_KV_HEADS) * HEAD_DIM
_KPE0 = _C0 + MLA_Q_LORA + MLA_KV_LORA
_ABC_W = _KPE0 + MLA_ROPE
_D_W = (GQA_HEADS + 2 * GQA_KV_HEADS) * HEAD_DIM
MLA_QK = 2 * LANE


def _proj_kernel(x_ref, shift_ref, scale_ref, w_ref, wd_ref, rope_ref, gq_lora_ref, gkv_lora_ref, gq_ref, gk_ref,
                 wuq_ref, wukv_ref,
                 qa_ref, ka_ref, va_ref, qb_ref, kb_ref, vb_ref, qc_ref, kc_ref, vc_ref, qd_ref, kd_ref, vd_ref):
    h = (_layer_norm(x_ref[0]) * (1.0 + scale_ref[0]) + shift_ref[0]).astype(BF16)
    cos2, slo2, shi2 = rope_ref[0], rope_ref[1], rope_ref[2]
    cos1, slo1, shi1 = rope_ref[3], rope_ref[4], rope_ref[5]
    rope2 = lambda t: _rope(t, cos2, slo2, shi2, HEAD_DIM // 4)
    rope1 = lambda t: _rope(t, cos1, slo1, shi1, MLA_ROPE // 4)
    hd = HEAD_DIM
    sc = hd ** -0.5 * LOG2E

    def store_t(ref, lo, t):
        ref[0, lo:lo + t.shape[1], :] = t.T.astype(BF16)


    wq, wk = GQA_HEADS * hd, GQA_KV_HEADS * hd
    pd = _dot(h, wd_ref[...])
    for i in range(GQA_HEADS):
        store_t(qd_ref, i * hd, rope2(_rms_norm(pd[:, i * hd:(i + 1) * hd], gq_ref[...])) * sc)
    for i in range(GQA_KV_HEADS):
        t = _rms_norm(pd[:, wq + i * hd:wq + (i + 1) * hd], gk_ref[...])
        kd_ref[0, :, i * hd:(i + 1) * hd] = rope2(t).astype(BF16)
    store_t(vd_ref, 0, pd[:, wq + wk:])

    pc = _dot(h, w_ref[:, _C0:_KPE0])
    cq = _rms_norm(pc[:, :MLA_Q_LORA], gq_lora_ref[...]).astype(BF16)
    ckv = _rms_norm(pc[:, MLA_Q_LORA:], gkv_lora_ref[...]).astype(BF16)
    kpe = _dot(h, w_ref[:, _KPE0:_ABC_W])
    kpe = rope1(jnp.concatenate([kpe, jnp.zeros((kpe.shape[0], LANE - MLA_ROPE), F32)], axis=1)).astype(BF16)
    qup = _dot(cq, wuq_ref[...])
    kvup = _dot(ckv, wukv_ref[...])
    sc_mla = (MLA_NOPE + MLA_ROPE) ** -0.5 * LOG2E
    for i in range(MLA_HEADS):
        o = i * MLA_QK
        store_t(qc_ref, o, qup[:, o:o + LANE] * sc_mla)
        store_t(qc_ref, o + LANE, rope1(qup[:, o + LANE:o + 2 * LANE]) * sc_mla)
        kc_ref[0, :, o:o + LANE] = kvup[:, o:o + LANE].astype(BF16)
        kc_ref[0, :, o + LANE:o + 2 * LANE] = kpe
        store_t(vc_ref, i * MLA_V, kvup[:, o + LANE:o + 2 * LANE])

    wq, wk = SWA_HEADS * hd, SWA_KV_HEADS * hd
    pb = _dot(h, w_ref[:, _B0:_B0 + wq + 2 * wk])
    for i in range(SWA_HEADS):
        store_t(qb_ref, i * hd, rope2(pb[:, i * hd:(i + 1) * hd]) * sc)
    for i in range(SWA_KV_HEADS):
        kb_ref[0, :, i * hd:(i + 1) * hd] = rope2(pb[:, wq + i * hd:wq + (i + 1) * hd]).astype(BF16)
    store_t(vb_ref, 0, pb[:, wq + wk:])

    w = NA_HEADS * hd
    pa = _dot(h, w_ref[:, _A0:_A0 + 3 * w])
    store_t(qa_ref, 0, pa[:, :w] * sc)
    store_t(va_ref, 0, pa[:, 2 * w:])
    ka_ref[0] = pa[:, w:2 * w].astype(BF16)


def _proj(x, shift, scale, rope_tab, sw, layer, tm):
    B, S, D = x.shape
    widths = (NA_HEADS * HEAD_DIM,) * 3 + (SWA_HEADS * HEAD_DIM, SWA_KV_HEADS * HEAD_DIM, SWA_KV_HEADS * HEAD_DIM) \
        + (MLA_HEADS * MLA_QK, MLA_HEADS * MLA_QK, MLA_HEADS * MLA_V) \
        + (GQA_HEADS * HEAD_DIM, GQA_KV_HEADS * HEAD_DIM, GQA_KV_HEADS * HEAD_DIM)
    transposed = (True, False, True) * 4
    const = lambda arr: _layer_spec(arr, layer, lambda b, i: (0, 0), arr.shape[1:], pipeline_mode=pl.Buffered(1))
    vec = pl.BlockSpec((1, 1, D), lambda b, i: (b, 0, 0))
    out_spec = lambda n, t: (pl.BlockSpec((1, n, tm), lambda b, i: (b, 0, i)) if t
                             else pl.BlockSpec((1, tm, n), lambda b, i: (b, i, 0)))
    params = (sw["w_in_abc"], sw["w_in_d"], None, sw["mla_q_norm"], sw["mla_kv_norm"], sw["gqa_q_norm"],
              sw["gqa_k_norm"], sw["w_uq"], sw["w_ukv"])
    in_specs = [pl.BlockSpec((1, tm, D), lambda b, i: (b, i, 0)), vec, vec]
    in_specs += [pl.BlockSpec((6, tm, LANE), lambda b, i: (0, i, 0)) if p is None else const(p) for p in params]
    args = [rope_tab if p is None else p for p in params]
    return pl.pallas_call(
        _proj_kernel,
        out_shape=[jax.ShapeDtypeStruct((B, n, S) if t else (B, S, n), BF16) for n, t in zip(widths, transposed)],
        grid=(B, S // tm),
        in_specs=in_specs,
        out_specs=[out_spec(n, t) for n, t in zip(widths, transposed)],
        compiler_params=_cparams("parallel", "parallel"),
        name="proj",
    )(x, shift, scale, *args)


def _with_ones_rows(vt):
    return jnp.concatenate([vt, jnp.ones((BF16_SUBLANES, vt.shape[1]), BF16)], axis=0)


def _softmax_pv(parts, sink=None):
    m = parts[0][0].max(axis=0, keepdims=True)
    for s, _ in parts[1:]:
        m = jnp.maximum(m, s.max(axis=0, keepdims=True))
    l = 0.0
    if sink is not None:
        m = jnp.maximum(m, sink)
        l = jnp.exp2(sink - m)
    acc = None
    for s, vt in parts:
        p = jnp.exp2(s - m)
        l = l + p.sum(axis=0, keepdims=True)
        pv = _dot(vt, p.astype(BF16))
        acc = pv if acc is None else acc + pv
    return acc / l


def _pipelined_tiles(n_tiles, scores, finish, lookahead):
    pending = [scores(t) for t in range(min(lookahead, n_tiles))]
    for t in range(n_tiles):
        if t + lookahead < n_tiles:
            pending.append(scores(t + lookahead))
        finish(t, pending.pop(0))


def _na_kernel(cls_ref, ws_ref, qt_ref, k_ref, vt_ref, kc_ref, vct_ref, *rest):
    del cls_ref
    bias_refs, o_ref = rest[:-1], rest[-1]
    nk, tq = NA_KEY_ROWS * GRID_W, NA_Q_ROWS * GRID_W
    start = lambda t: pl.multiple_of(ws_ref[pl.program_id(2) * len(bias_refs) + t] * GRID_W, tq)

    def scores(t):
        qt = qt_ref[0, :, t * tq:(t + 1) * tq]
        return _dot(k_ref[0, pl.ds(start(t), nk), :], qt), _dot(kc_ref[0], qt)

    def finish(t, s):
        o = _softmax_pv([(s[0] + bias_refs[t][0, 0], vt_ref[0, :, pl.ds(start(t), nk)]), (s[1], vct_ref[0])])
        o_ref[0, t * tq:(t + 1) * tq, :] = o.T.astype(o_ref.dtype)

    _pipelined_tiles(len(bias_refs), scores, finish, NA_LOOKAHEAD)


def _na_tables(rpb, rows):
    nt = rows // NA_Q_ROWS
    R = np.arange(nt) * NA_Q_ROWS
    ws = np.clip(R - NA_WIN_R // 2, 0, rows - NA_KEY_ROWS)
    qr = (R[:, None] + np.arange(NA_Q_ROWS)[None, :])
    kr = ws[:, None] + np.arange(NA_KEY_ROWS)[None, :]
    r0 = np.clip(qr - NA_WIN_R // 2, 0, rows - NA_WIN_R)
    row_ok = (kr[:, None, :] >= r0[:, :, None]) & (kr[:, None, :] < r0[:, :, None] + NA_WIN_R)
    drow = np.clip(kr[:, None, :] - qr[:, :, None] + NA_WIN_R - 1, 0, 2 * NA_WIN_R - 2)
    geom = np.concatenate([row_ok.reshape(nt, -1).astype(np.int64), drow.reshape(nt, -1)], axis=1)
    _, first, cls = np.unique(geom, axis=0, return_index=True, return_inverse=True)
    cq = np.arange(GRID_W)
    c0 = np.clip(cq - NA_WIN_C // 2, 0, GRID_W - NA_WIN_C)
    col_ok = (cq[None, :] >= c0[:, None]) & (cq[None, :] < c0[:, None] + NA_WIN_C)
    dcol = np.clip(cq[None, :] - cq[:, None] + NA_WIN_C - 1, 0, 2 * NA_WIN_C - 2)
    nc, H = len(first), rpb.shape[0]
    cols = jnp.zeros(rpb.shape[:2] + dcol.shape, F32)
    for j in range(rpb.shape[2]):
        cols = jnp.where(jnp.asarray(dcol == j)[None, None], rpb[:, :, j][:, :, None, None] * LOG2E, cols)
    cols = jnp.where(jnp.asarray(col_ok)[None, None], cols, NEG)
    nd = rpb.shape[1]
    pad = jnp.full((H, NA_KEY_ROWS * GRID_W, GRID_W), NEG, F32)
    flat = jnp.concatenate([pad, cols.transpose(0, 1, 3, 2).reshape(H, nd * GRID_W, GRID_W), pad], axis=1)
    per_class = []
    for c in first:
        strips = []
        for q in range(NA_Q_ROWS):
            d0 = int(kr[c, 0] - qr[c, q]) + NA_WIN_R - 1 + NA_KEY_ROWS
            strip = flat[:, d0 * GRID_W:(d0 + NA_KEY_ROWS) * GRID_W, :]
            keep = np.repeat(row_ok[c, q], GRID_W)
            strips.append(jnp.where(jnp.asarray(keep)[None, :, None], strip, NEG))
        per_class.append(jnp.concatenate(strips, axis=-1))
    bias = jnp.stack(per_class)
    return bias, jnp.asarray(cls.reshape(-1), jnp.int32), jnp.asarray(ws, jnp.int32)


def _na_attention(qt, k, vt, kc, vct, rpb):
    B, _, S = qt.shape
    C = kc.shape[1]
    rows = S // GRID_W
    assert rows >= NA_KEY_ROWS and rows % NA_Q_ROWS == 0
    bias, cls, ws = _na_tables(rpb, rows)
    tq, nk, hd = NA_Q_ROWS * GRID_W, NA_KEY_ROWS * GRID_W, HEAD_DIM
    n_sub = NA_TILES_PER_STEP if (S // tq) % NA_TILES_PER_STEP == 0 else 1
    bias_spec = lambda t: pl.BlockSpec((1, 1, nk, tq), lambda b, h, i, c, w: (c[i * n_sub + t], h, 0, 0))
    grid_spec = pltpu.PrefetchScalarGridSpec(
        num_scalar_prefetch=2,
        grid=(B, NA_HEADS, S // (tq * n_sub)),
        in_specs=[
            pl.BlockSpec((1, hd, n_sub * tq), lambda b, h, i, c, w: (b, h, i)),
            pl.BlockSpec((1, S, hd), lambda b, h, i, c, w: (b, 0, h)),
            pl.BlockSpec((1, hd, S), lambda b, h, i, c, w: (b, h, 0)),
            pl.BlockSpec((1, C, hd), lambda b, h, i, c, w: (b, 0, h)),
            pl.BlockSpec((1, hd, C), lambda b, h, i, c, w: (b, h, 0)),
        ] + [bias_spec(t) for t in range(n_sub)],
        out_specs=pl.BlockSpec((1, n_sub * tq, hd), lambda b, h, i, c, w: (b, i, h)),
    )
    return pl.pallas_call(
        _na_kernel,
        out_shape=jax.ShapeDtypeStruct((B, S, NA_HEADS * hd), BF16),
        grid_spec=grid_spec,
        compiler_params=_cparams("parallel", "parallel", "arbitrary"),
        name="na_attn",
    )(cls, ws, qt, k, vt, kc, vct, *([bias] * n_sub))


def _swa_kernel(qt_ref, k_ref, vt_ref, kc_ref, vct_ref, sink_ref, o_ref, *, tq, n_sub, seq):
    nk = tq + 2 * SWA_WINDOW
    tile = lambda t: pl.program_id(2) * n_sub + t
    start = lambda t: pl.multiple_of(jnp.clip(tile(t) * tq - SWA_WINDOW, 0, seq - nk), SWA_WINDOW)

    def scores(t):
        qt = qt_ref[0, :, t * tq:(t + 1) * tq]
        return _dot(k_ref[0, pl.ds(start(t), nk), :], qt), _dot(kc_ref[0], qt)

    def finish(t, s):
        kpos = start(t) + lax.broadcasted_iota(jnp.int32, (nk, tq), 0)
        qpos = tile(t) * tq + lax.broadcasted_iota(jnp.int32, (nk, tq), 1)
        s_win = jnp.where(jnp.abs(qpos - kpos) <= SWA_WINDOW, s[0], NEG)
        o = _softmax_pv([(s_win, vt_ref[0, :, pl.ds(start(t), nk)]), (s[1], vct_ref[0])], sink=sink_ref[0])
        o_ref[0, t * tq:(t + 1) * tq, :] = o.T.astype(o_ref.dtype)

    _pipelined_tiles(n_sub, scores, finish, SWA_LOOKAHEAD)


def _swa_attention(qt, k, vt, kc, vct, sink, tq):
    B, _, S = qt.shape
    C = kc.shape[1]
    hd, G = HEAD_DIM, SWA_HEADS // SWA_KV_HEADS
    assert S >= tq + 2 * SWA_WINDOW and S % tq == 0 and tq % SWA_WINDOW == 0
    n_sub = SWA_TILES_PER_STEP if (S // tq) % SWA_TILES_PER_STEP == 0 else 1
    return pl.pallas_call(
        functools.partial(_swa_kernel, tq=tq, n_sub=n_sub, seq=S),
        out_shape=jax.ShapeDtypeStruct((B, S, SWA_HEADS * hd), BF16),
        grid=(B, SWA_HEADS, S // (tq * n_sub)),
        in_specs=[
            pl.BlockSpec((1, hd, n_sub * tq), lambda b, h, i: (b, h, i)),
            pl.BlockSpec((1, S, hd), lambda b, h, i: (b, 0, h // G)),
            pl.BlockSpec((1, hd, S), lambda b, h, i: (b, h // G, 0)),
            pl.BlockSpec((1, C, hd), lambda b, h, i: (b, 0, h // G)),
            pl.BlockSpec((1, hd, C), lambda b, h, i: (b, h // G, 0)),
            pl.BlockSpec((1, 1, 1), lambda b, h, i: (h, 0, 0)),
        ],
        out_specs=pl.BlockSpec((1, n_sub * tq, hd), lambda b, h, i: (b, i, h)),
        compiler_params=_cparams("parallel", "parallel", "arbitrary"),
        name="swa_attn",
    )(qt, k, vt, kc, vct, (sink * LOG2E).reshape(SWA_HEADS, 1, 1).astype(F32))


def _dense_kernel(*refs, tk, seq, n_ctx, has_sink):
    refs = list(refs)
    qt_ref, k_ref, vt_ref = refs[:3]
    o_ref = refs[-1]
    kc_ref, vct_ref = refs[3:5] if n_ctx else (None, None)
    qt = qt_ref[0]
    tq = qt.shape[1]
    dv = vt_ref.shape[1]
    if has_sink:
        m = jnp.broadcast_to(refs[-2][0], (1, tq))
        acc = jnp.concatenate([jnp.zeros((dv, tq), F32), jnp.ones((BF16_SUBLANES, tq), F32)], axis=0)
    else:
        m = jnp.full((1, tq), -jnp.inf, F32)
        acc = jnp.zeros((dv + BF16_SUBLANES, tq), F32)

    chunks = [(k_ref, vt_ref, j * tk, tk) for j in range(seq // tk)]
    if n_ctx:
        chunks.append((kc_ref, vct_ref, 0, n_ctx))
    scores = lambda c: _dot(c[0][0, c[2]:c[2] + c[3], :], qt)

    pending = [scores(c) for c in chunks[:DENSE_LOOKAHEAD]]
    for j, c in enumerate(chunks):
        if j + DENSE_LOOKAHEAD < len(chunks):
            pending.append(scores(chunks[j + DENSE_LOOKAHEAD]))
        s = pending.pop(0)
        m_new = jnp.maximum(m, s.max(axis=0, keepdims=True))
        alpha = jnp.exp2(m - m_new)
        p = jnp.exp2(s - m_new).astype(BF16)
        acc = alpha * acc + _dot(_with_ones_rows(c[1][0, :, c[2]:c[2] + c[3]]), p)
        m = m_new
    o_ref[0] = (acc[:dv] / acc[dv:dv + 1]).T.astype(o_ref.dtype)


def _dense_attention(qt, k, vt, kc, vct, n_heads, group, tq, tk, sink=None):
    B, _, S = qt.shape
    Sk = k.shape[1]
    dk = qt.shape[1] // n_heads
    dv = vt.shape[1] // (n_heads // group)
    assert S % tq == 0 and Sk % tk == 0
    n_ctx = 0 if kc is None else kc.shape[1]
    in_specs = [
        pl.BlockSpec((1, dk, tq), lambda b, h, i: (b, h, i)),
        pl.BlockSpec((1, Sk, dk), lambda b, h, i: (b, 0, h // group)),
        pl.BlockSpec((1, dv, Sk), lambda b, h, i: (b, h // group, 0)),
    ]
    args = [qt, k, vt]
    if n_ctx:
        in_specs += [
            pl.BlockSpec((1, n_ctx, dk), lambda b, h, i: (b, 0, h // group)),
            pl.BlockSpec((1, dv, n_ctx), lambda b, h, i: (b, h // group, 0)),
        ]
        args += [kc, vct]
    if sink is not None:
        in_specs.append(pl.BlockSpec((1, 1, 1), lambda b, h, i: (h, 0, 0)))
        args.append((sink * LOG2E).reshape(n_heads, 1, 1).astype(F32))
    return pl.pallas_call(
        functools.partial(_dense_kernel, tk=tk, seq=Sk, n_ctx=n_ctx, has_sink=sink is not None),
        out_shape=jax.ShapeDtypeStruct((B, S, n_heads * dv), BF16),
        grid=(B, n_heads, S // tq),
        in_specs=in_specs,
        out_specs=pl.BlockSpec((1, tq, dv), lambda b, h, i: (b, i, h)),
        compiler_params=_cparams("parallel", "parallel", "arbitrary"),
        name="dense_attn",
    )(*args)


def _out_kernel(oa_ref, ob_ref, oc_ref, od_ref, w_ref, x_ref, gate_ref, g_ref, b_ref, o_ref, *, alpha, sub):
    for r0 in range(0, x_ref.shape[1], sub):
        rows = slice(r0, r0 + sub)
        y = None
        o = 0
        for r in (oa_ref, ob_ref, oc_ref, od_ref):
            n = r.shape[2]
            t = _dot(r[0, rows, :], w_ref[o:o + n, :])
            y = t if y is None else y + t
            o += n
        z = alpha * x_ref[0, rows, :] + gate_ref[0] * y
        o_ref[0, rows, :] = _layer_norm(z) * g_ref[...] + b_ref[...]


def _out_proj(mix, x, gate, sw, layer, alpha, tm):
    B, S, D = x.shape
    const = lambda arr: _layer_spec(arr, layer, lambda bb, i: (0, 0), arr.shape[1:], pipeline_mode=pl.Buffered(1))
    return pl.pallas_call(
        functools.partial(_out_kernel, alpha=alpha, sub=min(OUT_SUB_ROWS, tm)),
        out_shape=jax.ShapeDtypeStruct((B, S, D), F32),
        grid=(B, S // tm),
        in_specs=[pl.BlockSpec((1, tm, m.shape[2]), lambda bb, i: (bb, i, 0)) for m in mix] + [
            const(sw["w_out"]),
            pl.BlockSpec((1, tm, D), lambda bb, i: (bb, i, 0)),
            pl.BlockSpec((1, 1, D), lambda bb, i: (bb, 0, 0)),
            const(sw["ln1_g"]), const(sw["ln1_b"]),
        ],
        out_specs=pl.BlockSpec((1, tm, D), lambda bb, i: (bb, i, 0)),
        compiler_params=_cparams("parallel", "parallel"),
        name="out_proj",
    )(*mix, sw["w_out"], x, gate, sw["ln1_g"], sw["ln1_b"])


HALO = BF16_SUBLANES


def _ffn_kernel(x_hbm, xp_ref, xn_ref, shift_ref, scale_ref, gate_ref, wg_ref, wu_ref, cw_ref, cb_ref, wd_ref,
                g_ref, b_ref, o_ref, h_ref, x_buf, x_sem, *, alpha, tm, seg):
    bb, i, f = pl.program_id(0), pl.program_id(1), pl.program_id(2)
    n_b, n_i = pl.num_programs(0), pl.num_programs(1)
    acc_ref = o_ref.at[0]

    def x_copy(b_idx, i_idx):
        return pltpu.make_async_copy(x_hbm.at[b_idx, pl.ds(i_idx * tm, tm), :], x_buf, x_sem)

    @pl.when(f == 0)
    def _():
        @pl.when((bb == 0) & (i == 0))
        def _():
            x_copy(0, 0).start()

        x_copy(bb, i).wait()
        xv = x_buf[...]
        mod = lambda t: _layer_norm(t) * (1.0 + scale_ref[0]) + shift_ref[0]
        keep_p = ((i * tm) % seg != 0).astype(F32)
        keep_n = (((i + 1) * tm) % seg != 0).astype(F32)
        h_ref[0:HALO, :] = (mod(xp_ref[0]) * keep_p).astype(BF16)
        h_ref[HALO:HALO + tm, :] = mod(xv).astype(BF16)
        h_ref[HALO + tm:, :] = (mod(xn_ref[0]) * keep_n).astype(BF16)
        acc_ref[...] = alpha * xv

    @pl.when((f == 1) & ((bb < n_b - 1) | (i < n_i - 1)))
    def _():
        wrap = i == n_i - 1
        x_copy(jnp.where(wrap, bb + 1, bb), jnp.where(wrap, 0, i + 1)).start()

    n = tm + 2 * HALO
    gt = _dot(h_ref[...], wg_ref[...])
    up = _dot(h_ref[HALO:HALO + tm, :], wu_ref[...])
    prev, nxt = pltpu.roll(gt, 1, 0), pltpu.roll(gt, n - 1, 0)
    if seg < tm:
        pos = (lax.broadcasted_iota(jnp.int32, (n, 1), 0) + (seg - HALO)) % seg
        prev = jnp.where(pos == 0, 0.0, prev)
        nxt = jnp.where(pos == seg - 1, 0.0, nxt)
    a = (prev * cw_ref[0:1, :] + gt * cw_ref[1:2, :] + nxt * cw_ref[2:3, :] + cb_ref[...])[HALO:HALO + tm]
    y = (a * jax.nn.sigmoid(a) * up).astype(BF16)
    acc_ref[...] += gate_ref[0] * _dot(y, wd_ref[...])

    @pl.when(f == pl.num_programs(2) - 1)
    def _():
        o_ref[0] = _layer_norm(acc_ref[...]) * g_ref[...] + b_ref[...]


def _ffn(x, shift, scale, gate, sw, layer, alpha, tm, tf, seg=None):
    B, S, D = x.shape
    seg = S if seg is None else seg
    F = sw["ffn_w_gate"].shape[2]
    assert S % tm == 0 and F % tf == 0 and tm % HALO == 0 and F // tf >= 2
    assert S % seg == 0 and (seg % tm == 0 or tm % seg == 0)
    nh = tm // HALO
    vec = pl.BlockSpec((1, 1, D), lambda bb, i, f: (bb, 0, 0))
    cols = lambda arr, rows: _layer_spec(arr, layer, lambda bb, i, f: (0, f), (rows, tf))
    const = lambda arr: _layer_spec(arr, layer, lambda bb, i, f: (0, 0), arr.shape[1:])
    return pl.pallas_call(
        functools.partial(_ffn_kernel, alpha=alpha, tm=tm, seg=seg),
        out_shape=jax.ShapeDtypeStruct((B, S, D), F32),
        grid=(B, S // tm, F // tf),
        in_specs=[
            pl.BlockSpec(memory_space=pl.ANY),
            pl.BlockSpec((1, HALO, D), lambda bb, i, f: (bb, jnp.maximum(i * nh - 1, 0), 0)),
            pl.BlockSpec((1, HALO, D), lambda bb, i, f: (bb, jnp.minimum((i + 1) * nh, S // HALO - 1), 0)),
            vec, vec, vec,
            cols(sw["ffn_w_gate"], D), cols(sw["ffn_w_up"], D), cols(sw["ffn_conv_w"], CONV_W),
            cols(sw["ffn_conv_b"], 1),
            _layer_spec(sw["ffn_w_down"], layer, lambda bb, i, f: (f, 0), (tf, D)),
            const(sw["ln2_g"]), const(sw["ln2_b"]),
        ],
        out_specs=pl.BlockSpec((1, tm, D), lambda bb, i, f: (bb, i, 0)),
        scratch_shapes=[pltpu.VMEM((tm + 2 * HALO, D), BF16), pltpu.VMEM((tm, D), F32), pltpu.SemaphoreType.DMA(())],
        compiler_params=_cparams("arbitrary", "arbitrary", "arbitrary"),
        name="conv_ffn",
    )(x, x, x, shift, scale, gate, sw["ffn_w_gate"], sw["ffn_w_up"], sw["ffn_conv_w"], sw["ffn_conv_b"],
      sw["ffn_w_down"], sw["ln2_g"], sw["ln2_b"])


def _rope_tables(S):
    t = jnp.arange(S)
    row, col = (t // GRID_W).astype(F32), (t % GRID_W).astype(F32)
    lane = np.arange(LANE)

    def tables(dim):
        half = dim // 4
        inv_freq = ROPE_THETA ** (-jnp.arange(half, dtype=F32) / half)
        used = lane < dim
        pos = jnp.where(jnp.asarray((lane % dim) < dim // 2)[None, :], row[:, None], col[:, None])
        ang = pos * inv_freq[jnp.asarray(lane % half)][None, :]
        cos, sin = jnp.cos(ang), jnp.sin(ang)
        lo = jnp.asarray(used & ((lane % (2 * half)) < half))[None, :]
        hi = jnp.asarray(used & ((lane % (2 * half)) >= half))[None, :]
        return [jnp.where(jnp.asarray(used)[None, :], cos, 1.0), jnp.where(lo, -sin, 0.0), jnp.where(hi, sin, 0.0)]

    return jnp.stack(tables(HEAD_DIM) + tables(MLA_ROPE)).astype(F32)


def _identity_rope(C):
    one, zero = jnp.ones((C, LANE), F32), jnp.zeros((C, LANE), F32)
    return jnp.stack([one, zero, zero, one, zero, zero])


def _pad_w_uq(w):
    L, r, _ = w.shape
    w = w.reshape(L, r, MLA_HEADS, MLA_NOPE + MLA_ROPE)
    w = jnp.concatenate([w, jnp.zeros((L, r, MLA_HEADS, MLA_QK - MLA_NOPE - MLA_ROPE), w.dtype)], axis=3)
    return w.reshape(L, r, MLA_HEADS * MLA_QK).astype(BF16)


def kernel(x, c, ctx, c_ctx, w_ada, b_ada, w_in, na_rpb, swa_sink, mla_q_norm, mla_kv_norm, mla_w_uq, mla_w_ukv,
           gqa_q_norm, gqa_k_norm, w_out, ln1_g, ln1_b, ffn_w_gate, ffn_w_up, ffn_conv_w, ffn_conv_b, ffn_w_down,
           ln2_g, ln2_b):
    B, S, D = x.shape
    C = ctx.shape[1]
    depth = w_ada.shape[0]
    alpha = (2 * depth) ** 0.25

    cvec = jnp.concatenate([c, c_ctx[None, :], jnp.zeros((8 - B - 1, D), F32)], axis=0)
    ada = _ada(cvec, w_ada, b_ada)
    rope_x, rope_c = _rope_tables(S), _identity_rope(C)

    row = lambda a: a[:, None, :]
    w_in_b = w_in.astype(BF16)
    sw = {
        "w_in_abc": w_in_b[..., :_ABC_W], "w_in_d": w_in_b[..., _ABC_W:],
        "w_uq": _pad_w_uq(mla_w_uq), "w_ukv": mla_w_ukv.astype(BF16),
        "mla_q_norm": row(mla_q_norm), "mla_kv_norm": row(mla_kv_norm),
        "gqa_q_norm": row(gqa_q_norm), "gqa_k_norm": row(gqa_k_norm),
        "w_out": w_out.astype(BF16), "ln1_g": row(ln1_g), "ln1_b": row(ln1_b),
        "ffn_w_gate": ffn_w_gate.astype(BF16), "ffn_w_up": ffn_w_up.astype(BF16), "ffn_conv_w": ffn_conv_w,
        "ffn_conv_b": row(ffn_conv_b), "ffn_w_down": ffn_w_down.astype(BF16), "ln2_g": row(ln2_g), "ln2_b": row(ln2_b),
    }

    tm_proj = min(256, S)
    tm_out = min(1024, S)
    tm_ffn, tf = min(1024, S), 512
    tq_dense, tk_dense = min(1024, S), min(512, S)
    tq_swa = min(256, S - 2 * SWA_WINDOW)
    g_swa, g_gqa = SWA_HEADS // SWA_KV_HEADS, GQA_HEADS // GQA_KV_HEADS

    for i in range(depth):
        need_ctx = i < depth - 1
        mx = ada[i, :B].reshape(B, 1, 6, D)
        mod_x = [mx[:, :, j] for j in range(6)]
        mod_c = [jnp.broadcast_to(ada[i, B].reshape(1, 1, 6, D)[:, :, j], (B, 1, D)) for j in range(6)]

        qa, ka, va, qb, kb, vb, qc, kc, vc, qd, kd, vd = _proj(x, mod_x[0], mod_x[1], rope_x, sw, i, tm_proj)
        qa_c, ka_c, va_c, qb_c, kb_c, vb_c, qc_c, kc_c, vc_c, qd_c, kd_c, vd_c = _proj(
            ctx, mod_c[0], mod_c[1], rope_c, sw, i, C)

        mix_x = (
            _na_attention(qa, ka, va, ka_c, va_c, na_rpb[i]),
            _swa_attention(qb, kb, vb, kb_c, vb_c, swa_sink[i], tq_swa),
            _dense_attention(qc, kc, vc, kc_c, vc_c, MLA_HEADS, 1, tq_dense, tk_dense),
            _dense_attention(qd, kd, vd, kd_c, vd_c, GQA_HEADS, g_gqa, tq_dense, tk_dense),
        )
        x = _out_proj(mix_x, x, mod_x[2], sw, i, alpha, tm_out)
        if need_ctx:
            mix_c = (
                _dense_attention(qa_c, ka_c, va_c, None, None, NA_HEADS, 1, C, C),
                _dense_attention(qb_c, kb_c, vb_c, None, None, SWA_HEADS, g_swa, C, C, sink=swa_sink[i]),
                _dense_attention(qc_c, kc_c, vc_c, None, None, MLA_HEADS, 1, C, C),
                _dense_attention(qd_c, kd_c, vd_c, None, None, GQA_HEADS, g_gqa, C, C),
            )
            ctx = _out_proj(mix_c, ctx, mod_c[2], sw, i, alpha, C)

        x = _ffn(x, mod_x[3], mod_x[4], mod_x[5], sw, i, alpha, tm_ffn, tf)
        if need_ctx:
            flat = _ffn(ctx.reshape(1, B * C, D), mod_c[3][:1], mod_c[4][:1], mod_c[5][:1], sw, i, alpha,
                        min(tm_ffn, B * C), tf, seg=C)
            ctx = flat.reshape(B, C, D)
    return x
```

```python
import functools
import math

import numpy as np
import jax
import jax.numpy as jnp
from jax import lax
from jax.experimental import pallas as pl
from jax.experimental.pallas import tpu as pltpu

GRID_W = 64
HEAD_DIM = 128
NA_HEADS = 4
NA_WIN_R = 8
NA_WIN_C = 16
SWA_HEADS = 4
SWA_KV_HEADS = 2
SWA_WINDOW = 128
MLA_HEADS = 4
MLA_Q_LORA = 384
MLA_KV_LORA = 128
MLA_NOPE = 128
MLA_ROPE = 64
MLA_V = 128
GQA_HEADS = 4
GQA_KV_HEADS = 2
CONV_W = 3
ROPE_THETA = 10000.0
EPS = 1e-6
NEG = -1e30
LOG2E = math.log2(math.e)

LANE = 128
BF16_SUBLANES = 16
VMEM_LIMIT = 56 * 1024 * 1024

F32 = jnp.float32
BF16 = jnp.bfloat16

OUT_SUB_ROWS = 256
NA_Q_ROWS = 4
NA_KEY_ROWS = NA_WIN_R + NA_Q_ROWS
NA_TILES_PER_STEP = 8
SWA_TILES_PER_STEP = 8
NA_LOOKAHEAD, SWA_LOOKAHEAD = 1, 2
DENSE_LOOKAHEAD = 1


def _cparams(*sem):
    return pltpu.CompilerParams(dimension_semantics=sem, vmem_limit_bytes=VMEM_LIMIT)


def _dot(a, b):
    return jnp.dot(a, b, preferred_element_type=F32)


def _layer_norm(x):
    mu = jnp.mean(x, axis=-1, keepdims=True)
    xc = x - mu
    var = jnp.mean(xc * xc, axis=-1, keepdims=True)
    return xc * lax.rsqrt(var + EPS)


def _rms_norm(x, g):
    return x * lax.rsqrt(jnp.mean(x * x, axis=-1, keepdims=True) + EPS) * g


def _rope(x, cos, sin_lo, sin_hi, half):
    n = x.shape[-1]
    return x * cos + pltpu.roll(x, n - half, 1) * sin_lo + pltpu.roll(x, half, 1) * sin_hi


def _layer_spec(arr, layer, index_map_rest, block_rest, **kw):
    return pl.BlockSpec((None,) + tuple(block_rest), lambda *g: (layer,) + tuple(index_map_rest(*g)), **kw)


def _ada_kernel(c_ref, w_ref, b_ref, o_ref):
    c = c_ref[...]
    a = (c * jax.nn.sigmoid(c)).astype(BF16)
    o_ref[0] = _dot(a, w_ref[0].astype(BF16)) + b_ref[0]


def _ada(cvec, w_ada, b_ada):
    L, D, N = w_ada.shape
    M = cvec.shape[0]
    tn = 1024
    return pl.pallas_call(
        _ada_kernel,
        out_shape=jax.ShapeDtypeStruct((L, M, N), F32),
        grid=(L, N // tn),
        in_specs=[
            pl.BlockSpec((M, D), lambda l, j: (0, 0)),
            pl.BlockSpec((1, D, tn), lambda l, j: (l, 0, j)),
            pl.BlockSpec((1, 1, tn), lambda l, j: (l, 0, j)),
        ],
        out_specs=pl.BlockSpec((1, M, tn), lambda l, j: (l, 0, j)),
        compiler_params=_cparams("parallel", "parallel"),
        name="ada",
    )(cvec, w_ada, b_ada.reshape(L, 1, N))


_A0 = 0
_B0 = _A0 + 3 * NA_HEADS * HEAD_DIM
_C0 = _B0 + (SWA_HEADS + 2 * SWA_KV_HEADS) * HEAD_DIM
_KPE0 = _C0 + MLA_Q_LORA + MLA_KV_LORA
_ABC_W = _KPE0 + MLA_ROPE
_D_W = (GQA_HEADS + 2 * GQA_KV_HEADS) * HEAD_DIM
MLA_QK = 2 * LANE


def _proj_kernel(x_ref, shift_ref, scale_ref, w_ref, wd_ref, rope_ref, gq_lora_ref, gkv_lora_ref, gq_ref, gk_ref,
                 wuq_ref, wukv_ref,
                 qa_ref, ka_ref, va_ref, qb_ref, kb_ref, vb_ref, qc_ref, kc_ref, vc_ref, qd_ref, kd_ref, vd_ref):
    h = (_layer_norm(x_ref[0]) * (1.0 + scale_ref[0]) + shift_ref[0]).astype(BF16)
    cos2, slo2, shi2 = rope_ref[0], rope_ref[1], rope_ref[2]
    cos1, slo1, shi1 = rope_ref[3], rope_ref[4], rope_ref[5]
    rope2 = lambda t: _rope(t, cos2, slo2, shi2, HEAD_DIM // 4)
    rope1 = lambda t: _rope(t, cos1, slo1, shi1, MLA_ROPE // 4)
    hd = HEAD_DIM
    sc = hd ** -0.5 * LOG2E

    def store_t(ref, lo, t):
        ref[0, lo:lo + t.shape[1], :] = t.T.astype(BF16)


    wq, wk = GQA_HEADS * hd, GQA_KV_HEADS * hd
    pd = _dot(h, wd_ref[...])
    for i in range(GQA_HEADS):
        store_t(qd_ref, i * hd, rope2(_rms_norm(pd[:, i * hd:(i + 1) * hd], gq_ref[...])) * sc)
    for i in range(GQA_KV_HEADS):
        t = _rms_norm(pd[:, wq + i * hd:wq + (i + 1) * hd], gk_ref[...])
        kd_ref[0, :, i * hd:(i + 1) * hd] = rope2(t).astype(BF16)
    store_t(vd_ref, 0, pd[:, wq + wk:])

    pc = _dot(h, w_ref[:, _C0:_KPE0])
    cq = _rms_norm(pc[:, :MLA_Q_LORA], gq_lora_ref[...]).astype(BF16)
    ckv = _rms_norm(pc[:, MLA_Q_LORA:], gkv_lora_ref[...]).astype(BF16)
    kpe = _dot(h, w_ref[:, _KPE0:_ABC_W])
    kpe = rope1(jnp.concatenate([kpe, jnp.zeros((kpe.shape[0], LANE - MLA_ROPE), F32)], axis=1)).astype(BF16)
    qup = _dot(cq, wuq_ref[...])
    kvup = _dot(ckv, wukv_ref[...])
    sc_mla = (MLA_NOPE + MLA_ROPE) ** -0.5 * LOG2E
    for i in range(MLA_HEADS):
        o = i * MLA_QK
        store_t(qc_ref, o, qup[:, o:o + LANE] * sc_mla)
        store_t(qc_ref, o + LANE, rope1(qup[:, o + LANE:o + 2 * LANE]) * sc_mla)
        kc_ref[0, :, o:o + LANE] = kvup[:, o:o + LANE].astype(BF16)
        kc_ref[0, :, o + LANE:o + 2 * LANE] = kpe
        store_t(vc_ref, i * MLA_V, kvup[:, o + LANE:o + 2 * LANE])

    wq, wk = SWA_HEADS * hd, SWA_KV_HEADS * hd
    pb = _dot(h, w_ref[:, _B0:_B0 + wq + 2 * wk])
    for i in range(SWA_HEADS):
        store_t(qb_ref, i * hd, rope2(pb[:, i * hd:(i + 1) * hd]) * sc)
    for i in range(SWA_KV_HEADS):
        kb_ref[0, :, i * hd:(i + 1) * hd] = rope2(pb[:, wq + i * hd:wq + (i + 1) * hd]).astype(BF16)
    store_t(vb_ref, 0, pb[:, wq + wk:])

    w = NA_HEADS * hd
    pa = _dot(h, w_ref[:, _A0:_A0 + 3 * w])
    store_t(qa_ref, 0, pa[:, :w] * sc)
    store_t(va_ref, 0, pa[:, 2 * w:])
    ka_ref[0] = pa[:, w:2 * w].astype(BF16)


def _proj(x, shift, scale, rope_tab, sw, layer, tm):
    B, S, D = x.shape
    widths = (NA_HEADS * HEAD_DIM,) * 3 + (SWA_HEADS * HEAD_DIM, SWA_KV_HEADS * HEAD_DIM, SWA_KV_HEADS * HEAD_DIM) \
        + (MLA_HEADS * MLA_QK, MLA_HEADS * MLA_QK, MLA_HEADS * MLA_V) \
        + (GQA_HEADS * HEAD_DIM, GQA_KV_HEADS * HEAD_DIM, GQA_KV_HEADS * HEAD_DIM)
    transposed = (True, False, True) * 4
    const = lambda arr: _layer_spec(arr, layer, lambda b, i: (0, 0), arr.shape[1:], pipeline_mode=pl.Buffered(1))
    vec = pl.BlockSpec((1, 1, D), lambda b, i: (b, 0, 0))
    out_spec = lambda n, t: (pl.BlockSpec((1, n, tm), lambda b, i: (b, 0, i)) if t
                             else pl.BlockSpec((1, tm, n), lambda b, i: (b, i, 0)))
    params = (sw["w_in_abc"], sw["w_in_d"], None, sw["mla_q_norm"], sw["mla_kv_norm"], sw["gqa_q_norm"],
              sw["gqa_k_norm"], sw["w_uq"], sw["w_ukv"])
    in_specs = [pl.BlockSpec((1, tm, D), lambda b, i: (b, i, 0)), vec, vec]
    in_specs += [pl.BlockSpec((6, tm, LANE), lambda b, i: (0, i, 0)) if p is None else const(p) for p in params]
    args = [rope_tab if p is None else p for p in params]
    return pl.pallas_call(
        _proj_kernel,
        out_shape=[jax.ShapeDtypeStruct((B, n, S) if t else (B, S, n), BF16) for n, t in zip(widths, transposed)],
        grid=(B, S // tm),
        in_specs=in_specs,
        out_specs=[out_spec(n, t) for n, t in zip(widths, transposed)],
        compiler_params=_cparams("parallel", "parallel"),
        name="proj",
    )(x, shift, scale, *args)


def _with_ones_rows(vt):
    return jnp.concatenate([vt, jnp.ones((BF16_SUBLANES, vt.shape[1]), BF16)], axis=0)


def _softmax_pv(parts, sink=None):
    m = parts[0][0].max(axis=0, keepdims=True)
    for s, _ in parts[1:]:
        m = jnp.maximum(m, s.max(axis=0, keepdims=True))
    l = 0.0
    if sink is not None:
        m = jnp.maximum(m, sink)
        l = jnp.exp2(sink - m)
    acc = None
    for s, vt in parts:
        p = jnp.exp2(s - m)
        l = l + p.sum(axis=0, keepdims=True)
        pv = _dot(vt, p.astype(BF16))
        acc = pv if acc is None else acc + pv
    return acc / l


def _pipelined_tiles(n_tiles, scores, finish, lookahead):
    pending = [scores(t) for t in range(min(lookahead, n_tiles))]
    for t in range(n_tiles):
        if t + lookahead < n_tiles:
            pending.append(scores(t + lookahead))
        finish(t, pending.pop(0))


def _na_kernel(cls_ref, ws_ref, qt_ref, k_ref, vt_ref, kc_ref, vct_ref, *rest):
    del cls_ref
    bias_refs, o_ref = rest[:-1], rest[-1]
    nk, tq = NA_KEY_ROWS * GRID_W, NA_Q_ROWS * GRID_W
    start = lambda t: pl.multiple_of(ws_ref[pl.program_id(2) * len(bias_refs) + t] * GRID_W, tq)

    def scores(t):
        qt = qt_ref[0, :, t * tq:(t + 1) * tq]
        return _dot(k_ref[0, pl.ds(start(t), nk), :], qt), _dot(kc_ref[0], qt)

    def finish(t, s):
        o = _softmax_pv([(s[0] + bias_refs[t][0, 0], vt_ref[0, :, pl.ds(start(t), nk)]), (s[1], vct_ref[0])])
        o_ref[0, t * tq:(t + 1) * tq, :] = o.T.astype(o_ref.dtype)

    _pipelined_tiles(len(bias_refs), scores, finish, NA_LOOKAHEAD)


def _na_tables(rpb, rows):
    nt = rows // NA_Q_ROWS
    R = np.arange(nt) * NA_Q_ROWS
    ws = np.clip(R - NA_WIN_R // 2, 0, rows - NA_KEY_ROWS)
    qr = (R[:, None] + np.arange(NA_Q_ROWS)[None, :])
    kr = ws[:, None] + np.arange(NA_KEY_ROWS)[None, :]
    r0 = np.clip(qr - NA_WIN_R // 2, 0, rows - NA_WIN_R)
    row_ok = (kr[:, None, :] >= r0[:, :, None]) & (kr[:, None, :] < r0[:, :, None] + NA_WIN_R)
    drow = np.clip(kr[:, None, :] - qr[:, :, None] + NA_WIN_R - 1, 0, 2 * NA_WIN_R - 2)
    geom = np.concatenate([row_ok.reshape(nt, -1).astype(np.int64), drow.reshape(nt, -1)], axis=1)
    _, first, cls = np.unique(geom, axis=0, return_index=True, return_inverse=True)
    cq = np.arange(GRID_W)
    c0 = np.clip(cq - NA_WIN_C // 2, 0, GRID_W - NA_WIN_C)
    col_ok = (cq[None, :] >= c0[:, None]) & (cq[None, :] < c0[:, None] + NA_WIN_C)
    dcol = np.clip(cq[None, :] - cq[:, None] + NA_WIN_C - 1, 0, 2 * NA_WIN_C - 2)
    nc, H = len(first), rpb.shape[0]
    cols = jnp.zeros(rpb.shape[:2] + dcol.shape, F32)
    for j in range(rpb.shape[2]):
        cols = jnp.where(jnp.asarray(dcol == j)[None, None], rpb[:, :, j][:, :, None, None] * LOG2E, cols)
    cols = jnp.where(jnp.asarray(col_ok)[None, None], cols, NEG)
    nd = rpb.shape[1]
    pad = jnp.full((H, NA_KEY_ROWS * GRID_W, GRID_W), NEG, F32)
    flat = jnp.concatenate([pad, cols.transpose(0, 1, 3, 2).reshape(H, nd * GRID_W, GRID_W), pad], axis=1)
    per_class = []
    for c in first:
        strips = []
        for q in range(NA_Q_ROWS):
            d0 = int(kr[c, 0] - qr[c, q]) + NA_WIN_R - 1 + NA_KEY_ROWS
            strip = flat[:, d0 * GRID_W:(d0 + NA_KEY_ROWS) * GRID_W, :]
            keep = np.repeat(row_ok[c, q], GRID_W)
            strips.append(jnp.where(jnp.asarray(keep)[None, :, None], strip, NEG))
        per_class.append(jnp.concatenate(strips, axis=-1))
    bias = jnp.stack(per_class)
    return bias, jnp.asarray(cls.reshape(-1), jnp.int32), jnp.asarray(ws, jnp.int32)


def _na_attention(qt, k, vt, kc, vct, rpb):
    B, _, S = qt.shape
    C = kc.shape[1]
    rows = S // GRID_W
    assert rows >= NA_KEY_ROWS and rows % NA_Q_ROWS == 0
    bias, cls, ws = _na_tables(rpb, rows)
    tq, nk, hd = NA_Q_ROWS * GRID_W, NA_KEY_ROWS * GRID_W, HEAD_DIM
    n_sub = NA_TILES_PER_STEP if (S // tq) % NA_TILES_PER_STEP == 0 else 1
    bias_spec = lambda t: pl.BlockSpec((1, 1, nk, tq), lambda b, h, i, c, w: (c[i * n_sub + t], h, 0, 0))
    grid_spec = pltpu.PrefetchScalarGridSpec(
        num_scalar_prefetch=2,
        grid=(B, NA_HEADS, S // (tq * n_sub)),
        in_specs=[
            pl.BlockSpec((1, hd, n_sub * tq), lambda b, h, i, c, w: (b, h, i)),
            pl.BlockSpec((1, S, hd), lambda b, h, i, c, w: (b, 0, h)),
            pl.BlockSpec((1, hd, S), lambda b, h, i, c, w: (b, h, 0)),
            pl.BlockSpec((1, C, hd), lambda b, h, i, c, w: (b, 0, h)),
            pl.BlockSpec((1, hd, C), lambda b, h, i, c, w: (b, h, 0)),
        ] + [bias_spec(t) for t in range(n_sub)],
        out_specs=pl.BlockSpec((1, n_sub * tq, hd), lambda b, h, i, c, w: (b, i, h)),
    )
    return pl.pallas_call(
        _na_kernel,
        out_shape=jax.ShapeDtypeStruct((B, S, NA_HEADS * hd), BF16),
        grid_spec=grid_spec,
        compiler_params=_cparams("parallel", "parallel", "arbitrary"),
        name="na_attn",
    )(cls, ws, qt, k, vt, kc, vct, *([bias] * n_sub))


def _swa_kernel(qt_ref, k_ref, vt_ref, kc_ref, vct_ref, sink_ref, o_ref, *, tq, n_sub, seq):
    nk = tq + 2 * SWA_WINDOW
    tile = lambda t: pl.program_id(2) * n_sub + t
    start = lambda t: pl.multiple_of(jnp.clip(tile(t) * tq - SWA_WINDOW, 0, seq - nk), SWA_WINDOW)

    def scores(t):
        qt = qt_ref[0, :, t * tq:(t + 1) * tq]
        return _dot(k_ref[0, pl.ds(start(t), nk), :], qt), _dot(kc_ref[0], qt)

    def finish(t, s):
        kpos = start(t) + lax.broadcasted_iota(jnp.int32, (nk, tq), 0)
        qpos = tile(t) * tq + lax.broadcasted_iota(jnp.int32, (nk, tq), 1)
        s_win = jnp.where(jnp.abs(qpos - kpos) <= SWA_WINDOW, s[0], NEG)
        o = _softmax_pv([(s_win, vt_ref[0, :, pl.ds(start(t), nk)]), (s[1], vct_ref[0])], sink=sink_ref[0])
        o_ref[0, t * tq:(t + 1) * tq, :] = o.T.astype(o_ref.dtype)

    _pipelined_tiles(n_sub, scores, finish, SWA_LOOKAHEAD)


def _swa_attention(qt, k, vt, kc, vct, sink, tq):
    B, _, S = qt.shape
    C = kc.shape[1]
    hd, G = HEAD_DIM, SWA_HEADS // SWA_KV_HEADS
    assert S >= tq + 2 * SWA_WINDOW and S % tq == 0 and tq % SWA_WINDOW == 0
    n_sub = SWA_TILES_PER_STEP if (S // tq) % SWA_TILES_PER_STEP == 0 else 1
    return pl.pallas_call(
        functools.partial(_swa_kernel, tq=tq, n_sub=n_sub, seq=S),
        out_shape=jax.ShapeDtypeStruct((B, S, SWA_HEADS * hd), BF16),
        grid=(B, SWA_HEADS, S // (tq * n_sub)),
        in_specs=[
            pl.BlockSpec((1, hd, n_sub * tq), lambda b, h, i: (b, h, i)),
            pl.BlockSpec((1, S, hd), lambda b, h, i: (b, 0, h // G)),
            pl.BlockSpec((1, hd, S), lambda b, h, i: (b, h // G, 0)),
            pl.BlockSpec((1, C, hd), lambda b, h, i: (b, 0, h // G)),
            pl.BlockSpec((1, hd, C), lambda b, h, i: (b, h // G, 0)),
            pl.BlockSpec((1, 1, 1), lambda b, h, i: (h, 0, 0)),
        ],
        out_specs=pl.BlockSpec((1, n_sub * tq, hd), lambda b, h, i: (b, i, h)),
        compiler_params=_cparams("parallel", "parallel", "arbitrary"),
        name="swa_attn",
    )(qt, k, vt, kc, vct, (sink * LOG2E).reshape(SWA_HEADS, 1, 1).astype(F32))


def _dense_kernel(*refs, tk, seq, n_ctx, has_sink):
    refs = list(refs)
    qt_ref, k_ref, vt_ref = refs[:3]
    o_ref = refs[-1]
    kc_ref, vct_ref = refs[3:5] if n_ctx else (None, None)
    qt = qt_ref[0]
    tq = qt.shape[1]
    dv = vt_ref.shape[1]
    if has_sink:
        m = jnp.broadcast_to(refs[-2][0], (1, tq))
        acc = jnp.concatenate([jnp.zeros((dv, tq), F32), jnp.ones((BF16_SUBLANES, tq), F32)], axis=0)
    else:
        m = jnp.full((1, tq), -jnp.inf, F32)
        acc = jnp.zeros((dv + BF16_SUBLANES, tq), F32)

    chunks = [(k_ref, vt_ref, j * tk, tk) for j in range(seq // tk)]
    if n_ctx:
        chunks.append((kc_ref, vct_ref, 0, n_ctx))
    scores = lambda c: _dot(c[0][0, c[2]:c[2] + c[3], :], qt)

    pending = [scores(c) for c in chunks[:DENSE_LOOKAHEAD]]
    for j, c in enumerate(chunks):
        if j + DENSE_LOOKAHEAD < len(chunks):
            pending.append(scores(chunks[j + DENSE_LOOKAHEAD]))
        s = pending.pop(0)
        m_new = jnp.maximum(m, s.max(axis=0, keepdims=True))
        alpha = jnp.exp2(m - m_new)
        p = jnp.exp2(s - m_new).astype(BF16)
        acc = alpha * acc + _dot(_with_ones_rows(c[1][0, :, c[2]:c[2] + c[3]]), p)
        m = m_new
    o_ref[0] = (acc[:dv] / acc[dv:dv + 1]).T.astype(o_ref.dtype)


def _dense_attention(qt, k, vt, kc, vct, n_heads, group, tq, tk, sink=None):
    B, _, S = qt.shape
    Sk = k.shape[1]
    dk = qt.shape[1] // n_heads
    dv = vt.shape[1] // (n_heads // group)
    assert S % tq == 0 and Sk % tk == 0
    n_ctx = 0 if kc is None else kc.shape[1]
    in_specs = [
        pl.BlockSpec((1, dk, tq), lambda b, h, i: (b, h, i)),
        pl.BlockSpec((1, Sk, dk), lambda b, h, i: (b, 0, h // group)),
        pl.BlockSpec((1, dv, Sk), lambda b, h, i: (b, h // group, 0)),
    ]
    args = [qt, k, vt]
    if n_ctx:
        in_specs += [
            pl.BlockSpec((1, n_ctx, dk), lambda b, h, i: (b, 0, h // group)),
            pl.BlockSpec((1, dv, n_ctx), lambda b, h, i: (b, h // group, 0)),
        ]
        args += [kc, vct]
    if sink is not None:
        in_specs.append(pl.BlockSpec((1, 1, 1), lambda b, h, i: (h, 0, 0)))
        args.append((sink * LOG2E).reshape(n_heads, 1, 1).astype(F32))
    return pl.pallas_call(
        functools.partial(_dense_kernel, tk=tk, seq=Sk, n_ctx=n_ctx, has_sink=sink is not None),
        out_shape=jax.ShapeDtypeStruct((B, S, n_heads * dv), BF16),
        grid=(B, n_heads, S // tq),
        in_specs=in_specs,
        out_specs=pl.BlockSpec((1, tq, dv), lambda b, h, i: (b, i, h)),
        compiler_params=_cparams("parallel", "parallel", "arbitrary"),
        name="dense_attn",
    )(*args)


def _out_kernel(oa_ref, ob_ref, oc_ref, od_ref, w_ref, x_ref, gate_ref, g_ref, b_ref, o_ref, *, alpha, sub):
    for r0 in range(0, x_ref.shape[1], sub):
        rows = slice(r0, r0 + sub)
        y = None
        o = 0
        for r in (oa_ref, ob_ref, oc_ref, od_ref):
            n = r.shape[2]
            t = _dot(r[0, rows, :], w_ref[o:o + n, :])
            y = t if y is None else y + t
            o += n
        z = alpha * x_ref[0, rows, :] + gate_ref[0] * y
        o_ref[0, rows, :] = _layer_norm(z) * g_ref[...] + b_ref[...]


def _out_proj(mix, x, gate, sw, layer, alpha, tm):
    B, S, D = x.shape
    const = lambda arr: _layer_spec(arr, layer, lambda bb, i: (0, 0), arr.shape[1:], pipeline_mode=pl.Buffered(1))
    return pl.pallas_call(
        functools.partial(_out_kernel, alpha=alpha, sub=min(OUT_SUB_ROWS, tm)),
        out_shape=jax.ShapeDtypeStruct((B, S, D), F32),
        grid=(B, S // tm),
        in_specs=[pl.BlockSpec((1, tm, m.shape[2]), lambda bb, i: (bb, i, 0)) for m in mix] + [
            const(sw["w_out"]),
            pl.BlockSpec((1, tm, D), lambda bb, i: (bb, i, 0)),
            pl.BlockSpec((1, 1, D), lambda bb, i: (bb, 0, 0)),
            const(sw["ln1_g"]), const(sw["ln1_b"]),
        ],
        out_specs=pl.BlockSpec((1, tm, D), lambda bb, i: (bb, i, 0)),
        compiler_params=_cparams("parallel", "parallel"),
        name="out_proj",
    )(*mix, sw["w_out"], x, gate, sw["ln1_g"], sw["ln1_b"])


HALO = BF16_SUBLANES


def _ffn_kernel(x_hbm, xp_ref, xn_ref, shift_ref, scale_ref, gate_ref, wg_ref, wu_ref, cw_ref, cb_ref, wd_ref,
                g_ref, b_ref, o_ref, h_ref, x_buf, x_sem, *, alpha, tm, seg):
    bb, i, f = pl.program_id(0), pl.program_id(1), pl.program_id(2)
    n_b, n_i = pl.num_programs(0), pl.num_programs(1)
    acc_ref = o_ref.at[0]

    def x_copy(b_idx, i_idx):
        return pltpu.make_async_copy(x_hbm.at[b_idx, pl.ds(i_idx * tm, tm), :], x_buf, x_sem)

    @pl.when(f == 0)
    def _():
        @pl.when((bb == 0) & (i == 0))
        def _():
            x_copy(0, 0).start()

        x_copy(bb, i).wait()
        xv = x_buf[...]
        mod = lambda t: _layer_norm(t) * (1.0 + scale_ref[0]) + shift_ref[0]
        keep_p = ((i * tm) % seg != 0).astype(F32)
        keep_n = (((i + 1) * tm) % seg != 0).astype(F32)
        h_ref[0:HALO, :] = (mod(xp_ref[0]) * keep_p).astype(BF16)
        h_ref[HALO:HALO + tm, :] = mod(xv).astype(BF16)
        h_ref[HALO + tm:, :] = (mod(xn_ref[0]) * keep_n).astype(BF16)
        acc_ref[...] = alpha * xv

    @pl.when((f == 1) & ((bb < n_b - 1) | (i < n_i - 1)))
    def _():
        wrap = i == n_i - 1
        x_copy(jnp.where(wrap, bb + 1, bb), jnp.where(wrap, 0, i + 1)).start()

    n = tm + 2 * HALO
    gt = _dot(h_ref[...], wg_ref[...])
    up = _dot(h_ref[HALO:HALO + tm, :], wu_ref[...])
    prev, nxt = pltpu.roll(gt, 1, 0), pltpu.roll(gt, n - 1, 0)
    if seg < tm:
        pos = (lax.broadcasted_iota(jnp.int32, (n, 1), 0) + (seg - HALO)) % seg
        prev = jnp.where(pos == 0, 0.0, prev)
        nxt = jnp.where(pos == seg - 1, 0.0, nxt)
    a = (prev * cw_ref[0:1, :] + gt * cw_ref[1:2, :] + nxt * cw_ref[2:3, :] + cb_ref[...])[HALO:HALO + tm]
    y = (a * jax.nn.sigmoid(a) * up).astype(BF16)
    acc_ref[...] += gate_ref[0] * _dot(y, wd_ref[...])

    @pl.when(f == pl.num_programs(2) - 1)
    def _():
        o_ref[0] = _layer_norm(acc_ref[...]) * g_ref[...] + b_ref[...]


def _ffn(x, shift, scale, gate, sw, layer, alpha, tm, tf, seg=None):
    B, S, D = x.shape
    seg = S if seg is None else seg
    F = sw["ffn_w_gate"].shape[2]
    assert S % tm == 0 and F % tf == 0 and tm % HALO == 0 and F // tf >= 2
    assert S % seg == 0 and (seg % tm == 0 or tm % seg == 0)
    nh = tm // HALO
    vec = pl.BlockSpec((1, 1, D), lambda bb, i, f: (bb, 0, 0))
    cols = lambda arr, rows: _layer_spec(arr, layer, lambda bb, i, f: (0, f), (rows, tf))
    const = lambda arr: _layer_spec(arr, layer, lambda bb, i, f: (0, 0), arr.shape[1:])
    return pl.pallas_call(
        functools.partial(_ffn_kernel, alpha=alpha, tm=tm, seg=seg),
        out_shape=jax.ShapeDtypeStruct((B, S, D), F32),
        grid=(B, S // tm, F // tf),
        in_specs=[
            pl.BlockSpec(memory_space=pl.ANY),
            pl.BlockSpec((1, HALO, D), lambda bb, i, f: (bb, jnp.maximum(i * nh - 1, 0), 0)),
            pl.BlockSpec((1, HALO, D), lambda bb, i, f: (bb, jnp.minimum((i + 1) * nh, S // HALO - 1), 0)),
            vec, vec, vec,
            cols(sw["ffn_w_gate"], D), cols(sw["ffn_w_up"], D), cols(sw["ffn_conv_w"], CONV_W),
            cols(sw["ffn_conv_b"], 1),
            _layer_spec(sw["ffn_w_down"], layer, lambda bb, i, f: (f, 0), (tf, D)),
            const(sw["ln2_g"]), const(sw["ln2_b"]),
        ],
        out_specs=pl.BlockSpec((1, tm, D), lambda bb, i, f: (bb, i, 0)),
        scratch_shapes=[pltpu.VMEM((tm + 2 * HALO, D), BF16), pltpu.VMEM((tm, D), F32), pltpu.SemaphoreType.DMA(())],
        compiler_params=_cparams("arbitrary", "arbitrary", "arbitrary"),
        name="conv_ffn",
    )(x, x, x, shift, scale, gate, sw["ffn_w_gate"], sw["ffn_w_up"], sw["ffn_conv_w"], sw["ffn_conv_b"],
      sw["ffn_w_down"], sw["ln2_g"], sw["ln2_b"])


def _rope_tables(S):
    t = jnp.arange(S)
    row, col = (t // GRID_W).astype(F32), (t % GRID_W).astype(F32)
    lane = np.arange(LANE)

    def tables(dim):
        half = dim // 4
        inv_freq = ROPE_THETA ** (-jnp.arange(half, dtype=F32) / half)
        used = lane < dim
        pos = jnp.where(jnp.asarray((lane % dim) < dim // 2)[None, :], row[:, None], col[:, None])
        ang = pos * inv_freq[jnp.asarray(lane % half)][None, :]
        cos, sin = jnp.cos(ang), jnp.sin(ang)
        lo = jnp.asarray(used & ((lane % (2 * half)) < half))[None, :]
        hi = jnp.asarray(used & ((lane % (2 * half)) >= half))[None, :]
        return [jnp.where(jnp.asarray(used)[None, :], cos, 1.0), jnp.where(lo, -sin, 0.0), jnp.where(hi, sin, 0.0)]

    return jnp.stack(tables(HEAD_DIM) + tables(MLA_ROPE)).astype(F32)


def _identity_rope(C):
    one, zero = jnp.ones((C, LANE), F32), jnp.zeros((C, LANE), F32)
    return jnp.stack([one, zero, zero, one, zero, zero])


def _pad_w_uq(w):
    L, r, _ = w.shape
    w = w.reshape(L, r, MLA_HEADS, MLA_NOPE + MLA_ROPE)
    w = jnp.concatenate([w, jnp.zeros((L, r, MLA_HEADS, MLA_QK - MLA_NOPE - MLA_ROPE), w.dtype)], axis=3)
    return w.reshape(L, r, MLA_HEADS * MLA_QK).astype(BF16)


def kernel(x, c, ctx, c_ctx, w_ada, b_ada, w_in, na_rpb, swa_sink, mla_q_norm, mla_kv_norm, mla_w_uq, mla_w_ukv,
           gqa_q_norm, gqa_k_norm, w_out, ln1_g, ln1_b, ffn_w_gate, ffn_w_up, ffn_conv_w, ffn_conv_b, ffn_w_down,
           ln2_g, ln2_b):
    B, S, D = x.shape
    C = ctx.shape[1]
    depth = w_ada.shape[0]
    alpha = (2 * depth) ** 0.25

    cvec = jnp.concatenate([c, c_ctx[None, :], jnp.zeros((8 - B - 1, D), F32)], axis=0)
    ada = _ada(cvec, w_ada, b_ada)
    rope_x, rope_c = _rope_tables(S), _identity_rope(C)

    row = lambda a: a[:, None, :]
    w_in_b = w_in.astype(BF16)
    sw = {
        "w_in_abc": w_in_b[..., :_ABC_W], "w_in_d": w_in_b[..., _ABC_W:],
        "w_uq": _pad_w_uq(mla_w_uq), "w_ukv": mla_w_ukv.astype(BF16),
        "mla_q_norm": row(mla_q_norm), "mla_kv_norm": row(mla_kv_norm),
        "gqa_q_norm": row(gqa_q_norm), "gqa_k_norm": row(gqa_k_norm),
        "w_out": w_out.astype(BF16), "ln1_g": row(ln1_g), "ln1_b": row(ln1_b),
        "ffn_w_gate": ffn_w_gate.astype(BF16), "ffn_w_up": ffn_w_up.astype(BF16), "ffn_conv_w": ffn_conv_w,
        "ffn_conv_b": row(ffn_conv_b), "ffn_w_down": ffn_w_down.astype(BF16), "ln2_g": row(ln2_g), "ln2_b": row(ln2_b),
    }

    tm_proj = min(256, S)
    tm_out = min(512, S)
    tm_ffn, tf = min(1024, S), 512
    tq_dense, tk_dense = min(1024, S), min(512, S)
    tq_swa = min(256, S - 2 * SWA_WINDOW)
    g_swa, g_gqa = SWA_HEADS // SWA_KV_HEADS, GQA_HEADS // GQA_KV_HEADS

    for i in range(depth):
        need_ctx = i < depth - 1
        mx = ada[i, :B].reshape(B, 1, 6, D)
        mod_x = [mx[:, :, j] for j in range(6)]
        mod_c = [jnp.broadcast_to(ada[i, B].reshape(1, 1, 6, D)[:, :, j], (B, 1, D)) for j in range(6)]

        qa, ka, va, qb, kb, vb, qc, kc, vc, qd, kd, vd = _proj(x, mod_x[0], mod_x[1], rope_x, sw, i, tm_proj)
        qa_c, ka_c, va_c, qb_c, kb_c, vb_c, qc_c, kc_c, vc_c, qd_c, kd_c, vd_c = _proj(
            ctx, mod_c[0], mod_c[1], rope_c, sw, i, C)

        mix_x = (
            _na_attention(qa, ka, va, ka_c, va_c, na_rpb[i]),
            _swa_attention(qb, kb, vb, kb_c, vb_c, swa_sink[i], tq_swa),
            _dense_attention(qc, kc, vc, kc_c, vc_c, MLA_HEADS, 1, tq_dense, tk_dense),
            _dense_attention(qd, kd, vd, kd_c, vd_c, GQA_HEADS, g_gqa, tq_dense, tk_dense),
        )
        x = _out_proj(mix_x, x, mod_x[2], sw, i, alpha, tm_out)
        if need_ctx:
            mix_c = (
                _dense_attention(qa_c, ka_c, va_c, None, None, NA_HEADS, 1, C, C),
                _dense_attention(qb_c, kb_c, vb_c, None, None, SWA_HEADS, g_swa, C, C, sink=swa_sink[i]),
                _dense_attention(qc_c, kc_c, vc_c, None, None, MLA_HEADS, 1, C, C),
                _dense_attention(qd_c, kd_c, vd_c, None, None, GQA_HEADS, g_gqa, C, C),
            )
            ctx = _out_proj(mix_c, ctx, mod_c[2], sw, i, alpha, C)

        x = _ffn(x, mod_x[3], mod_x[4], mod_x[5], sw, i, alpha, tm_ffn, tf)
        if need_ctx:
            flat = _ffn(ctx.reshape(1, B * C, D), mod_c[3][:1], mod_c[4][:1], mod_c[5][:1], sw, i, alpha,
                        min(tm_ffn, B * C), tf, seg=C)
            ctx = flat.reshape(B, C, D)
    return x
```

```python
import functools
import math

import numpy as np
import jax
import jax.numpy as jnp
from jax import lax
from jax.experimental import pallas as pl
from jax.experimental.pallas import tpu as pltpu

GRID_W = 64
HEAD_DIM = 128
NA_HEADS = 4
NA_WIN_R = 8
NA_WIN_C = 16
SWA_HEADS = 4
SWA_KV_HEADS = 2
SWA_WINDOW = 128
MLA_HEADS = 4
MLA_Q_LORA = 384
MLA_KV_LORA = 128
MLA_NOPE = 128
MLA_ROPE = 64
MLA_V = 128
GQA_HEADS = 4
GQA_KV_HEADS = 2
CONV_W = 3
ROPE_THETA = 10000.0
EPS = 1e-6
NEG = -1e30
LOG2E = math.log2(math.e)

LANE = 128
BF16_SUBLANES = 16
VMEM_LIMIT = 56 * 1024 * 1024

F32 = jnp.float32
BF16 = jnp.bfloat16

OUT_SUB_ROWS = 256
NA_Q_ROWS = 4
NA_KEY_ROWS = NA_WIN_R + NA_Q_ROWS
NA_TILES_PER_STEP = 16
SWA_TILES_PER_STEP = 16
NA_LOOKAHEAD, SWA_LOOKAHEAD = 1, 2
DENSE_LOOKAHEAD = 1


def _cparams(*sem):
    return pltpu.CompilerParams(dimension_semantics=sem, vmem_limit_bytes=VMEM_LIMIT)


def _dot(a, b):
    return jnp.dot(a, b, preferred_element_type=F32)


def _layer_norm(x):
    mu = jnp.mean(x, axis=-1, keepdims=True)
    xc = x - mu
    var = jnp.mean(xc * xc, axis=-1, keepdims=True)
    return xc * lax.rsqrt(var + EPS)


def _rms_norm(x, g):
    return x * lax.rsqrt(jnp.mean(x * x, axis=-1, keepdims=True) + EPS) * g


def _rope(x, cos, sin_lo, sin_hi, half):
    n = x.shape[-1]
    return x * cos + pltpu.roll(x, n - half, 1) * sin_lo + pltpu.roll(x, half, 1) * sin_hi


def _layer_spec(arr, layer, index_map_rest, block_rest, **kw):
    return pl.BlockSpec((None,) + tuple(block_rest), lambda *g: (layer,) + tuple(index_map_rest(*g)), **kw)


def _ada_kernel(c_ref, w_ref, b_ref, o_ref):
    c = c_ref[...]
    a = (c * jax.nn.sigmoid(c)).astype(BF16)
    o_ref[0] = _dot(a, w_ref[0].astype(BF16)) + b_ref[0]


def _ada(cvec, w_ada, b_ada):
    L, D, N = w_ada.shape
    M = cvec.shape[0]
    tn = 1024
    return pl.pallas_call(
        _ada_kernel,
        out_shape=jax.ShapeDtypeStruct((L, M, N), F32),
        grid=(L, N // tn),
        in_specs=[
            pl.BlockSpec((M, D), lambda l, j: (0, 0)),
            pl.BlockSpec((1, D, tn), lambda l, j: (l, 0, j)),
            pl.BlockSpec((1, 1, tn), lambda l, j: (l, 0, j)),
        ],
        out_specs=pl.BlockSpec((1, M, tn), lambda l, j: (l, 0, j)),
        compiler_params=_cparams("parallel", "parallel"),
        name="ada",
    )(cvec, w_ada, b_ada.reshape(L, 1, N))


_A0 = 0
_B0 = _A0 + 3 * NA_HEADS * HEAD_DIM
_C0 = _B0 + (SWA_HEADS + 2 * SWA_KV_HEADS) * HEAD_DIM
_KPE0 = _C0 + MLA_Q_LORA + MLA_KV_LORA
_ABC_W = _KPE0 + MLA_ROPE
_D_W = (GQA_HEADS + 2 * GQA_KV_HEADS) * HEAD_DIM
MLA_QK = 2 * LANE


def _proj_kernel(x_ref, shift_ref, scale_ref, w_ref, wd_ref, rope_ref, gq_lora_ref, gkv_lora_ref, gq_ref, gk_ref,
                 wuq_ref, wukv_ref,
                 qa_ref, ka_ref, va_ref, qb_ref, kb_ref, vb_ref, qc_ref, kc_ref, vc_ref, qd_ref, kd_ref, vd_ref):
    h = (_layer_norm(x_ref[0]) * (1.0 + scale_ref[0]) + shift_ref[0]).astype(BF16)
    cos2, slo2, shi2 = rope_ref[0], rope_ref[1], rope_ref[2]
    cos1, slo1, shi1 = rope_ref[3], rope_ref[4], rope_ref[5]
    rope2 = lambda t: _rope(t, cos2, slo2, shi2, HEAD_DIM // 4)
    rope1 = lambda t: _rope(t, cos1, slo1, shi1, MLA_ROPE // 4)
    hd = HEAD_DIM
    sc = hd ** -0.5 * LOG2E

    def store_t(ref, lo, t):
        ref[0, lo:lo + t.shape[1], :] = t.T.astype(BF16)


    wq, wk = GQA_HEADS * hd, GQA_KV_HEADS * hd
    pd = _dot(h, wd_ref[...])
    for i in range(GQA_HEADS):
        store_t(qd_ref, i * hd, rope2(_rms_norm(pd[:, i * hd:(i + 1) * hd], gq_ref[...])) * sc)
    for i in range(GQA_KV_HEADS):
        t = _rms_norm(pd[:, wq + i * hd:wq + (i + 1) * hd], gk_ref[...])
        kd_ref[0, :, i * hd:(i + 1) * hd] = rope2(t).astype(BF16)
    store_t(vd_ref, 0, pd[:, wq + wk:])

    pc = _dot(h, w_ref[:, _C0:_KPE0])
    cq = _rms_norm(pc[:, :MLA_Q_LORA], gq_lora_ref[...]).astype(BF16)
    ckv = _rms_norm(pc[:, MLA_Q_LORA:], gkv_lora_ref[...]).astype(BF16)
    kpe = _dot(h, w_ref[:, _KPE0:_ABC_W])
    kpe = rope1(jnp.concatenate([kpe, jnp.zeros((kpe.shape[0], LANE - MLA_ROPE), F32)], axis=1)).astype(BF16)
    qup = _dot(cq, wuq_ref[...])
    kvup = _dot(ckv, wukv_ref[...])
    sc_mla = (MLA_NOPE + MLA_ROPE) ** -0.5 * LOG2E
    for i in range(MLA_HEADS):
        o = i * MLA_QK
        store_t(qc_ref, o, qup[:, o:o + LANE] * sc_mla)
        store_t(qc_ref, o + LANE, rope1(qup[:, o + LANE:o + 2 * LANE]) * sc_mla)
        kc_ref[0, :, o:o + LANE] = kvup[:, o:o + LANE].astype(BF16)
        kc_ref[0, :, o + LANE:o + 2 * LANE] = kpe
        store_t(vc_ref, i * MLA_V, kvup[:, o + LANE:o + 2 * LANE])

    wq, wk = SWA_HEADS * hd, SWA_KV_HEADS * hd
    pb = _dot(h, w_ref[:, _B0:_B0 + wq + 2 * wk])
    for i in range(SWA_HEADS):
        store_t(qb_ref, i * hd, rope2(pb[:, i * hd:(i + 1) * hd]) * sc)
    for i in range(SWA_KV_HEADS):
        kb_ref[0, :, i * hd:(i + 1) * hd] = rope2(pb[:, wq + i * hd:wq + (i + 1) * hd]).astype(BF16)
    store_t(vb_ref, 0, pb[:, wq + wk:])

    w = NA_HEADS * hd
    pa = _dot(h, w_ref[:, _A0:_A0 + 3 * w])
    store_t(qa_ref, 0, pa[:, :w] * sc)
    store_t(va_ref, 0, pa[:, 2 * w:])
    ka_ref[0] = pa[:, w:2 * w].astype(BF16)


def _proj(x, shift, scale, rope_tab, sw, layer, tm):
    B, S, D = x.shape
    widths = (NA_HEADS * HEAD_DIM,) * 3 + (SWA_HEADS * HEAD_DIM, SWA_KV_HEADS * HEAD_DIM, SWA_KV_HEADS * HEAD_DIM) \
        + (MLA_HEADS * MLA_QK, MLA_HEADS * MLA_QK, MLA_HEADS * MLA_V) \
        + (GQA_HEADS * HEAD_DIM, GQA_KV_HEADS * HEAD_DIM, GQA_KV_HEADS * HEAD_DIM)
    transposed = (True, False, True) * 4
    const = lambda arr: _layer_spec(arr, layer, lambda b, i: (0, 0), arr.shape[1:], pipeline_mode=pl.Buffered(1))
    vec = pl.BlockSpec((1, 1, D), lambda b, i: (b, 0, 0))
    out_spec = lambda n, t: (pl.BlockSpec((1, n, tm), lambda b, i: (b, 0, i)) if t
                             else pl.BlockSpec((1, tm, n), lambda b, i: (b, i, 0)))
    params = (sw["w_in_abc"], sw["w_in_d"], None, sw["mla_q_norm"], sw["mla_kv_norm"], sw["gqa_q_norm"],
              sw["gqa_k_norm"], sw["w_uq"], sw["w_ukv"])
    in_specs = [pl.BlockSpec((1, tm, D), lambda b, i: (b, i, 0)), vec, vec]
    in_specs += [pl.BlockSpec((6, tm, LANE), lambda b, i: (0, i, 0)) if p is None else const(p) for p in params]
    args = [rope_tab if p is None else p for p in params]
    return pl.pallas_call(
        _proj_kernel,
        out_shape=[jax.ShapeDtypeStruct((B, n, S) if t else (B, S, n), BF16) for n, t in zip(widths, transposed)],
        grid=(B, S // tm),
        in_specs=in_specs,
        out_specs=[out_spec(n, t) for n, t in zip(widths, transposed)],
        compiler_params=_cparams("parallel", "parallel"),
        name="proj",
    )(x, shift, scale, *args)


def _with_ones_rows(vt):
    return jnp.concatenate([vt, jnp.ones((BF16_SUBLANES, vt.shape[1]), BF16)], axis=0)


def _softmax_pv(parts, sink=None):
    m = parts[0][0].max(axis=0, keepdims=True)
    for s, _ in parts[1:]:
        m = jnp.maximum(m, s.max(axis=0, keepdims=True))
    l = 0.0
    if sink is not None:
        m = jnp.maximum(m, sink)
        l = jnp.exp2(sink - m)
    acc = None
    for s, vt in parts:
        p = jnp.exp2(s - m)
        l = l + p.sum(axis=0, keepdims=True)
        pv = _dot(vt, p.astype(BF16))
        acc = pv if acc is None else acc + pv
    return acc / l


def _pipelined_tiles(n_tiles, scores, finish, lookahead):
    pending = [scores(t) for t in range(min(lookahead, n_tiles))]
    for t in range(n_tiles):
        if t + lookahead < n_tiles:
            pending.append(scores(t + lookahead))
        finish(t, pending.pop(0))


def _na_kernel(cls_ref, ws_ref, qt_ref, k_ref, vt_ref, kc_ref, vct_ref, *rest):
    del cls_ref
    bias_refs, o_ref = rest[:-1], rest[-1]
    nk, tq = NA_KEY_ROWS * GRID_W, NA_Q_ROWS * GRID_W
    start = lambda t: pl.multiple_of(ws_ref[pl.program_id(2) * len(bias_refs) + t] * GRID_W, tq)

    def scores(t):
        qt = qt_ref[0, :, t * tq:(t + 1) * tq]
        return _dot(k_ref[0, pl.ds(start(t), nk), :], qt), _dot(kc_ref[0], qt)

    def finish(t, s):
        o = _softmax_pv([(s[0] + bias_refs[t][0, 0], vt_ref[0, :, pl.ds(start(t), nk)]), (s[1], vct_ref[0])])
        o_ref[0, t * tq:(t + 1) * tq, :] = o.T.astype(o_ref.dtype)

    _pipelined_tiles(len(bias_refs), scores, finish, NA_LOOKAHEAD)


def _na_tables(rpb, rows):
    nt = rows // NA_Q_ROWS
    R = np.arange(nt) * NA_Q_ROWS
    ws = np.clip(R - NA_WIN_R // 2, 0, rows - NA_KEY_ROWS)
    qr = (R[:, None] + np.arange(NA_Q_ROWS)[None, :])
    kr = ws[:, None] + np.arange(NA_KEY_ROWS)[None, :]
    r0 = np.clip(qr - NA_WIN_R // 2, 0, rows - NA_WIN_R)
    row_ok = (kr[:, None, :] >= r0[:, :, None]) & (kr[:, None, :] < r0[:, :, None] + NA_WIN_R)
    drow = np.clip(kr[:, None, :] - qr[:, :, None] + NA_WIN_R - 1, 0, 2 * NA_WIN_R - 2)
    geom = np.concatenate([row_ok.reshape(nt, -1).astype(np.int64), drow.reshape(nt, -1)], axis=1)
    _, first, cls = np.unique(geom, axis=0, return_index=True, return_inverse=True)
    cq = np.arange(GRID_W)
    c0 = np.clip(cq - NA_WIN_C // 2, 0, GRID_W - NA_WIN_C)
    col_ok = (cq[None, :] >= c0[:, None]) & (cq[None, :] < c0[:, None] + NA_WIN_C)
    dcol = np.clip(cq[None, :] - cq[:, None] + NA_WIN_C - 1, 0, 2 * NA_WIN_C - 2)
    nc, H = len(first), rpb.shape[0]
    cols = jnp.zeros(rpb.shape[:2] + dcol.shape, F32)
    for j in range(rpb.shape[2]):
        cols = jnp.where(jnp.asarray(dcol == j)[None, None], rpb[:, :, j][:, :, None, None] * LOG2E, cols)
    cols = jnp.where(jnp.asarray(col_ok)[None, None], cols, NEG)
    nd = rpb.shape[1]
    pad = jnp.full((H, NA_KEY_ROWS * GRID_W, GRID_W), NEG, F32)
    flat = jnp.concatenate([pad, cols.transpose(0, 1, 3, 2).reshape(H, nd * GRID_W, GRID_W), pad], axis=1)
    per_class = []
    for c in first:
        strips = []
        for q in range(NA_Q_ROWS):
            d0 = int(kr[c, 0] - qr[c, q]) + NA_WIN_R - 1 + NA_KEY_ROWS
            strip = flat[:, d0 * GRID_W:(d0 + NA_KEY_ROWS) * GRID_W, :]
            keep = np.repeat(row_ok[c, q], GRID_W)
            strips.append(jnp.where(jnp.asarray(keep)[None, :, None], strip, NEG))
        per_class.append(jnp.concatenate(strips, axis=-1))
    bias = jnp.stack(per_class)
    return bias, jnp.asarray(cls.reshape(-1), jnp.int32), jnp.asarray(ws, jnp.int32)


def _na_attention(qt, k, vt, kc, vct, rpb):
    B, _, S = qt.shape
    C = kc.shape[1]
    rows = S // GRID_W
    assert rows >= NA_KEY_ROWS and rows % NA_Q_ROWS == 0
    bias, cls, ws = _na_tables(rpb, rows)
    tq, nk, hd = NA_Q_ROWS * GRID_W, NA_KEY_ROWS * GRID_W, HEAD_DIM
    n_sub = NA_TILES_PER_STEP if (S // tq) % NA_TILES_PER_STEP == 0 else 1
    bias_spec = lambda t: pl.BlockSpec((1, 1, nk, tq), lambda b, h, i, c, w: (c[i * n_sub + t], h, 0, 0))
    grid_spec = pltpu.PrefetchScalarGridSpec(
        num_scalar_prefetch=2,
        grid=(B, NA_HEADS, S // (tq * n_sub)),
        in_specs=[
            pl.BlockSpec((1, hd, n_sub * tq), lambda b, h, i, c, w: (b, h, i)),
            pl.BlockSpec((1, S, hd), lambda b, h, i, c, w: (b, 0, h)),
            pl.BlockSpec((1, hd, S), lambda b, h, i, c, w: (b, h, 0)),
            pl.BlockSpec((1, C, hd), lambda b, h, i, c, w: (b, 0, h)),
            pl.BlockSpec((1, hd, C), lambda b, h, i, c, w: (b, h, 0)),
        ] + [bias_spec(t) for t in range(n_sub)],
        out_specs=pl.BlockSpec((1, n_sub * tq, hd), lambda b, h, i, c, w: (b, i, h)),
    )
    return pl.pallas_call(
        _na_kernel,
        out_shape=jax.ShapeDtypeStruct((B, S, NA_HEADS * hd), BF16),
        grid_spec=grid_spec,
        compiler_params=_cparams("parallel", "parallel", "arbitrary"),
        name="na_attn",
    )(cls, ws, qt, k, vt, kc, vct, *([bias] * n_sub))


def _swa_kernel(qt_ref, k_ref, vt_ref, kc_ref, vct_ref, sink_ref, o_ref, *, tq, n_sub, seq):
    nk = tq + 2 * SWA_WINDOW
    tile = lambda t: pl.program_id(2) * n_sub + t
    start = lambda t: pl.multiple_of(jnp.clip(tile(t) * tq - SWA_WINDOW, 0, seq - nk), SWA_WINDOW)

    def scores(t):
        qt = qt_ref[0, :, t * tq:(t + 1) * tq]
        return _dot(k_ref[0, pl.ds(start(t), nk), :], qt), _dot(kc_ref[0], qt)

    def finish(t, s):
        kpos = start(t) + lax.broadcasted_iota(jnp.int32, (nk, tq), 0)
        qpos = tile(t) * tq + lax.broadcasted_iota(jnp.int32, (nk, tq), 1)
        s_win = jnp.where(jnp.abs(qpos - kpos) <= SWA_WINDOW, s[0], NEG)
        o = _softmax_pv([(s_win, vt_ref[0, :, pl.ds(start(t), nk)]), (s[1], vct_ref[0])], sink=sink_ref[0])
        o_ref[0, t * tq:(t + 1) * tq, :] = o.T.astype(o_ref.dtype)

    _pipelined_tiles(n_sub, scores, finish, SWA_LOOKAHEAD)


def _swa_attention(qt, k, vt, kc, vct, sink, tq):
    B, _, S = qt.shape
    C = kc.shape[1]
    hd, G = HEAD_DIM, SWA_HEADS // SWA_KV_HEADS
    assert S >= tq + 2 * SWA_WINDOW and S % tq == 0 and tq % SWA_WINDOW == 0
    n_sub = SWA_TILES_PER_STEP if (S // tq) % SWA_TILES_PER_STEP == 0 else 1
    return pl.pallas_call(
        functools.partial(_swa_kernel, tq=tq, n_sub=n_sub, seq=S),
        out_shape=jax.ShapeDtypeStruct((B, S, SWA_HEADS * hd), BF16),
        grid=(B, SWA_HEADS, S // (tq * n_sub)),
        in_specs=[
            pl.BlockSpec((1, hd, n_sub * tq), lambda b, h, i: (b, h, i)),
            pl.BlockSpec((1, S, hd), lambda b, h, i: (b, 0, h // G)),
            pl.BlockSpec((1, hd, S), lambda b, h, i: (b, h // G, 0)),
            pl.BlockSpec((1, C, hd), lambda b, h, i: (b, 0, h // G)),
            pl.BlockSpec((1, hd, C), lambda b, h, i: (b, h // G, 0)),
            pl.BlockSpec((1, 1, 1), lambda b, h, i: (h, 0, 0)),
        ],
        out_specs=pl.BlockSpec((1, n_sub * tq, hd), lambda b, h, i: (b, i, h)),
        compiler_params=_cparams("parallel", "parallel", "arbitrary"),
        name="swa_attn",
    )(qt, k, vt, kc, vct, (sink * LOG2E).reshape(SWA_HEADS, 1, 1).astype(F32))


def _dense_kernel(*refs, tk, seq, n_ctx, has_sink):
    refs = list(refs)
    qt_ref, k_ref, vt_ref = refs[:3]
    o_ref = refs[-1]
    kc_ref, vct_ref = refs[3:5] if n_ctx else (None, None)
    qt = qt_ref[0]
    tq = qt.shape[1]
    dv = vt_ref.shape[1]
    if has_sink:
        m = jnp.broadcast_to(refs[-2][0], (1, tq))
        acc = jnp.concatenate([jnp.zeros((dv, tq), F32), jnp.ones((BF16_SUBLANES, tq), F32)], axis=0)
    else:
        m = jnp.full((1, tq), -jnp.inf, F32)
        acc = jnp.zeros((dv + BF16_SUBLANES, tq), F32)

    chunks = [(k_ref, vt_ref, j * tk, tk) for j in range(seq // tk)]
    if n_ctx:
        chunks.append((kc_ref, vct_ref, 0, n_ctx))
    scores = lambda c: _dot(c[0][0, c[2]:c[2] + c[3], :], qt)

    pending = [scores(c) for c in chunks[:DENSE_LOOKAHEAD]]
    for j, c in enumerate(chunks):
        if j + DENSE_LOOKAHEAD < len(chunks):
            pending.append(scores(chunks[j + DENSE_LOOKAHEAD]))
        s = pending.pop(0)
        m_new = jnp.maximum(m, s.max(axis=0, keepdims=True))
        alpha = jnp.exp2(m - m_new)
        p = jnp.exp2(s - m_new).astype(BF16)
        acc = alpha * acc + _dot(_with_ones_rows(c[1][0, :, c[2]:c[2] + c[3]]), p)
        m = m_new
    o_ref[0] = (acc[:dv] / acc[dv:dv + 1]).T.astype(o_ref.dtype)


def _dense_attention(qt, k, vt, kc, vct, n_heads, group, tq, tk, sink=None):
    B, _, S = qt.shape
    Sk = k.shape[1]
    dk = qt.shape[1] // n_heads
    dv = vt.shape[1] // (n_heads // group)
    assert S % tq == 0 and Sk % tk == 0
    n_ctx = 0 if kc is None else kc.shape[1]
    in_specs = [
        pl.BlockSpec((1, dk, tq), lambda b, h, i: (b, h, i)),
        pl.BlockSpec((1, Sk, dk), lambda b, h, i: (b, 0, h // group)),
        pl.BlockSpec((1, dv, Sk), lambda b, h, i: (b, h // group, 0)),
    ]
    args = [qt, k, vt]
    if n_ctx:
        in_specs += [
            pl.BlockSpec((1, n_ctx, dk), lambda b, h, i: (b, 0, h // group)),
            pl.BlockSpec((1, dv, n_ctx), lambda b, h, i: (b, h // group, 0)),
        ]
        args += [kc, vct]
    if sink is not None:
        in_specs.append(pl.BlockSpec((1, 1, 1), lambda b, h, i: (h, 0, 0)))
        args.append((sink * LOG2E).reshape(n_heads, 1, 1).astype(F32))
    return pl.pallas_call(
        functools.partial(_dense_kernel, tk=tk, seq=Sk, n_ctx=n_ctx, has_sink=sink is not None),
        out_shape=jax.ShapeDtypeStruct((B, S, n_heads * dv), BF16),
        grid=(B, n_heads, S // tq),
        in_specs=in_specs,
        out_specs=pl.BlockSpec((1, tq, dv), lambda b, h, i: (b, i, h)),
        compiler_params=_cparams("parallel", "parallel", "arbitrary"),
        name="dense_attn",
    )(*args)


def _out_kernel(oa_ref, ob_ref, oc_ref, od_ref, w_ref, x_ref, gate_ref, g_ref, b_ref, o_ref, *, alpha, sub):
    for r0 in range(0, x_ref.shape[1], sub):
        rows = slice(r0, r0 + sub)
        y = None
        o = 0
        for r in (oa_ref, ob_ref, oc_ref, od_ref):
            n = r.shape[2]
            t = _dot(r[0, rows, :], w_ref[o:o + n, :])
            y = t if y is None else y + t
            o += n
        z = alpha * x_ref[0, rows, :] + gate_ref[0] * y
        o_ref[0, rows, :] = _layer_norm(z) * g_ref[...] + b_ref[...]


def _out_proj(mix, x, gate, sw, layer, alpha, tm):
    B, S, D = x.shape
    const = lambda arr: _layer_spec(arr, layer, lambda bb, i: (0, 0), arr.shape[1:], pipeline_mode=pl.Buffered(1))
    return pl.pallas_call(
        functools.partial(_out_kernel, alpha=alpha, sub=min(OUT_SUB_ROWS, tm)),
        out_shape=jax.ShapeDtypeStruct((B, S, D), F32),
        grid=(B, S // tm),
        in_specs=[pl.BlockSpec((1, tm, m.shape[2]), lambda bb, i: (bb, i, 0)) for m in mix] + [
            const(sw["w_out"]),
            pl.BlockSpec((1, tm, D), lambda bb, i: (bb, i, 0)),
            pl.BlockSpec((1, 1, D), lambda bb, i: (bb, 0, 0)),
            const(sw["ln1_g"]), const(sw["ln1_b"]),
        ],
        out_specs=pl.BlockSpec((1, tm, D), lambda bb, i: (bb, i, 0)),
        compiler_params=_cparams("parallel", "parallel"),
        name="out_proj",
    )(*mix, sw["w_out"], x, gate, sw["ln1_g"], sw["ln1_b"])


HALO = BF16_SUBLANES


def _ffn_kernel(x_hbm, xp_ref, xn_ref, shift_ref, scale_ref, gate_ref, wg_ref, wu_ref, cw_ref, cb_ref, wd_ref,
                g_ref, b_ref, o_ref, h_ref, x_buf, x_sem, *, alpha, tm, seg):
    bb, i, f = pl.program_id(0), pl.program_id(1), pl.program_id(2)
    n_b, n_i = pl.num_programs(0), pl.num_programs(1)
    acc_ref = o_ref.at[0]

    def x_copy(b_idx, i_idx):
        return pltpu.make_async_copy(x_hbm.at[b_idx, pl.ds(i_idx * tm, tm), :], x_buf, x_sem)

    @pl.when(f == 0)
    def _():
        @pl.when((bb == 0) & (i == 0))
        def _():
            x_copy(0, 0).start()

        x_copy(bb, i).wait()
        xv = x_buf[...]
        mod = lambda t: _layer_norm(t) * (1.0 + scale_ref[0]) + shift_ref[0]
        keep_p = ((i * tm) % seg != 0).astype(F32)
        keep_n = (((i + 1) * tm) % seg != 0).astype(F32)
        h_ref[0:HALO, :] = (mod(xp_ref[0]) * keep_p).astype(BF16)
        h_ref[HALO:HALO + tm, :] = mod(xv).astype(BF16)
        h_ref[HALO + tm:, :] = (mod(xn_ref[0]) * keep_n).astype(BF16)
        acc_ref[...] = alpha * xv

    @pl.when((f == 1) & ((bb < n_b - 1) | (i < n_i - 1)))
    def _():
        wrap = i == n_i - 1
        x_copy(jnp.where(wrap, bb + 1, bb), jnp.where(wrap, 0, i + 1)).start()

    n = tm + 2 * HALO
    gt = _dot(h_ref[...], wg_ref[...])
    up = _dot(h_ref[HALO:HALO + tm, :], wu_ref[...])
    prev, nxt = pltpu.roll(gt, 1, 0), pltpu.roll(gt, n - 1, 0)
    if seg < tm:
        pos = (lax.broadcasted_iota(jnp.int32, (n, 1), 0) + (seg - HALO)) % seg
        prev = jnp.where(pos == 0, 0.0, prev)
        nxt = jnp.where(pos == seg - 1, 0.0, nxt)
    a = (prev * cw_ref[0:1, :] + gt * cw_ref[1:2, :] + nxt * cw_ref[2:3, :] + cb_ref[...])[HALO:HALO + tm]
    y = (a * jax.nn.sigmoid(a) * up).astype(BF16)
    acc_ref[...] += gate_ref[0] * _dot(y, wd_ref[...])

    @pl.when(f == pl.num_programs(2) - 1)
    def _():
        o_ref[0] = _layer_norm(acc_ref[...]) * g_ref[...] + b_ref[...]


def _ffn(x, shift, scale, gate, sw, layer, alpha, tm, tf, seg=None):
    B, S, D = x.shape
    seg = S if seg is None else seg
    F = sw["ffn_w_gate"].shape[2]
    assert S % tm == 0 and F % tf == 0 and tm % HALO == 0 and F // tf >= 2
    assert S % seg == 0 and (seg % tm == 0 or tm % seg == 0)
    nh = tm // HALO
    vec = pl.BlockSpec((1, 1, D), lambda bb, i, f: (bb, 0, 0))
    cols = lambda arr, rows: _layer_spec(arr, layer, lambda bb, i, f: (0, f), (rows, tf))
    const = lambda arr: _layer_spec(arr, layer, lambda bb, i, f: (0, 0), arr.shape[1:])
    return pl.pallas_call(
        functools.partial(_ffn_kernel, alpha=alpha, tm=tm, seg=seg),
        out_shape=jax.ShapeDtypeStruct((B, S, D), F32),
        grid=(B, S // tm, F // tf),
        in_specs=[
            pl.BlockSpec(memory_space=pl.ANY),
            pl.BlockSpec((1, HALO, D), lambda bb, i, f: (bb, jnp.maximum(i * nh - 1, 0), 0)),
            pl.BlockSpec((1, HALO, D), lambda bb, i, f: (bb, jnp.minimum((i + 1) * nh, S // HALO - 1), 0)),
            vec, vec, vec,
            cols(sw["ffn_w_gate"], D), cols(sw["ffn_w_up"], D), cols(sw["ffn_conv_w"], CONV_W),
            cols(sw["ffn_conv_b"], 1),
            _layer_spec(sw["ffn_w_down"], layer, lambda bb, i, f: (f, 0), (tf, D)),
            const(sw["ln2_g"]), const(sw["ln2_b"]),
        ],
        out_specs=pl.BlockSpec((1, tm, D), lambda bb, i, f: (bb, i, 0)),
        scratch_shapes=[pltpu.VMEM((tm + 2 * HALO, D), BF16), pltpu.VMEM((tm, D), F32), pltpu.SemaphoreType.DMA(())],
        compiler_params=_cparams("arbitrary", "arbitrary", "arbitrary"),
        name="conv_ffn",
    )(x, x, x, shift, scale, gate, sw["ffn_w_gate"], sw["ffn_w_up"], sw["ffn_conv_w"], sw["ffn_conv_b"],
      sw["ffn_w_down"], sw["ln2_g"], sw["ln2_b"])


def _rope_tables(S):
    t = jnp.arange(S)
    row, col = (t // GRID_W).astype(F32), (t % GRID_W).astype(F32)
    lane = np.arange(LANE)

    def tables(dim):
        half = dim // 4
        inv_freq = ROPE_THETA ** (-jnp.arange(half, dtype=F32) / half)
        used = lane < dim
        pos = jnp.where(jnp.asarray((lane % dim) < dim // 2)[None, :], row[:, None], col[:, None])
        ang = pos * inv_freq[jnp.asarray(lane % half)][None, :]
        cos, sin = jnp.cos(ang), jnp.sin(ang)
        lo = jnp.asarray(used & ((lane % (2 * half)) < half))[None, :]
        hi = jnp.asarray(used & ((lane % (2 * half)) >= half))[None, :]
        return [jnp.where(jnp.asarray(used)[None, :], cos, 1.0), jnp.where(lo, -sin, 0.0), jnp.where(hi, sin, 0.0)]

    return jnp.stack(tables(HEAD_DIM) + tables(MLA_ROPE)).astype(F32)


def _identity_rope(C):
    one, zero = jnp.ones((C, LANE), F32), jnp.zeros((C, LANE), F32)
    return jnp.stack([one, zero, zero, one, zero, zero])


def _pad_w_uq(w):
    L, r, _ = w.shape
    w = w.reshape(L, r, MLA_HEADS, MLA_NOPE + MLA_ROPE)
    w = jnp.concatenate([w, jnp.zeros((L, r, MLA_HEADS, MLA_QK - MLA_NOPE - MLA_ROPE), w.dtype)], axis=3)
    return w.reshape(L, r, MLA_HEADS * MLA_QK).astype(BF16)


def kernel(x, c, ctx, c_ctx, w_ada, b_ada, w_in, na_rpb, swa_sink, mla_q_norm, mla_kv_norm, mla_w_uq, mla_w_ukv,
           gqa_q_norm, gqa_k_norm, w_out, ln1_g, ln1_b, ffn_w_gate, ffn_w_up, ffn_conv_w, ffn_conv_b, ffn_w_down,
           ln2_g, ln2_b):
    B, S, D = x.shape
    C = ctx.shape[1]
    depth = w_ada.shape[0]
    alpha = (2 * depth) ** 0.25

    cvec = jnp.concatenate([c, c_ctx[None, :], jnp.zeros((8 - B - 1, D), F32)], axis=0)
    ada = _ada(cvec, w_ada, b_ada)
    rope_x, rope_c = _rope_tables(S), _identity_rope(C)

    row = lambda a: a[:, None, :]
    w_in_b = w_in.astype(BF16)
    sw = {
        "w_in_abc": w_in_b[..., :_ABC_W], "w_in_d": w_in_b[..., _ABC_W:],
        "w_uq": _pad_w_uq(mla_w_uq), "w_ukv": mla_w_ukv.astype(BF16),
        "mla_q_norm": row(mla_q_norm), "mla_kv_norm": row(mla_kv_norm),
        "gqa_q_norm": row(gqa_q_norm), "gqa_k_norm": row(gqa_k_norm),
        "w_out": w_out.astype(BF16), "ln1_g": row(ln1_g), "ln1_b": row(ln1_b),
        "ffn_w_gate": ffn_w_gate.astype(BF16), "ffn_w_up": ffn_w_up.astype(BF16), "ffn_conv_w": ffn_conv_w,
        "ffn_conv_b": row(ffn_conv_b), "ffn_w_down": ffn_w_down.astype(BF16), "ln2_g": row(ln2_g), "ln2_b": row(ln2_b),
    }

    tm_proj = min(256, S)
    tm_out = min(512, S)
    tm_ffn, tf = min(1024, S), 512
    tq_dense, tk_dense = min(1024, S), min(512, S)
    tq_swa = min(256, S - 2 * SWA_WINDOW)
    g_swa, g_gqa = SWA_HEADS // SWA_KV_HEADS, GQA_HEADS // GQA_KV_HEADS

    for i in range(depth):
        need_ctx = i < depth - 1
        mx = ada[i, :B].reshape(B, 1, 6, D)
        mod_x = [mx[:, :, j] for j in range(6)]
        mod_c = [jnp.broadcast_to(ada[i, B].reshape(1, 1, 6, D)[:, :, j], (B, 1, D)) for j in range(6)]

        qa, ka, va, qb, kb, vb, qc, kc, vc, qd, kd, vd = _proj(x, mod_x[0], mod_x[1], rope_x, sw, i, tm_proj)
        qa_c, ka_c, va_c, qb_c, kb_c, vb_c, qc_c, kc_c, vc_c, qd_c, kd_c, vd_c = _proj(
            ctx, mod_c[0], mod_c[1], rope_c, sw, i, C)

        mix_x = (
            _na_attention(qa, ka, va, ka_c, va_c, na_rpb[i]),
            _swa_attention(qb, kb, vb, kb_c, vb_c, swa_sink[i], tq_swa),
            _dense_attention(qc, kc, vc, kc_c, vc_c, MLA_HEADS, 1, tq_dense, tk_dense),
            _dense_attention(qd, kd, vd, kd_c, vd_c, GQA_HEADS, g_gqa, tq_dense, tk_dense),
        )
        x = _out_proj(mix_x, x, mod_x[2], sw, i, alpha, tm_out)
        if need_ctx:
            mix_c = (
                _dense_attention(qa_c, ka_c, va_c, None, None, NA_HEADS, 1, C, C),
                _dense_attention(qb_c, kb_c, vb_c, None, None, SWA_HEADS, g_swa, C, C, sink=swa_sink[i]),
                _dense_attention(qc_c, kc_c, vc_c, None, None, MLA_HEADS, 1, C, C),
                _dense_attention(qd_c, kd_c, vd_c, None, None, GQA_HEADS, g_gqa, C, C),
            )
            ctx = _out_proj(mix_c, ctx, mod_c[2], sw, i, alpha, C)

        x = _ffn(x, mod_x[3], mod_x[4], mod_x[5], sw, i, alpha, tm_ffn, tf)
        if need_ctx:
            flat = _ffn(ctx.reshape(1, B * C, D), mod_c[3][:1], mod_c[4][:1], mod_c[5][:1], sw, i, alpha,
                        min(tm_ffn, B * C), tf, seg=C)
            ctx = flat.reshape(B, C, D)
    return x
```
